```python
import jax, jax.numpy as jnp
from jax import lax
import numpy as np

D_MODEL = 1024
BATCH = 8
SEQ = 2048
DEPTH = 1
DEC_BATCH = 128
DEC_SEQ = 4
PAST_LEN = 8192
PAGE_SIZE = 128

HEAD_DIM = 64
N_HEADS_A = 8
N_HEADS_B = 8
WIDTH_A = N_HEADS_A * HEAD_DIM
WIDTH_B = N_HEADS_B * HEAD_DIM
MIX_WIDTH = WIDTH_A + WIDTH_B
IN_WIDTH = 2 * WIDTH_A + 3 * WIDTH_B
CHUNK = 128
DILATED_PATTERNS = ((128, 1), (512, 4), (2048, 16))
MAX_WINDOW = max(w for w, _ in DILATED_PATTERNS)
ATTN_BLOCK = 128
D_FF = ((8 * D_MODEL + 3 * 256 - 1) // (3 * 256)) * 256
EPS = 1e-6

kernel_name = "hymba_gmlp_dilated_swa_decode"

F32 = jnp.float32


def rms_norm(x, g):
    xf = x.astype(F32)
    y = xf * lax.rsqrt(jnp.mean(xf * xf, axis=-1, keepdims=True) + EPS)
    return (y * g.astype(F32)).astype(x.dtype)


def layer_norm(x, g, b):
    xf = x.astype(F32)
    mu = jnp.mean(xf, axis=-1, keepdims=True)
    xc = xf - mu
    y = xc * lax.rsqrt(jnp.mean(xc * xc, axis=-1, keepdims=True) + EPS)
    return y * g.astype(F32) + b.astype(F32)


def _in_proj(x, g_attn, w_in):
    z = rms_norm(x, g_attn) @ w_in
    o1 = WIDTH_A
    o2 = o1 + WIDTH_A
    o3 = o2 + WIDTH_B
    o4 = o3 + WIDTH_B
    return z[..., :o1], z[..., o1:o2], z[..., o2:o3], z[..., o3:o4], z[..., o4:]


def _gmlp_features(zu, zv, ln_g, ln_b):
    u = jax.nn.gelu(zu.astype(F32))
    vn = layer_norm(jax.nn.gelu(zv.astype(F32)), ln_g, ln_b)
    return u, vn


def _spatial_gate(u, vn, w_s, b_s):
    N, L, _ = vn.shape
    c = min(CHUNK, L)
    nc = L // c
    tri = jnp.tril(jnp.ones((c, c), dtype=bool))
    ws = jnp.where(tri, w_s[:, :c, :c].astype(F32), 0.0)
    vr = vn.reshape(N, nc, c, N_HEADS_A, HEAD_DIM)
    s = jnp.einsum('hij,bnjhd->bnihd', ws, vr)
    s = s + b_s[:, :c].astype(F32).T[None, None, :, :, None]
    return u * s.reshape(N, L, WIDTH_A)


def _qkv(zq, zk, zv, g_q, g_k):
    shp = zq.shape[:-1] + (N_HEADS_B, HEAD_DIM)
    q = rms_norm(zq.reshape(shp), g_q).astype(F32) * (HEAD_DIM ** -0.5)
    k = rms_norm(zk.reshape(shp), g_k).astype(F32)
    v = zv.reshape(shp).astype(F32)
    return q, k, v


def _band_stats(q, k, v, n_back):
    N, L, H, Dh = q.shape
    blk = min(ATTN_BLOCK, L)
    nb = -(-L // blk)
    Lp = nb * blk
    pad = ((0, 0), (0, Lp - L), (0, 0), (0, 0))
    q, k, v = jnp.pad(q, pad), jnp.pad(k, pad), jnp.pad(v, pad)
    qb = q.reshape(N, nb, blk, H, Dh)

    def with_prev(a):
        ab = a.reshape(N, nb, blk, H, Dh)
        prev = jnp.pad(ab, ((0, 0), (1, 0), (0, 0), (0, 0), (0, 0)))[:, :-1]
        return jnp.concatenate([prev, ab], axis=2)

    kk, vv = with_prev(k), with_prev(v)
    s = jnp.einsum('nbihd,nbjhd->nbhij', qb, kk)
    qpos = jnp.arange(nb)[:, None] * blk + jnp.arange(blk)[None, :]
    kpos = jnp.arange(nb)[:, None] * blk - blk + jnp.arange(2 * blk)[None, :]
    dist = qpos[:, :, None] - kpos[:, None, :]
    valid = (dist >= 0) & (dist <= n_back) & (kpos[:, None, :] >= 0)
    s = jnp.where(valid[None, :, None], s, -jnp.inf)
    m = jnp.max(s, axis=-1)
    p = jnp.exp(s - m[..., None])
    l = jnp.sum(p, axis=-1)
    acc = jnp.einsum('nbhij,nbjhd->nbihd', p, vv).reshape(N, Lp, H, Dh)[:, :L]
    m = jnp.swapaxes(m, 2, 3).reshape(N, Lp, H)[:, :L]
    l = jnp.swapaxes(l, 2, 3).reshape(N, Lp, H)[:, :L]
    return m, l, acc


def _dilated_branch_prompt(q, k, v, window, dil):
    B, S, H, Dh = q.shape
    L = S // dil

    def to_res(a):
        return jnp.swapaxes(a.reshape(B, L, dil, H, Dh), 1, 2).reshape(B * dil, L, H, Dh)

    def from_res(a):
        a = a.reshape((B, dil) + a.shape[1:])
        return jnp.swapaxes(a, 1, 2).reshape((B, S) + a.shape[3:])

    m, l, acc = _band_stats(to_res(q), to_res(k), to_res(v), window // dil)
    return from_res(m), from_res(l), from_res(acc)


def _dilated_branch_sample(q, k_all, v_all, window, dil):
    T = q.shape[1]
    Lb = k_all.shape[1] - T
    n_keys = window // dil + 1
    idx = Lb + jnp.arange(T)[:, None] - dil * jnp.arange(n_keys)[None, :]
    valid = (idx >= 0) & (idx + (PAST_LEN - Lb) >= 0)
    idx_c = jnp.clip(idx, 0)
    kg = k_all[:, idx_c]
    vg = v_all[:, idx_c]
    s = jnp.einsum('nthd,ntjhd->nthj', q, kg)
    s = jnp.where(valid[None, :, None, :], s, -jnp.inf)
    m = jnp.max(s, axis=-1)
    p = jnp.exp(s - m[..., None])
    l = jnp.sum(p, axis=-1)
    acc = jnp.einsum('nthj,ntjhd->nthd', p, vg)
    return m, l, acc


def _combine(stats):
    m_all = jnp.stack([st[0] for st in stats])
    l_all = jnp.stack([st[1] for st in stats])
    acc_all = jnp.stack([st[2] for st in stats])
    w = jnp.exp(m_all - jnp.max(m_all, axis=0, keepdims=True))
    num = jnp.sum(w[..., None] * acc_all, axis=0)
    den = jnp.sum(w * l_all, axis=0)
    return num / den[..., None]


def _finish(x, a_out, b_out, g_out_a, g_out_b, w_o, g_ffn, w_gate, w_up, w_down):
    lead = b_out.shape[:2]
    mix = jnp.concatenate([rms_norm(a_out, g_out_a), rms_norm(b_out.reshape(lead + (WIDTH_B,)), g_out_b)], axis=-1)
    x = x + mix.astype(x.dtype) @ w_o
    h = rms_norm(x, g_ffn)
    return x + (jax.nn.silu(h @ w_gate) * (h @ w_up)) @ w_down


def setup_inputs(seed: int = 0) -> dict:
    key = jax.random.key(seed)
    ks = jax.random.split(key, 24)
    win_buf = min(MAX_WINDOW, PAST_LEN)
    nrm = lambda k, shp, s: jax.random.normal(k, shp, F32) * s
    gain = lambda k, shp: 1.0 + 0.05 * jax.random.normal(k, shp, F32)
    return {
        "x_prompt": nrm(ks[0], (BATCH, SEQ, D_MODEL), 1.0),
        "x_sample": nrm(ks[1], (DEC_BATCH, DEC_SEQ, D_MODEL), 1.0),
        "cache_k": nrm(ks[2], (DEPTH, DEC_BATCH, win_buf, N_HEADS_B, HEAD_DIM), 1.0),
        "cache_v": nrm(ks[3], (DEPTH, DEC_BATCH, win_buf, N_HEADS_B, HEAD_DIM), 1.0),
        "g_attn": gain(ks[4], (DEPTH, D_MODEL)),
        "w_in": nrm(ks[5], (DEPTH, D_MODEL, IN_WIDTH), D_MODEL ** -0.5),
        "ln_v_g": gain(ks[6], (DEPTH, WIDTH_A)),
        "ln_v_b": nrm(ks[7], (DEPTH, WIDTH_A), 0.02),
        "w_s": nrm(ks[8], (DEPTH, N_HEADS_A, CHUNK, CHUNK), CHUNK ** -0.5),
        "b_s": 1.0 + nrm(ks[9], (DEPTH, N_HEADS_A, CHUNK), 0.02),
        "g_q": gain(ks[10], (DEPTH, HEAD_DIM)),
        "g_k": gain(ks[11], (DEPTH, HEAD_DIM)),
        "g_out_a": gain(ks[12], (DEPTH, WIDTH_A)),
        "g_out_b": gain(ks[13], (DEPTH, WIDTH_B)),
        "w_o": nrm(ks[14], (DEPTH, MIX_WIDTH, D_MODEL), MIX_WIDTH ** -0.5),
        "g_ffn": gain(ks[15], (DEPTH, D_MODEL)),
        "w_gate": nrm(ks[16], (DEPTH, D_MODEL, D_FF), D_MODEL ** -0.5),
        "w_up": nrm(ks[17], (DEPTH, D_MODEL, D_FF), D_MODEL ** -0.5),
        "w_down": nrm(ks[18], (DEPTH, D_FF, D_MODEL), D_FF ** -0.5),
    }


def reference(x_prompt, x_sample, cache_k, cache_v, g_attn, w_in, ln_v_g, ln_v_b, w_s, b_s,
              g_q, g_k, g_out_a, g_out_b, w_o, g_ffn, w_gate, w_up, w_down):
    xp, xs = x_prompt, x_sample
    win_p = min(MAX_WINDOW, xp.shape[1])
    kp_l, vp_l, ksm_l, vsm_l, cs_l = [], [], [], [], []
    for l in range(DEPTH):
        zu, zv, zq, zk, zvb = _in_proj(xp, g_attn[l], w_in[l])
        u, vn = _gmlp_features(zu, zv, ln_v_g[l], ln_v_b[l])
        a_out = _spatial_gate(u, vn, w_s[l], b_s[l])
        q, k, v = _qkv(zq, zk, zvb, g_q[l], g_k[l])
        b_out = _combine([_dilated_branch_prompt(q, k, v, wd, dl) for wd, dl in DILATED_PATTERNS])
        new_xp = _finish(xp, a_out, b_out, g_out_a[l], g_out_b[l], w_o[l], g_ffn[l], w_gate[l], w_up[l], w_down[l])
        kp_l.append(k[:, -win_p:].astype(xp.dtype))
        vp_l.append(v[:, -win_p:].astype(xp.dtype))

        zu, zv, zq, zk, zvb = _in_proj(xs, g_attn[l], w_in[l])
        u, vn = _gmlp_features(zu, zv, ln_v_g[l], ln_v_b[l])
        a_out = _spatial_gate(u, vn, w_s[l], b_s[l])
        q, k, v = _qkv(zq, zk, zvb, g_q[l], g_k[l])
        k_all = jnp.concatenate([cache_k[l].astype(F32), k], axis=1)
        v_all = jnp.concatenate([cache_v[l].astype(F32), v], axis=1)
        b_out = _combine([_dilated_branch_sample(q, k_all, v_all, wd, dl) for wd, dl in DILATED_PATTERNS])
        new_xs = _finish(xs, a_out, b_out, g_out_a[l], g_out_b[l], w_o[l], g_ffn[l], w_gate[l], w_up[l], w_down[l])
        ksm_l.append(k.astype(xs.dtype))
        vsm_l.append(v.astype(xs.dtype))
        cs_l.append(vn.astype(xs.dtype))
        xp, xs = new_xp, new_xs
    return (xp, xs, jnp.stack(kp_l), jnp.stack(vp_l), jnp.stack(ksm_l), jnp.stack(vsm_l), jnp.stack(cs_l))
```

```python
import functools

import numpy as np
import jax
import jax.numpy as jnp
from jax import lax
from jax.experimental import pallas as pl
from jax.experimental.pallas import tpu as pltpu

F32 = jnp.float32
BF16 = jnp.bfloat16

HEAD_DIM = 64
N_HEADS = 8
WIDTH = N_HEADS * HEAD_DIM
CHUNK = 128
DILATED_PATTERNS = ((128, 1), (512, 4), (2048, 16))
EPS = 1e-6
PAST_LEN = 8192
NEG_BIG = -1e30

LANES = 128
N_SLABS = WIDTH // LANES
VMEM_LIMIT = 56 * 1024 * 1024


def _dot(a, b):
    return jnp.dot(a, b, preferred_element_type=F32)


def _dot_nt(a, b):
    return lax.dot_general(a, b, (((1,), (1,)), ((), ())), preferred_element_type=F32)


def _const_spec(shape):
    return pl.BlockSpec(shape, lambda *_: (0,) * len(shape), pipeline_mode=pl.Buffered(1))


def _head_rms(z, bd_ref, g_ref):
    sq = z * z
    hi = sq.astype(BF16)
    lo = (sq - hi.astype(F32)).astype(BF16)
    msq = _dot(hi, bd_ref[...]) + _dot(lo, bd_ref[...])
    return z * lax.rsqrt(msq + EPS) * g_ref[...]


def _features(x_ref, g_attn_ref, w_in_ref, ln_g_ref, ln_b_ref, gq_ref, gk_ref, bd_ref):
    x = x_ref[...]
    ms = jnp.mean(x * x, axis=-1, keepdims=True)
    xn = (x * lax.rsqrt(ms + EPS) * g_attn_ref[...]).astype(BF16)

    def proj(j):
        return _dot(xn, w_in_ref[:, j * WIDTH:(j + 1) * WIDTH])

    u = jax.nn.gelu(proj(0))
    gv = jax.nn.gelu(proj(1))
    mu = jnp.mean(gv, axis=-1, keepdims=True)
    xc = gv - mu
    var = jnp.mean(xc * xc, axis=-1, keepdims=True)
    vn = xc * lax.rsqrt(var + EPS) * ln_g_ref[...] + ln_b_ref[...]
    q = _head_rms(proj(2), bd_ref, gq_ref)
    k = _head_rms(proj(3), bd_ref, gk_ref)
    v = proj(4)
    return u, vn, q, k, v


def _group_norm_rows(slabs, g_ref):
    ssq = sum(jnp.sum(a * a, axis=-1, keepdims=True) for a in slabs)
    r = lax.rsqrt(ssq * (1.0 / WIDTH) + EPS)
    return [a * r * g_ref[:, s * LANES:(s + 1) * LANES] for s, a in enumerate(slabs)]


def _in_proj_prompt_kernel(x_ref, g_attn_ref, w_in_ref, ln_g_ref, ln_b_ref, gq_ref, gk_ref, bd_ref,
                           ws_ref, bs_ref, g_out_a_ref,
                           q_ref, k_ref, v_ref, kout_ref, vout_ref, mixa_ref):
    u, vn, q, k, v = _features(x_ref, g_attn_ref, w_in_ref, ln_g_ref, ln_b_ref, gq_ref, gk_ref, bd_ref)
    tm = u.shape[0]
    first_head = lax.broadcasted_iota(jnp.int32, (CHUNK, LANES), 1) < HEAD_DIM
    for c in range(tm // CHUNK):
        rows = slice(c * CHUNK, (c + 1) * CHUNK)
        vn_c = vn[rows].astype(BF16)
        slabs = []
        for s in range(N_SLABS):
            cols = slice(s * LANES, (s + 1) * LANES)
            g0 = _dot(ws_ref[2 * s], vn_c[:, cols])
            g1 = _dot(ws_ref[2 * s + 1], vn_c[:, cols])
            gate = jnp.where(first_head, g0, g1) + bs_ref[:, cols]
            slabs.append(u[rows, cols] * gate)
        for s, a in enumerate(_group_norm_rows(slabs, g_out_a_ref)):
            mixa_ref[rows, s * LANES:(s + 1) * LANES] = a.astype(BF16)
    for s in range(N_SLABS):
        cols = slice(s * LANES, (s + 1) * LANES)
        q_ref[s] = q[:, cols]
        k_ref[s] = k[:, cols]
        v_ref[s] = v[:, cols]
    kout_ref[...] = k
    vout_ref[...] = v


def _in_proj_sample_kernel(x_ref, g_attn_ref, w_in_ref, ln_g_ref, ln_b_ref, gq_ref, gk_ref, bd_ref,
                           coef_ref, bias_ref, g_out_a_ref,
                           q_ref, k_ref, v_ref, vn_ref, mixa_ref, *, seq):
    u, vn, q, k, v = _features(x_ref, g_attn_ref, w_in_ref, ln_g_ref, ln_b_ref, gq_ref, gk_ref, bd_ref)
    tm = u.shape[0]
    sub = coef_ref.shape[1]
    gate = jnp.zeros((tm // sub, sub, WIDTH), F32) + bias_ref[...]
    for d in range(seq):
        shifted = vn if d == 0 else pltpu.roll(vn, d, axis=0)
        gate = gate + shifted.reshape(tm // sub, sub, WIDTH) * coef_ref[d]
    a = u * gate.reshape(tm, WIDTH)
    slabs = _group_norm_rows([a[:, s * LANES:(s + 1) * LANES] for s in range(N_SLABS)], g_out_a_ref)
    for s, a_s in enumerate(slabs):
        mixa_ref[:, s * LANES:(s + 1) * LANES] = a_s.astype(BF16)
    q_ref[...] = q
    k_ref[...] = k
    v_ref[...] = v
    vn_ref[...] = vn


def _prompt_attn_kernel(q_ref, k_ref, v_ref, o_ref, m_sc, l_sc, *, n_blocks):
    j = pl.program_id(1)

    @pl.when(j == 0)
    def _init():
        m_sc[...] = jnp.full(m_sc.shape, NEG_BIG, F32)
        l_sc[...] = jnp.zeros(l_sc.shape, F32)
        o_ref[...] = jnp.zeros(o_ref.shape, F32)

    first_head = lax.broadcasted_iota(jnp.int32, (CHUNK, LANES), 1) < HEAD_DIM
    row = lax.broadcasted_iota(jnp.int32, (CHUNK, 2 * CHUNK), 0)
    col = lax.broadcasted_iota(jnp.int32, (CHUNK, 2 * CHUNK), 1)
    band = (col >= row) & (col <= row + CHUNK)
    causal = (lax.broadcasted_iota(jnp.int32, (CHUNK, CHUNK), 1)
              <= lax.broadcasted_iota(jnp.int32, (CHUNK, CHUNK), 0))

    for window, dil in DILATED_PATTERNS:
        blocks_per_class = n_blocks // dil
        assert window // dil == CHUNK
        if blocks_per_class > 1:
            r = j // blocks_per_class
            jb = j % blocks_per_class
            start = r + (dil * CHUNK) * jb
            prev_start = r + (dil * CHUNK) * jnp.maximum(jb - 1, 0)
            mask = band & ((col >= CHUNK) | (jb > 0))
        else:
            start = j
            mask = causal

        def rows_at(st):
            return pl.ds(st, CHUNK) if dil == 1 else pl.ds(st, CHUNK, stride=dil)

        cur = rows_at(start)
        for s in range(N_SLABS):
            qb = q_ref[s, cur, :].astype(BF16)
            kk = k_ref[s, cur, :]
            vv = v_ref[s, cur, :]
            if blocks_per_class > 1:
                prev = rows_at(prev_start)
                kk = jnp.concatenate([k_ref[s, prev, :], kk], axis=0)
                vv = jnp.concatenate([v_ref[s, prev, :], vv], axis=0)
            kk = kk.astype(BF16)
            vv = vv.astype(BF16)
            stats = []
            for keep in (first_head, ~first_head):
                sc = _dot_nt(jnp.where(keep, qb, jnp.zeros_like(qb)), kk)
                sc = jnp.where(mask, sc, -jnp.inf)
                m = jnp.max(sc, axis=-1, keepdims=True)
                p = jnp.exp(sc - m)
                l = jnp.sum(p, axis=-1, keepdims=True)
                acc = _dot(p.astype(BF16), vv)
                stats.append((m, l, acc))
            m_new = jnp.where(first_head, stats[0][0], stats[1][0])
            l_new = jnp.where(first_head, stats[0][1], stats[1][1])
            acc_new = jnp.where(first_head, stats[0][2], stats[1][2])
            m_old = m_sc[s, cur, :]
            m_tot = jnp.maximum(m_old, m_new)
            a_old = jnp.exp(m_old - m_tot)
            a_new = jnp.exp(m_new - m_tot)
            m_sc[s, cur, :] = m_tot
            l_sc[s, cur, :] = a_old * l_sc[s, cur, :] + a_new * l_new
            o_ref[s, cur, :] = a_old * o_ref[s, cur, :] + a_new * acc_new

    @pl.when(j == n_blocks - 1)
    def _finalize():
        rows_per_step = 2 * CHUNK

        def body(i, carry):
            rows = pl.ds(pl.multiple_of(i * rows_per_step, rows_per_step), rows_per_step)
            for s in range(N_SLABS):
                o_ref[s, rows, :] = o_ref[s, rows, :] / l_sc[s, rows, :]
            return carry

        lax.fori_loop(0, o_ref.shape[1] // rows_per_step, body, 0)


def _sample_attn_kernel(q_ref, kn_ref, vn_ref, kt_ref, vt_ref, mask_ref, o_ref, *, seq, new_valid):
    n_rows = q_ref.shape[1]
    n_real = len(DILATED_PATTERNS) * seq
    row = lax.broadcasted_iota(jnp.int32, (n_rows, 1), 0)
    real = row < n_real
    valid = mask_ref[...] > 0.5
    new_rows = []
    for tp in range(seq):
        sel = functools.reduce(jnp.logical_or, [row == r for r in range(n_rows) if new_valid[tp][r]])
        new_rows.append(sel)

    for h in range(N_HEADS):
        q = q_ref[h]
        sc = _dot(q.astype(BF16), kt_ref[h].astype(BF16))
        sc = jnp.where(valid, sc, -jnp.inf)
        kn = kn_ref[h]
        vn = vn_ref[h]
        s_new = [jnp.where(new_rows[tp], jnp.sum(q * kn[tp:tp + 1, :], axis=-1, keepdims=True), -jnp.inf)
                 for tp in range(seq)]
        m = jnp.max(sc, axis=-1, keepdims=True)
        for sn in s_new:
            m = jnp.maximum(m, sn)
        m = jnp.where(real, m, 0.0)
        p = jnp.exp(sc - m)
        p_new = [jnp.exp(sn - m) for sn in s_new]
        l = jnp.sum(p, axis=-1, keepdims=True) + sum(p_new)
        m_eff = jnp.where(real, m, NEG_BIG)
        l_eff = jnp.where(real, l, 0.0)
        m_all = m_eff
        for i in range(1, n_rows // seq):
            m_all = jnp.maximum(m_all, pltpu.roll(m_eff, i * seq, axis=0))
        c = jnp.exp(m_eff - m_all)
        cl = c * l_eff
        den = cl
        for i in range(1, n_rows // seq):
            den = den + pltpu.roll(cl, i * seq, axis=0)
        w = jnp.where(real, c / den, 0.0)
        out = _dot_nt((p * w).astype(BF16), vt_ref[h].astype(BF16))
        for tp in range(seq):
            out = out + (p_new[tp] * w) * vn[tp:tp + 1, :]
        tot = out
        for i in range(1, n_rows // seq):
            tot = tot + pltpu.roll(out, i * seq, axis=0)
        o_ref[h] = tot


def _finish_kernel(x_ref, mixa_ref, b_ref, g_out_b_ref, wo_ref, g_ffn_ref, wg_ref, wu_ref, wd_ref, o_ref):
    slabs = _group_norm_rows([b_ref[s] for s in range(N_SLABS)], g_out_b_ref)
    mix_b = jnp.concatenate([a.astype(BF16) for a in slabs], axis=-1)
    x1 = x_ref[...] + _dot(mixa_ref[...], wo_ref[:WIDTH, :]) + _dot(mix_b, wo_ref[WIDTH:, :])
    ms = jnp.mean(x1 * x1, axis=-1, keepdims=True)
    h = (x1 * lax.rsqrt(ms + EPS) * g_ffn_ref[...]).astype(BF16)
    act = (jax.nn.silu(_dot(h, wg_ref[...])) * _dot(h, wu_ref[...])).astype(BF16)
    o_ref[...] = x1 + _dot(act, wd_ref[...])


def _params(sem):
    return pltpu.CompilerParams(dimension_semantics=sem, vmem_limit_bytes=VMEM_LIMIT)


def _shared_in_proj_args(g_attn, w_in, ln_v_g, ln_v_b, g_q, g_k):
    d_model, in_width = w_in.shape
    gq = (jnp.tile(g_q, N_HEADS) * (HEAD_DIM ** -0.5))[None]
    gk = jnp.tile(g_k, N_HEADS)[None]
    bd = jnp.asarray(np.kron(np.eye(N_HEADS), np.full((HEAD_DIM, HEAD_DIM), 1.0 / HEAD_DIM)), BF16)
    args = (g_attn[None], w_in.astype(BF16), ln_v_g[None], ln_v_b[None], gq, gk, bd)
    specs = [_const_spec((1, d_model)), _const_spec((d_model, in_width)), _const_spec((1, WIDTH)),
             _const_spec((1, WIDTH)), _const_spec((1, WIDTH)), _const_spec((1, WIDTH)),
             _const_spec((WIDTH, WIDTH))]
    return args, specs


def _in_proj_prompt(x, shared, w_s, b_s, g_out_a, *, tm):
    batch, seq, d_model = x.shape
    n = batch * seq
    nt = seq // tm
    shared_args, shared_specs = shared
    tri = np.tril(np.ones((CHUNK, CHUNK), bool))
    ws = jnp.where(tri, w_s, 0.0).astype(BF16)
    bs = jnp.repeat(b_s.T, HEAD_DIM, axis=1)
    slab = jax.ShapeDtypeStruct((batch, N_SLABS, seq, LANES), F32)
    flat = jax.ShapeDtypeStruct((n, WIDTH), F32)
    slab_spec = pl.BlockSpec((None, N_SLABS, tm, LANES), lambda i: (i // nt, 0, i % nt, 0))
    row_spec = pl.BlockSpec((tm, WIDTH), lambda i: (i, 0))
    return pl.pallas_call(
        _in_proj_prompt_kernel,
        grid=(n // tm,),
        in_specs=[pl.BlockSpec((tm, d_model), lambda i: (i, 0))] + shared_specs + [
            _const_spec((N_HEADS, CHUNK, CHUNK)), _const_spec((CHUNK, WIDTH)), _const_spec((1, WIDTH))],
        out_specs=[slab_spec, slab_spec, slab_spec, row_spec, row_spec, row_spec],
        out_shape=[slab, slab, slab, flat, flat, jax.ShapeDtypeStruct((n, WIDTH), BF16)],
        compiler_params=_params(("parallel",)),
        name="in_proj_prompt",
    )(x.reshape(n, d_model), *shared_args, ws, bs, g_out_a[None])


def _in_proj_sample(x, shared, w_s, b_s, g_out_a):
    n_seq, seq, d_model = x.shape
    n = n_seq * seq
    shared_args, shared_specs = shared
    sub = 8
    assert sub % seq == 0 and seq <= CHUNK
    step = np.arange(sub) % seq
    taps = []
    for d in range(seq):
        src = np.maximum(step - d, 0)
        tap = jnp.where((step >= d)[:, None], w_s[:, step, src].T, 0.0)
        taps.append(jnp.repeat(tap, HEAD_DIM, axis=1))
    coef = jnp.stack(taps)
    bias = jnp.repeat(b_s[:, step].T, HEAD_DIM, axis=1)
    flat = jax.ShapeDtypeStruct((n, WIDTH), F32)
    row_spec = pl.BlockSpec((n, WIDTH), lambda i: (0, 0))
    return pl.pallas_call(
        functools.partial(_in_proj_sample_kernel, seq=seq),
        grid=(1,),
        in_specs=[pl.BlockSpec((n, d_model), lambda i: (0, 0))] + shared_specs + [
            _const_spec((seq, sub, WIDTH)), _const_spec((sub, WIDTH)), _const_spec((1, WIDTH))],
        out_specs=[row_spec] * 5,
        out_shape=[flat, flat, flat, flat, jax.ShapeDtypeStruct((n, WIDTH), BF16)],
        compiler_params=_params(("arbitrary",)),
        name="in_proj_sample",
    )(x.reshape(n, d_model), *shared_args, coef, bias, g_out_a[None])


def _prompt_attention(q, k, v):
    batch, _, seq, _ = q.shape
    n_blocks = seq // CHUNK
    assert all(n_blocks % dil == 0 for _, dil in DILATED_PATTERNS)
    spec = pl.BlockSpec((None, N_SLABS, seq, LANES), lambda b, j: (b, 0, 0, 0))
    return pl.pallas_call(
        functools.partial(_prompt_attn_kernel, n_blocks=n_blocks),
        grid=(batch, n_blocks),
        in_specs=[spec, spec, spec],
        out_specs=spec,
        out_shape=jax.ShapeDtypeStruct(q.shape, F32),
        scratch_shapes=[pltpu.VMEM((N_SLABS, seq, LANES), F32), pltpu.VMEM((N_SLABS, seq, LANES), F32)],
        compiler_params=_params(("parallel", "arbitrary")),
        name="prompt_attention",
    )(q, k, v)


def _sample_masks(seq, cache_len, past_len, n_rows):
    cache = np.zeros((n_rows, cache_len), np.float32)
    new = [[False] * n_rows for _ in range(seq)]
    for b, (window, dil) in enumerate(DILATED_PATTERNS):
        for t in range(seq):
            for jj in range(window // dil + 1):
                idx = cache_len + t - dil * jj
                if idx < 0 or idx + (past_len - cache_len) < 0:
                    continue
                if idx < cache_len:
                    cache[b * seq + t, idx] = 1.0
                else:
                    new[idx - cache_len][b * seq + t] = True
    return cache, new


def _sample_attention(q, k, v, cache_k, cache_v, past_len):
    n_seq, seq, _ = q.shape
    cache_len = cache_k.shape[1]
    n_rows = 16
    n_br = len(DILATED_PATTERNS)
    assert n_br * seq <= n_rows and n_rows % seq == 0
    mask, new_valid = _sample_masks(seq, cache_len, past_len, n_rows)

    def per_head(a, rows):
        a = a.reshape(n_seq, seq, N_HEADS, HEAD_DIM).transpose(0, 2, 1, 3)
        return a

    qh = per_head(q, seq)
    q16 = jnp.pad(jnp.tile(qh, (1, 1, n_br, 1)), ((0, 0), (0, 0), (0, n_rows - n_br * seq), (0, 0)))
    kn = jnp.pad(per_head(k, seq), ((0, 0), (0, 0), (0, 8 - seq), (0, 0)))
    vn = jnp.pad(per_head(v, seq), ((0, 0), (0, 0), (0, 8 - seq), (0, 0)))
    kt = jnp.transpose(cache_k, (0, 2, 3, 1))
    vt = jnp.transpose(cache_v, (0, 2, 3, 1))
    head_spec = lambda r: pl.BlockSpec((None, N_HEADS, r, HEAD_DIM), lambda i: (i, 0, 0, 0))
    cache_spec = pl.BlockSpec((None, N_HEADS, HEAD_DIM, cache_len), lambda i: (i, 0, 0, 0))
    out = pl.pallas_call(
        functools.partial(_sample_attn_kernel, seq=seq, new_valid=new_valid),
        grid=(n_seq,),
        in_specs=[head_spec(n_rows), head_spec(8), head_spec(8), cache_spec, cache_spec,
                  _const_spec((n_rows, cache_len))],
        out_specs=head_spec(n_rows),
        out_shape=jax.ShapeDtypeStruct((n_seq, N_HEADS, n_rows, HEAD_DIM), F32),
        compiler_params=_params(("parallel",)),
        name="sample_attention",
    )(q16, kn, vn, kt, vt, jnp.asarray(mask))
    b_out = out[:, :, :seq, :].transpose(0, 2, 1, 3).reshape(n_seq * seq, N_SLABS, LANES)
    return b_out.transpose(1, 0, 2)[None]


def _finish(x2d, mix_a, b_slab, g_out_b, w_o, g_ffn, w_gate, w_up, w_down, *, tm):
    n, d_model = x2d.shape
    batch, _, seq, _ = b_slab.shape
    nt = seq // tm
    d_ff = w_gate.shape[1]
    return pl.pallas_call(
        _finish_kernel,
        grid=(n // tm,),
        in_specs=[pl.BlockSpec((tm, d_model), lambda i: (i, 0)),
                  pl.BlockSpec((tm, WIDTH), lambda i: (i, 0)),
                  pl.BlockSpec((None, N_SLABS, tm, LANES), lambda i: (i // nt, 0, i % nt, 0)),
                  _const_spec((1, WIDTH)), _const_spec((2 * WIDTH, d_model)), _const_spec((1, d_model)),
                  _const_spec((d_model, d_ff)), _const_spec((d_model, d_ff)), _const_spec((d_ff, d_model))],
        out_specs=pl.BlockSpec((tm, d_model), lambda i: (i, 0)),
        out_shape=jax.ShapeDtypeStruct((n, d_model), F32),
        compiler_params=_params(("parallel",)),
        name="finish",
    )(x2d, mix_a, b_slab, g_out_b[None], w_o.astype(BF16), g_ffn[None],
      w_gate.astype(BF16), w_up.astype(BF16), w_down.astype(BF16))


def kernel(x_prompt, x_sample, cache_k, cache_v, g_attn, w_in, ln_v_g, ln_v_b, w_s, b_s, g_q, g_k,
           g_out_a, g_out_b, w_o, g_ffn, w_gate, w_up, w_down):
    depth = w_in.shape[0]
    assert depth == 1, "single-layer step"
    batch, seq, d_model = x_prompt.shape
    n_seq, dec_seq, _ = x_sample.shape
    l = 0
    shared = _shared_in_proj_args(g_attn[l], w_in[l], ln_v_g[l], ln_v_b[l], g_q[l], g_k[l])
    fin = (g_out_b[l], w_o[l], g_ffn[l], w_gate[l], w_up[l], w_down[l])

    q, k, v, k_out, v_out, mix_a = _in_proj_prompt(x_prompt, shared, w_s[l], b_s[l], g_out_a[l], tm=256)
    b_slab = _prompt_attention(q, k, v)
    y_prompt = _finish(x_prompt.reshape(batch * seq, d_model), mix_a, b_slab, *fin, tm=256)
    win = min(max(w for w, _ in DILATED_PATTERNS), seq)
    new_k_prompt = k_out.reshape(batch, seq, N_HEADS, HEAD_DIM)[:, -win:][None]
    new_v_prompt = v_out.reshape(batch, seq, N_HEADS, HEAD_DIM)[:, -win:][None]

    qs, ks, vs, vns, mix_as = _in_proj_sample(x_sample, shared, w_s[l], b_s[l], g_out_a[l])
    as3 = lambda a: a.reshape(n_seq, dec_seq, WIDTH)
    bs_slab = _sample_attention(as3(qs), as3(ks), as3(vs), cache_k[l], cache_v[l], PAST_LEN)
    y_sample = _finish(x_sample.reshape(n_seq * dec_seq, d_model), mix_as, bs_slab, *fin, tm=256)

    head_shape = (1, n_seq, dec_seq, N_HEADS, HEAD_DIM)
    return (y_prompt.reshape(batch, seq, d_model), y_sample.reshape(n_seq, dec_seq, d_model),
            new_k_prompt, new_v_prompt, ks.reshape(head_shape), vs.reshape(head_shape),
            vns.reshape(1, n_seq, dec_seq, WIDTH))
```

```python
import functools

import numpy as np
import jax
import jax.numpy as jnp
from jax import lax
from jax.experimental import pallas as pl
from jax.experimental.pallas import tpu as pltpu

F32 = jnp.float32
BF16 = jnp.bfloat16

HEAD_DIM = 64
N_HEADS = 8
WIDTH = N_HEADS * HEAD_DIM
CHUNK = 128
DILATED_PATTERNS = ((128, 1), (512, 4), (2048, 16))
EPS = 1e-6
PAST_LEN = 8192
NEG_BIG = -1e30

LANES = 128
N_SLABS = WIDTH // LANES
VMEM_LIMIT = 56 * 1024 * 1024


def _dot(a, b):
    return jnp.dot(a, b, preferred_element_type=F32)


def _dot_nt(a, b):
    return lax.dot_general(a, b, (((1,), (1,)), ((), ())), preferred_element_type=F32)


def _const_spec(shape):
    return pl.BlockSpec(shape, lambda *_: (0,) * len(shape), pipeline_mode=pl.Buffered(1))


def _head_rms(z, bd_ref, g_ref):
    sq = z * z
    hi = sq.astype(BF16)
    lo = (sq - hi.astype(F32)).astype(BF16)
    msq = _dot(hi, bd_ref[...]) + _dot(lo, bd_ref[...])
    return z * lax.rsqrt(msq + EPS) * g_ref[...]


def _features(x_ref, g_attn_ref, w_in_ref, ln_g_ref, ln_b_ref, gq_ref, gk_ref, bd_ref):
    x = x_ref[...]
    ms = jnp.mean(x * x, axis=-1, keepdims=True)
    xn = (x * lax.rsqrt(ms + EPS) * g_attn_ref[...]).astype(BF16)

    def proj(j):
        return _dot(xn, w_in_ref[:, j * WIDTH:(j + 1) * WIDTH])

    u = jax.nn.gelu(proj(0))
    gv = jax.nn.gelu(proj(1))
    mu = jnp.mean(gv, axis=-1, keepdims=True)
    xc = gv - mu
    var = jnp.mean(xc * xc, axis=-1, keepdims=True)
    vn = xc * lax.rsqrt(var + EPS) * ln_g_ref[...] + ln_b_ref[...]
    q = _head_rms(proj(2), bd_ref, gq_ref)
    k = _head_rms(proj(3), bd_ref, gk_ref)
    v = proj(4)
    return u, vn, q, k, v


def _group_norm_rows(slabs, g_ref):
    ssq = sum(jnp.sum(a * a, axis=-1, keepdims=True) for a in slabs)
    r = lax.rsqrt(ssq * (1.0 / WIDTH) + EPS)
    return [a * r * g_ref[:, s * LANES:(s + 1) * LANES] for s, a in enumerate(slabs)]


def _in_proj_prompt_kernel(x_ref, g_attn_ref, w_in_ref, ln_g_ref, ln_b_ref, gq_ref, gk_ref, bd_ref,
                           ws_ref, bs_ref, g_out_a_ref,
                           q_ref, k_ref, v_ref, kout_ref, vout_ref, mixa_ref):
    u, vn, q, k, v = _features(x_ref, g_attn_ref, w_in_ref, ln_g_ref, ln_b_ref, gq_ref, gk_ref, bd_ref)
    tm = u.shape[0]
    first_head = lax.broadcasted_iota(jnp.int32, (CHUNK, LANES), 1) < HEAD_DIM
    for c in range(tm // CHUNK):
        rows = slice(c * CHUNK, (c + 1) * CHUNK)
        vn_c = vn[rows].astype(BF16)
        slabs = []
        for s in range(N_SLABS):
            cols = slice(s * LANES, (s + 1) * LANES)
            g0 = _dot(ws_ref[2 * s], vn_c[:, cols])
            g1 = _dot(ws_ref[2 * s + 1], vn_c[:, cols])
            gate = jnp.where(first_head, g0, g1) + bs_ref[:, cols]
            slabs.append(u[rows, cols] * gate)
        for s, a in enumerate(_group_norm_rows(slabs, g_out_a_ref)):
            mixa_ref[rows, s * LANES:(s + 1) * LANES] = a.astype(BF16)
    for s in range(N_SLABS):
        cols = slice(s * LANES, (s + 1) * LANES)
        q_ref[s] = q[:, cols]
        k_ref[s] = k[:, cols]
        v_ref[s] = v[:, cols]
    kout_ref[...] = k
    vout_ref[...] = v


def _in_proj_sample_kernel(x_ref, g_attn_ref, w_in_ref, ln_g_ref, ln_b_ref, gq_ref, gk_ref, bd_ref,
                           coef_ref, bias_ref, g_out_a_ref,
                           q_ref, k_ref, v_ref, vn_ref, mixa_ref, *, seq):
    u, vn, q, k, v = _features(x_ref, g_attn_ref, w_in_ref, ln_g_ref, ln_b_ref, gq_ref, gk_ref, bd_ref)
    tm = u.shape[0]
    sub = coef_ref.shape[1]
    gate = jnp.zeros((tm // sub, sub, WIDTH), F32) + bias_ref[...]
    for d in range(seq):
        shifted = vn if d == 0 else pltpu.roll(vn, d, axis=0)
        gate = gate + shifted.reshape(tm // sub, sub, WIDTH) * coef_ref[d]
    a = u * gate.reshape(tm, WIDTH)
    slabs = _group_norm_rows([a[:, s * LANES:(s + 1) * LANES] for s in range(N_SLABS)], g_out_a_ref)
    for s, a_s in enumerate(slabs):
        mixa_ref[:, s * LANES:(s + 1) * LANES] = a_s.astype(BF16)
    q_ref[...] = q
    k_ref[...] = k
    v_ref[...] = v
    vn_ref[...] = vn


def _prompt_attn_kernel(q_ref, k_ref, v_ref, o_ref, m_sc, l_sc, *, n_blocks):
    j = pl.program_id(1)

    @pl.when(j == 0)
    def _init():
        m_sc[...] = jnp.full(m_sc.shape, NEG_BIG, F32)
        l_sc[...] = jnp.zeros(l_sc.shape, F32)
        o_ref[...] = jnp.zeros(o_ref.shape, F32)

    first_head = lax.broadcasted_iota(jnp.int32, (CHUNK, LANES), 1) < HEAD_DIM
    row = lax.broadcasted_iota(jnp.int32, (CHUNK, 2 * CHUNK), 0)
    col = lax.broadcasted_iota(jnp.int32, (CHUNK, 2 * CHUNK), 1)
    band = (col >= row) & (col <= row + CHUNK)
    causal = (lax.broadcasted_iota(jnp.int32, (CHUNK, CHUNK), 1)
              <= lax.broadcasted_iota(jnp.int32, (CHUNK, CHUNK), 0))

    for window, dil in DILATED_PATTERNS:
        blocks_per_class = n_blocks // dil
        assert window // dil == CHUNK
        if blocks_per_class > 1:
            r = j // blocks_per_class
            jb = j % blocks_per_class
            start = r + (dil * CHUNK) * jb
            prev_start = r + (dil * CHUNK) * jnp.maximum(jb - 1, 0)
            mask = band & ((col >= CHUNK) | (jb > 0))
        else:
            start = j
            mask = causal

        def rows_at(st):
            return pl.ds(st, CHUNK) if dil == 1 else pl.ds(st, CHUNK, stride=dil)

        cur = rows_at(start)
        for s in range(N_SLABS):
            qb = q_ref[s, cur, :].astype(BF16)
            kk = k_ref[s, cur, :]
            vv = v_ref[s, cur, :]
            if blocks_per_class > 1:
                prev = rows_at(prev_start)
                kk = jnp.concatenate([k_ref[s, prev, :], kk], axis=0)
                vv = jnp.concatenate([v_ref[s, prev, :], vv], axis=0)
            kk = kk.astype(BF16)
            vv = vv.astype(BF16)
            stats = []
            for keep in (first_head, ~first_head):
                sc = _dot_nt(jnp.where(keep, qb, jnp.zeros_like(qb)), kk)
                sc = jnp.where(mask, sc, -jnp.inf)
                m = jnp.max(sc, axis=-1, keepdims=True)
                p = jnp.exp(sc - m)
                l = jnp.sum(p, axis=-1, keepdims=True)
                acc = _dot(p.astype(BF16), vv)
                stats.append((m, l, acc))
            m_new = jnp.where(first_head, stats[0][0], stats[1][0])
            l_new = jnp.where(first_head, stats[0][1], stats[1][1])
            acc_new = jnp.where(first_head, stats[0][2], stats[1][2])
            m_old = m_sc[s, cur, :]
            m_tot = jnp.maximum(m_old, m_new)
            a_old = jnp.exp(m_old - m_tot)
            a_new = jnp.exp(m_new - m_tot)
            m_sc[s, cur, :] = m_tot
            l_sc[s, cur, :] = a_old * l_sc[s, cur, :] + a_new * l_new
            o_ref[s, cur, :] = a_old * o_ref[s, cur, :] + a_new * acc_new

    @pl.when(j == n_blocks - 1)
    def _finalize():
        rows_per_step = 2 * CHUNK

        def body(i, carry):
            rows = pl.ds(pl.multiple_of(i * rows_per_step, rows_per_step), rows_per_step)
            for s in range(N_SLABS):
                o_ref[s, rows, :] = o_ref[s, rows, :] / l_sc[s, rows, :]
            return carry

        lax.fori_loop(0, o_ref.shape[1] // rows_per_step, body, 0)


def _sample_attn_kernel(q_ref, kn_ref, vn_ref, kt_ref, vt_ref, mask_ref, o_ref, s_sc, p_sc, *, seq, new_valid):
    n_rows = q_ref.shape[1]
    n_real = len(DILATED_PATTERNS) * seq
    row = lax.broadcasted_iota(jnp.int32, (n_rows, 1), 0)
    real = row < n_real
    valid = mask_ref[...] > 0.5
    new_rows = []
    for tp in range(seq):
        sel = functools.reduce(jnp.logical_or, [row == r for r in range(n_rows) if new_valid[tp][r]])
        new_rows.append(sel)

    heads = range(N_HEADS)
    for h in heads:
        sc = _dot(q_ref[h].astype(BF16), kt_ref[h].astype(BF16))
        s_sc[h] = jnp.where(valid, sc, -jnp.inf)

    weights = []
    for h in heads:
        q = q_ref[h]
        kn = kn_ref[h]
        s_new = [jnp.where(new_rows[tp], jnp.sum(q * kn[tp:tp + 1, :], axis=-1, keepdims=True), -jnp.inf)
                 for tp in range(seq)]
        sc = s_sc[h]
        m = jnp.max(sc, axis=-1, keepdims=True)
        for sn in s_new:
            m = jnp.maximum(m, sn)
        m = jnp.where(real, m, 0.0)
        p = jnp.exp(sc - m)
        p_sc[h] = p.astype(BF16)
        p_new = [jnp.exp(sn - m) for sn in s_new]
        l = jnp.sum(p, axis=-1, keepdims=True) + sum(p_new)
        m_eff = jnp.where(real, m, NEG_BIG)
        l_eff = jnp.where(real, l, 0.0)
        m_all = m_eff
        for i in range(1, n_rows // seq):
            m_all = jnp.maximum(m_all, pltpu.roll(m_eff, i * seq, axis=0))
        c = jnp.exp(m_eff - m_all)
        cl = c * l_eff
        den = cl
        for i in range(1, n_rows // seq):
            den = den + pltpu.roll(cl, i * seq, axis=0)
        w = jnp.where(real, c / den, 0.0)
        weights.append((w, [pn * w for pn in p_new]))

    for h in heads:
        w, pw_new = weights[h]
        vn = vn_ref[h]
        out = _dot_nt(p_sc[h], vt_ref[h].astype(BF16)) * w
        for tp in range(seq):
            out = out + pw_new[tp] * vn[tp:tp + 1, :]
        tot = out
        for i in range(1, n_rows // seq):
            tot = tot + pltpu.roll(out, i * seq, axis=0)
        o_ref[h] = tot


def _finish_kernel(x_ref, mixa_ref, b_ref, g_out_b_ref, wo_ref, g_ffn_ref, wg_ref, wu_ref, wd_ref, o_ref):
    slabs = _group_norm_rows([b_ref[s] for s in range(N_SLABS)], g_out_b_ref)
    mix_b = jnp.concatenate([a.astype(BF16) for a in slabs], axis=-1)
    x1 = x_ref[...] + _dot(mixa_ref[...], wo_ref[:WIDTH, :]) + _dot(mix_b, wo_ref[WIDTH:, :])
    ms = jnp.mean(x1 * x1, axis=-1, keepdims=True)
    h = (x1 * lax.rsqrt(ms + EPS) * g_ffn_ref[...]).astype(BF16)
    act = (jax.nn.silu(_dot(h, wg_ref[...])) * _dot(h, wu_ref[...])).astype(BF16)
    o_ref[...] = x1 + _dot(act, wd_ref[...])


def _params(sem):
    return pltpu.CompilerParams(dimension_semantics=sem, vmem_limit_bytes=VMEM_LIMIT)


def _shared_in_proj_args(g_attn, w_in, ln_v_g, ln_v_b, g_q, g_k):
    d_model, in_width = w_in.shape
    gq = (jnp.tile(g_q, N_HEADS) * (HEAD_DIM ** -0.5))[None]
    gk = jnp.tile(g_k, N_HEADS)[None]
    bd = jnp.asarray(np.kron(np.eye(N_HEADS), np.full((HEAD_DIM, HEAD_DIM), 1.0 / HEAD_DIM)), BF16)
    args = (g_attn[None], w_in.astype(BF16), ln_v_g[None], ln_v_b[None], gq, gk, bd)
    specs = [_const_spec((1, d_model)), _const_spec((d_model, in_width)), _const_spec((1, WIDTH)),
             _const_spec((1, WIDTH)), _const_spec((1, WIDTH)), _const_spec((1, WIDTH)),
             _const_spec((WIDTH, WIDTH))]
    return args, specs


def _in_proj_prompt(x, shared, w_s, b_s, g_out_a, *, tm):
    batch, seq, d_model = x.shape
    n = batch * seq
    nt = seq // tm
    shared_args, shared_specs = shared
    tri = np.tril(np.ones((CHUNK, CHUNK), bool))
    ws = jnp.where(tri, w_s, 0.0).astype(BF16)
    bs = jnp.repeat(b_s.T, HEAD_DIM, axis=1)
    slab = jax.ShapeDtypeStruct((batch, N_SLABS, seq, LANES), F32)
    flat = jax.ShapeDtypeStruct((n, WIDTH), F32)
    slab_spec = pl.BlockSpec((None, N_SLABS, tm, LANES), lambda i: (i // nt, 0, i % nt, 0))
    row_spec = pl.BlockSpec((tm, WIDTH), lambda i: (i, 0))
    return pl.pallas_call(
        _in_proj_prompt_kernel,
        grid=(n // tm,),
        in_specs=[pl.BlockSpec((tm, d_model), lambda i: (i, 0))] + shared_specs + [
            _const_spec((N_HEADS, CHUNK, CHUNK)), _const_spec((CHUNK, WIDTH)), _const_spec((1, WIDTH))],
        out_specs=[slab_spec, slab_spec, slab_spec, row_spec, row_spec, row_spec],
        out_shape=[slab, slab, slab, flat, flat, jax.ShapeDtypeStruct((n, WIDTH), BF16)],
        compiler_params=_params(("parallel",)),
        name="in_proj_prompt",
    )(x.reshape(n, d_model), *shared_args, ws, bs, g_out_a[None])


def _in_proj_sample(x, shared, w_s, b_s, g_out_a):
    n_seq, seq, d_model = x.shape
    n = n_seq * seq
    shared_args, shared_specs = shared
    sub = 8
    assert sub % seq == 0 and seq <= CHUNK
    step = np.arange(sub) % seq
    taps = []
    for d in range(seq):
        src = np.maximum(step - d, 0)
        tap = jnp.where((step >= d)[:, None], w_s[:, step, src].T, 0.0)
        taps.append(jnp.repeat(tap, HEAD_DIM, axis=1))
    coef = jnp.stack(taps)
    bias = jnp.repeat(b_s[:, step].T, HEAD_DIM, axis=1)
    flat = jax.ShapeDtypeStruct((n, WIDTH), F32)
    row_spec = pl.BlockSpec((n, WIDTH), lambda i: (0, 0))
    return pl.pallas_call(
        functools.partial(_in_proj_sample_kernel, seq=seq),
        grid=(1,),
        in_specs=[pl.BlockSpec((n, d_model), lambda i: (0, 0))] + shared_specs + [
            _const_spec((seq, sub, WIDTH)), _const_spec((sub, WIDTH)), _const_spec((1, WIDTH))],
        out_specs=[row_spec] * 5,
        out_shape=[flat, flat, flat, flat, jax.ShapeDtypeStruct((n, WIDTH), BF16)],
        compiler_params=_params(("arbitrary",)),
        name="in_proj_sample",
    )(x.reshape(n, d_model), *shared_args, coef, bias, g_out_a[None])


def _prompt_attention(q, k, v):
    batch, _, seq, _ = q.shape
    n_blocks = seq // CHUNK
    assert all(n_blocks % dil == 0 for _, dil in DILATED_PATTERNS)
    spec = pl.BlockSpec((None, N_SLABS, seq, LANES), lambda b, j: (b, 0, 0, 0))
    return pl.pallas_call(
        functools.partial(_prompt_attn_kernel, n_blocks=n_blocks),
        grid=(batch, n_blocks),
        in_specs=[spec, spec, spec],
        out_specs=spec,
        out_shape=jax.ShapeDtypeStruct(q.shape, F32),
        scratch_shapes=[pltpu.VMEM((N_SLABS, seq, LANES), F32), pltpu.VMEM((N_SLABS, seq, LANES), F32)],
        compiler_params=_params(("parallel", "arbitrary")),
        name="prompt_attention",
    )(q, k, v)


def _sample_masks(seq, cache_len, past_len, n_rows):
    cache = np.zeros((n_rows, cache_len), np.float32)
    new = [[False] * n_rows for _ in range(seq)]
    for b, (window, dil) in enumerate(DILATED_PATTERNS):
        for t in range(seq):
            for jj in range(window // dil + 1):
                idx = cache_len + t - dil * jj
                if idx < 0 or idx + (past_len - cache_len) < 0:
                    continue
                if idx < cache_len:
                    cache[b * seq + t, idx] = 1.0
                else:
                    new[idx - cache_len][b * seq + t] = True
    return cache, new


def _sample_attention(q, k, v, cache_k, cache_v, past_len):
    n_seq, seq, _ = q.shape
    cache_len = cache_k.shape[1]
    n_rows = 16
    n_br = len(DILATED_PATTERNS)
    assert n_br * seq <= n_rows and n_rows % seq == 0
    mask, new_valid = _sample_masks(seq, cache_len, past_len, n_rows)

    def per_head(a, rows):
        a = a.reshape(n_seq, seq, N_HEADS, HEAD_DIM).transpose(0, 2, 1, 3)
        return a

    qh = per_head(q, seq)
    q16 = jnp.pad(jnp.tile(qh, (1, 1, n_br, 1)), ((0, 0), (0, 0), (0, n_rows - n_br * seq), (0, 0)))
    kn = jnp.pad(per_head(k, seq), ((0, 0), (0, 0), (0, 8 - seq), (0, 0)))
    vn = jnp.pad(per_head(v, seq), ((0, 0), (0, 0), (0, 8 - seq), (0, 0)))
    kt = jnp.transpose(cache_k, (0, 2, 3, 1))
    vt = jnp.transpose(cache_v, (0, 2, 3, 1))
    head_spec = lambda r: pl.BlockSpec((None, N_HEADS, r, HEAD_DIM), lambda i: (i, 0, 0, 0))
    cache_spec = pl.BlockSpec((None, N_HEADS, HEAD_DIM, cache_len), lambda i: (i, 0, 0, 0))
    out = pl.pallas_call(
        functools.partial(_sample_attn_kernel, seq=seq, new_valid=new_valid),
        grid=(n_seq,),
        in_specs=[head_spec(n_rows), head_spec(8), head_spec(8), cache_spec, cache_spec,
                  _const_spec((n_rows, cache_len))],
        out_specs=head_spec(n_rows),
        out_shape=jax.ShapeDtypeStruct((n_seq, N_HEADS, n_rows, HEAD_DIM), F32),
        scratch_shapes=[pltpu.VMEM((N_HEADS, n_rows, cache_len), F32),
                        pltpu.VMEM((N_HEADS, n_rows, cache_len), BF16)],
        compiler_params=_params(("parallel",)),
        name="sample_attention",
    )(q16, kn, vn, kt, vt, jnp.asarray(mask))
    b_out = out[:, :, :seq, :].transpose(0, 2, 1, 3).reshape(n_seq * seq, N_SLABS, LANES)
    return b_out.transpose(1, 0, 2)[None]


def _finish(x2d, mix_a, b_slab, g_out_b, w_o, g_ffn, w_gate, w_up, w_down, *, tm):
    n, d_model = x2d.shape
    batch, _, seq, _ = b_slab.shape
    nt = seq // tm
    d_ff = w_gate.shape[1]
    return pl.pallas_call(
        _finish_kernel,
        grid=(n // tm,),
        in_specs=[pl.BlockSpec((tm, d_model), lambda i: (i, 0)),
                  pl.BlockSpec((tm, WIDTH), lambda i: (i, 0)),
                  pl.BlockSpec((None, N_SLABS, tm, LANES), lambda i: (i // nt, 0, i % nt, 0)),
                  _const_spec((1, WIDTH)), _const_spec((2 * WIDTH, d_model)), _const_spec((1, d_model)),
                  _const_spec((d_model, d_ff)), _const_spec((d_model, d_ff)), _const_spec((d_ff, d_model))],
        out_specs=pl.BlockSpec((tm, d_model), lambda i: (i, 0)),
        out_shape=jax.ShapeDtypeStruct((n, d_model), F32),
        compiler_params=_params(("parallel",)),
        name="finish",
    )(x2d, mix_a, b_slab, g_out_b[None], w_o.astype(BF16), g_ffn[None],
      w_gate.astype(BF16), w_up.astype(BF16), w_down.astype(BF16))


def kernel(x_prompt, x_sample, cache_k, cache_v, g_attn, w_in, ln_v_g, ln_v_b, w_s, b_s, g_q, g_k,
           g_out_a, g_out_b, w_o, g_ffn, w_gate, w_up, w_down):
    depth = w_in.shape[0]
    assert depth == 1, "single-layer step"
    batch, seq, d_model = x_prompt.shape
    n_seq, dec_seq, _ = x_sample.shape
    l = 0
    shared = _shared_in_proj_args(g_attn[l], w_in[l], ln_v_g[l], ln_v_b[l], g_q[l], g_k[l])
    fin = (g_out_b[l], w_o[l], g_ffn[l], w_gate[l], w_up[l], w_down[l])

    q, k, v, k_out, v_out, mix_a = _in_proj_prompt(x_prompt, shared, w_s[l], b_s[l], g_out_a[l], tm=256)
    b_slab = _prompt_attention(q, k, v)
    y_prompt = _finish(x_prompt.reshape(batch * seq, d_model), mix_a, b_slab, *fin, tm=256)
    win = min(max(w for w, _ in DILATED_PATTERNS), seq)
    new_k_prompt = k_out.reshape(batch, seq, N_HEADS, HEAD_DIM)[:, -win:][None]
    new_v_prompt = v_out.reshape(batch, seq, N_HEADS, HEAD_DIM)[:, -win:][None]

    qs, ks, vs, vns, mix_as = _in_proj_sample(x_sample, shared, w_s[l], b_s[l], g_out_a[l])
    as3 = lambda a: a.reshape(n_seq, dec_seq, WIDTH)
    bs_slab = _sample_attention(as3(qs), as3(ks), as3(vs), cache_k[l], cache_v[l], PAST_LEN)
    y_sample = _finish(x_sample.reshape(n_seq * dec_seq, d_model), mix_as, bs_slab, *fin, tm=256)

    head_shape = (1, n_seq, dec_seq, N_HEADS, HEAD_DIM)
    return (y_prompt.reshape(batch, seq, d_model), y_sample.reshape(n_seq, dec_seq, d_model),
            new_k_prompt, new_v_prompt, ks.reshape(head_shape), vs.reshape(head_shape),
            vns.reshape(1, n_seq, dec_seq, WIDTH))
```

```python
import functools

import numpy as np
import jax
import jax.numpy as jnp
from jax import lax
from jax.experimental import pallas as pl
from jax.experimental.pallas import tpu as pltpu

F32 = jnp.float32
BF16 = jnp.bfloat16

HEAD_DIM = 64
N_HEADS = 8
WIDTH = N_HEADS * HEAD_DIM
CHUNK = 128
DILATED_PATTERNS = ((128, 1), (512, 4), (2048, 16))
EPS = 1e-6
PAST_LEN = 8192
NEG_BIG = -1e30

LANES = 128
N_SLABS = WIDTH // LANES
VMEM_LIMIT = 56 * 1024 * 1024


def _dot(a, b):
    return jnp.dot(a, b, preferred_element_type=F32)


def _dot_nt(a, b):
    return lax.dot_general(a, b, (((1,), (1,)), ((), ())), preferred_element_type=F32)


def _const_spec(shape):
    return pl.BlockSpec(shape, lambda *_: (0,) * len(shape), pipeline_mode=pl.Buffered(1))


def _head_rms(z, bd_ref, g_ref):
    sq = z * z
    hi = sq.astype(BF16)
    lo = (sq - hi.astype(F32)).astype(BF16)
    msq = _dot(hi, bd_ref[...]) + _dot(lo, bd_ref[...])
    return z * lax.rsqrt(msq + EPS) * g_ref[...]


def _features(x_ref, g_attn_ref, w_in_ref, ln_g_ref, ln_b_ref, gq_ref, gk_ref, bd_ref):
    x = x_ref[...]
    ms = jnp.mean(x * x, axis=-1, keepdims=True)
    xn = (x * lax.rsqrt(ms + EPS) * g_attn_ref[...]).astype(BF16)

    def proj(j):
        return _dot(xn, w_in_ref[:, j * WIDTH:(j + 1) * WIDTH])

    u = jax.nn.gelu(proj(0))
    gv = jax.nn.gelu(proj(1))
    mu = jnp.mean(gv, axis=-1, keepdims=True)
    xc = gv - mu
    var = jnp.mean(xc * xc, axis=-1, keepdims=True)
    vn = xc * lax.rsqrt(var + EPS) * ln_g_ref[...] + ln_b_ref[...]
    q = _head_rms(proj(2), bd_ref, gq_ref)
    k = _head_rms(proj(3), bd_ref, gk_ref)
    v = proj(4)
    return u, vn, q, k, v


def _group_norm_rows(slabs, g_ref):
    ssq = sum(jnp.sum(a * a, axis=-1, keepdims=True) for a in slabs)
    r = lax.rsqrt(ssq * (1.0 / WIDTH) + EPS)
    return [a * r * g_ref[:, s * LANES:(s + 1) * LANES] for s, a in enumerate(slabs)]


def _in_proj_prompt_kernel(x_ref, g_attn_ref, w_in_ref, ln_g_ref, ln_b_ref, gq_ref, gk_ref, bd_ref,
                           ws_ref, bs_ref, g_out_a_ref,
                           q_ref, k_ref, v_ref, kout_ref, vout_ref, mixa_ref):
    u, vn, q, k, v = _features(x_ref, g_attn_ref, w_in_ref, ln_g_ref, ln_b_ref, gq_ref, gk_ref, bd_ref)
    tm = u.shape[0]
    first_head = lax.broadcasted_iota(jnp.int32, (CHUNK, LANES), 1) < HEAD_DIM
    for c in range(tm // CHUNK):
        rows = slice(c * CHUNK, (c + 1) * CHUNK)
        vn_c = vn[rows].astype(BF16)
        slabs = []
        for s in range(N_SLABS):
            cols = slice(s * LANES, (s + 1) * LANES)
            g0 = _dot(ws_ref[2 * s], vn_c[:, cols])
            g1 = _dot(ws_ref[2 * s + 1], vn_c[:, cols])
            gate = jnp.where(first_head, g0, g1) + bs_ref[:, cols]
            slabs.append(u[rows, cols] * gate)
        for s, a in enumerate(_group_norm_rows(slabs, g_out_a_ref)):
            mixa_ref[rows, s * LANES:(s + 1) * LANES] = a.astype(BF16)
    for s in range(N_SLABS):
        cols = slice(s * LANES, (s + 1) * LANES)
        q_ref[s] = q[:, cols]
        k_ref[s] = k[:, cols]
        v_ref[s] = v[:, cols]
    kout_ref[...] = k.T
    vout_ref[...] = v.T


def _in_proj_sample_kernel(x_ref, g_attn_ref, w_in_ref, ln_g_ref, ln_b_ref, gq_ref, gk_ref, bd_ref,
                           coef_ref, bias_ref, g_out_a_ref,
                           q_ref, k_ref, v_ref, vn_ref, mixa_ref, *, seq):
    u, vn, q, k, v = _features(x_ref, g_attn_ref, w_in_ref, ln_g_ref, ln_b_ref, gq_ref, gk_ref, bd_ref)
    tm = u.shape[0]
    sub = coef_ref.shape[1]
    gate = jnp.zeros((tm // sub, sub, WIDTH), F32) + bias_ref[...]
    for d in range(seq):
        shifted = vn if d == 0 else pltpu.roll(vn, d, axis=0)
        gate = gate + shifted.reshape(tm // sub, sub, WIDTH) * coef_ref[d]
    a = u * gate.reshape(tm, WIDTH)
    slabs = _group_norm_rows([a[:, s * LANES:(s + 1) * LANES] for s in range(N_SLABS)], g_out_a_ref)
    for s, a_s in enumerate(slabs):
        mixa_ref[:, s * LANES:(s + 1) * LANES] = a_s.astype(BF16)
    q_ref[...] = q
    k_ref[...] = k
    v_ref[...] = v
    vn_ref[...] = vn


def _prompt_attn_kernel(q_ref, k_ref, v_ref, o_ref, m_sc, l_sc, *, n_blocks):
    (w_near, d_near), (w_mid, d_mid), (w_far, d_far) = DILATED_PATTERNS
    assert d_near == 1 and d_far == n_blocks and n_blocks % d_mid == 0
    assert w_near // d_near == w_mid // d_mid == w_far // d_far == CHUNK
    mid_blocks = n_blocks // d_mid

    first_head = lax.broadcasted_iota(jnp.int32, (CHUNK, LANES), 1) < HEAD_DIM
    row = lax.broadcasted_iota(jnp.int32, (2 * CHUNK, 2 * CHUNK), 0) % CHUNK
    col = lax.broadcasted_iota(jnp.int32, (2 * CHUNK, 2 * CHUNK), 1)
    band = (col >= row) & (col <= row + CHUNK)
    no_prev = col >= CHUNK
    causal = (lax.broadcasted_iota(jnp.int32, (2 * CHUNK, CHUNK), 1)
              <= lax.broadcasted_iota(jnp.int32, (2 * CHUNK, CHUNK), 0) % CHUNK)

    def attend(s, cur, prev, mask):
        qb = q_ref[s, cur, :].astype(BF16)
        zero = jnp.zeros_like(qb)
        lhs = jnp.concatenate([jnp.where(first_head, qb, zero), jnp.where(first_head, zero, qb)], axis=0)
        kk = k_ref[s, cur, :]
        vv = v_ref[s, cur, :]
        if prev is not None:
            kk = jnp.concatenate([k_ref[s, prev, :], kk], axis=0)
            vv = jnp.concatenate([v_ref[s, prev, :], vv], axis=0)
        sc = jnp.where(mask, _dot_nt(lhs, kk.astype(BF16)), -jnp.inf)
        m = jnp.max(sc, axis=-1, keepdims=True)
        p = jnp.exp(sc - m)
        l = jnp.sum(p, axis=-1, keepdims=True)
        acc = _dot(p.astype(BF16), vv.astype(BF16))
        both = lambda a: jnp.where(first_head, a[:CHUNK], a[CHUNK:])
        return both(m), both(l), both(acc)

    def merged(s, cur, stats):
        m_new, l_new, acc_new = stats
        m_old = m_sc[s, cur, :]
        m_tot = jnp.maximum(m_old, m_new)
        a_old = jnp.exp(m_old - m_tot)
        a_new = jnp.exp(m_new - m_tot)
        return (m_tot, a_old * l_sc[s, cur, :] + a_new * l_new, a_old * o_ref[s, cur, :] + a_new * acc_new)

    def far_class(c, carry):
        cur = pl.ds(c, CHUNK, stride=d_far)
        for s in range(N_SLABS):
            m_sc[s, cur, :], l_sc[s, cur, :], o_ref[s, cur, :] = attend(s, cur, None, causal)
        return carry

    lax.fori_loop(0, d_far, far_class, 0)

    def span(jb, carry):
        for r in range(d_mid):
            cur = pl.ds(r + (d_mid * CHUNK) * jb, CHUNK, stride=d_mid)
            prev = pl.ds(r + (d_mid * CHUNK) * jnp.maximum(jb - 1, 0), CHUNK, stride=d_mid)
            mask = band & (no_prev | (jb > 0))
            for s in range(N_SLABS):
                m_sc[s, cur, :], l_sc[s, cur, :], o_ref[s, cur, :] = merged(s, cur, attend(s, cur, prev, mask))
        for i in range(d_mid):
            j = jb * d_mid + i
            cur = pl.ds(pl.multiple_of(j * CHUNK, CHUNK), CHUNK)
            prev = pl.ds(pl.multiple_of(jnp.maximum(j - 1, 0) * CHUNK, CHUNK), CHUNK)
            mask = band & (no_prev | (j > 0))
            for s in range(N_SLABS):
                _, l_tot, acc_tot = merged(s, cur, attend(s, cur, prev, mask))
                o_ref[s, cur, :] = acc_tot / l_tot
        return carry

    lax.fori_loop(0, mid_blocks, span, 0)


def _sample_attn_kernel(q_ref, kn_ref, vn_ref, kt_ref, vt_ref, mask_ref, o_ref, s_sc, p_sc, *, seq, new_valid):
    n_rows = q_ref.shape[1]
    n_real = len(DILATED_PATTERNS) * seq
    row = lax.broadcasted_iota(jnp.int32, (n_rows, 1), 0)
    real = row < n_real
    valid = mask_ref[...] > 0.5
    new_rows = []
    for tp in range(seq):
        sel = functools.reduce(jnp.logical_or, [row == r for r in range(n_rows) if new_valid[tp][r]])
        new_rows.append(sel)

    heads = range(N_HEADS)
    for h in heads:
        sc = _dot(q_ref[h].astype(BF16), kt_ref[h].astype(BF16))
        s_sc[h] = jnp.where(valid, sc, -jnp.inf)

    weights = []
    for h in heads:
        q = q_ref[h]
        kn = kn_ref[h]
        s_new = [jnp.where(new_rows[tp], jnp.sum(q * kn[tp:tp + 1, :], axis=-1, keepdims=True), -jnp.inf)
                 for tp in range(seq)]
        sc = s_sc[h]
        m = jnp.max(sc, axis=-1, keepdims=True)
        for sn in s_new:
            m = jnp.maximum(m, sn)
        m = jnp.where(real, m, 0.0)
        p = jnp.exp(sc - m)
        p_sc[h] = p.astype(BF16)
        p_new = [jnp.exp(sn - m) for sn in s_new]
        l = jnp.sum(p, axis=-1, keepdims=True) + sum(p_new)
        m_eff = jnp.where(real, m, NEG_BIG)
        l_eff = jnp.where(real, l, 0.0)
        m_all = m_eff
        for i in range(1, n_rows // seq):
            m_all = jnp.maximum(m_all, pltpu.roll(m_eff, i * seq, axis=0))
        c = jnp.exp(m_eff - m_all)
        cl = c * l_eff
        den = cl
        for i in range(1, n_rows // seq):
            den = den + pltpu.roll(cl, i * seq, axis=0)
        w = jnp.where(real, c / den, 0.0)
        weights.append((w, [pn * w for pn in p_new]))

    for h in heads:
        w, pw_new = weights[h]
        vn = vn_ref[h]
        out = _dot_nt(p_sc[h], vt_ref[h].astype(BF16)) * w
        for tp in range(seq):
            out = out + pw_new[tp] * vn[tp:tp + 1, :]
        tot = out
        for i in range(1, n_rows // seq):
            tot = tot + pltpu.roll(out, i * seq, axis=0)
        o_ref[h] = tot


def _finish_kernel(x_ref, mixa_ref, b_ref, g_out_b_ref, wo_ref, g_ffn_ref, wg_ref, wu_ref, wd_ref, o_ref):
    slabs = _group_norm_rows([b_ref[s] for s in range(N_SLABS)], g_out_b_ref)
    mix_b = jnp.concatenate([a.astype(BF16) for a in slabs], axis=-1)
    x1 = x_ref[...] + _dot(mixa_ref[...], wo_ref[:WIDTH, :]) + _dot(mix_b, wo_ref[WIDTH:, :])
    ms = jnp.mean(x1 * x1, axis=-1, keepdims=True)
    h = (x1 * lax.rsqrt(ms + EPS) * g_ffn_ref[...]).astype(BF16)
    act = (jax.nn.silu(_dot(h, wg_ref[...])) * _dot(h, wu_ref[...])).astype(BF16)
    o_ref[...] = x1 + _dot(act, wd_ref[...])


def _params(sem):
    return pltpu.CompilerParams(dimension_semantics=sem, vmem_limit_bytes=VMEM_LIMIT)


def _shared_in_proj_args(g_attn, w_in, ln_v_g, ln_v_b, g_q, g_k):
    d_model, in_width = w_in.shape
    gq = (jnp.tile(g_q, N_HEADS) * (HEAD_DIM ** -0.5))[None]
    gk = jnp.tile(g_k, N_HEADS)[None]
    bd = jnp.asarray(np.kron(np.eye(N_HEADS), np.full((HEAD_DIM, HEAD_DIM), 1.0 / HEAD_DIM)), BF16)
    args = (g_attn[None], w_in.astype(BF16), ln_v_g[None], ln_v_b[None], gq, gk, bd)
    specs = [_const_spec((1, d_model)), _const_spec((d_model, in_width)), _const_spec((1, WIDTH)),
             _const_spec((1, WIDTH)), _const_spec((1, WIDTH)), _const_spec((1, WIDTH)),
             _const_spec((WIDTH, WIDTH))]
    return args, specs


def _in_proj_prompt(x, shared, w_s, b_s, g_out_a, *, tm):
    batch, seq, d_model = x.shape
    n = batch * seq
    nt = seq // tm
    shared_args, shared_specs = shared
    tri = np.tril(np.ones((CHUNK, CHUNK), bool))
    ws = jnp.where(tri, w_s, 0.0).astype(BF16)
    bs = jnp.repeat(b_s.T, HEAD_DIM, axis=1)
    slab = jax.ShapeDtypeStruct((batch, N_SLABS, seq, LANES), F32)
    flat = jax.ShapeDtypeStruct((batch, WIDTH, seq), F32)
    slab_spec = pl.BlockSpec((None, N_SLABS, tm, LANES), lambda i: (i // nt, 0, i % nt, 0))
    flat_spec = pl.BlockSpec((None, WIDTH, tm), lambda i: (i // nt, 0, i % nt))
    row_spec = pl.BlockSpec((tm, WIDTH), lambda i: (i, 0))
    return pl.pallas_call(
        _in_proj_prompt_kernel,
        grid=(n // tm,),
        in_specs=[pl.BlockSpec((tm, d_model), lambda i: (i, 0))] + shared_specs + [
            _const_spec((N_HEADS, CHUNK, CHUNK)), _const_spec((CHUNK, WIDTH)), _const_spec((1, WIDTH))],
        out_specs=[slab_spec, slab_spec, slab_spec, flat_spec, flat_spec, row_spec],
        out_shape=[slab, slab, slab, flat, flat, jax.ShapeDtypeStruct((n, WIDTH), BF16)],
        compiler_params=_params(("parallel",)),
        name="in_proj_prompt",
    )(x.reshape(n, d_model), *shared_args, ws, bs, g_out_a[None])


def _in_proj_sample(x, shared, w_s, b_s, g_out_a):
    n_seq, seq, d_model = x.shape
    n = n_seq * seq
    shared_args, shared_specs = shared
    sub = 8
    assert sub % seq == 0 and seq <= CHUNK
    step = np.arange(sub) % seq
    taps = []
    for d in range(seq):
        src = np.maximum(step - d, 0)
        tap = jnp.where((step >= d)[:, None], w_s[:, step, src].T, 0.0)
        taps.append(jnp.repeat(tap, HEAD_DIM, axis=1))
    coef = jnp.stack(taps)
    bias = jnp.repeat(b_s[:, step].T, HEAD_DIM, axis=1)
    flat = jax.ShapeDtypeStruct((n, WIDTH), F32)
    row_spec = pl.BlockSpec((n, WIDTH), lambda i: (0, 0))
    return pl.pallas_call(
        functools.partial(_in_proj_sample_kernel, seq=seq),
        grid=(1,),
        in_specs=[pl.BlockSpec((n, d_model), lambda i: (0, 0))] + shared_specs + [
            _const_spec((seq, sub, WIDTH)), _const_spec((sub, WIDTH)), _const_spec((1, WIDTH))],
        out_specs=[row_spec] * 5,
        out_shape=[flat, flat, flat, flat, jax.ShapeDtypeStruct((n, WIDTH), BF16)],
        compiler_params=_params(("arbitrary",)),
        name="in_proj_sample",
    )(x.reshape(n, d_model), *shared_args, coef, bias, g_out_a[None])


def _prompt_attention(q, k, v):
    batch, _, seq, _ = q.shape
    n_blocks = seq // CHUNK
    assert all(n_blocks % dil == 0 for _, dil in DILATED_PATTERNS)
    spec = pl.BlockSpec((None, N_SLABS, seq, LANES), lambda b: (b, 0, 0, 0))
    return pl.pallas_call(
        functools.partial(_prompt_attn_kernel, n_blocks=n_blocks),
        grid=(batch,),
        in_specs=[spec, spec, spec],
        out_specs=spec,
        out_shape=jax.ShapeDtypeStruct(q.shape, F32),
        scratch_shapes=[pltpu.VMEM((N_SLABS, seq, LANES), F32), pltpu.VMEM((N_SLABS, seq, LANES), F32)],
        compiler_params=_params(("parallel",)),
        name="prompt_attention",
    )(q, k, v)


def _sample_masks(seq, cache_len, past_len, n_rows):
    cache = np.zeros((n_rows, cache_len), np.float32)
    new = [[False] * n_rows for _ in range(seq)]
    for b, (window, dil) in enumerate(DILATED_PATTERNS):
        for t in range(seq):
            for jj in range(window // dil + 1):
                idx = cache_len + t - dil * jj
                if idx < 0 or idx + (past_len - cache_len) < 0:
                    continue
                if idx < cache_len:
                    cache[b * seq + t, idx] = 1.0
                else:
                    new[idx - cache_len][b * seq + t] = True
    return cache, new


def _sample_attention(q, k, v, cache_k, cache_v, past_len):
    n_seq, seq, _ = q.shape
    cache_len = cache_k.shape[1]
    n_rows = 16
    n_br = len(DILATED_PATTERNS)
    assert n_br * seq <= n_rows and n_rows % seq == 0
    mask, new_valid = _sample_masks(seq, cache_len, past_len, n_rows)

    def per_head(a, rows):
        a = a.reshape(n_seq, seq, N_HEADS, HEAD_DIM).transpose(0, 2, 1, 3)
        return a

    qh = per_head(q, seq)
    q16 = jnp.pad(jnp.tile(qh, (1, 1, n_br, 1)), ((0, 0), (0, 0), (0, n_rows - n_br * seq), (0, 0)))
    kn = jnp.pad(per_head(k, seq), ((0, 0), (0, 0), (0, 8 - seq), (0, 0)))
    vn = jnp.pad(per_head(v, seq), ((0, 0), (0, 0), (0, 8 - seq), (0, 0)))
    kt = jnp.transpose(cache_k, (0, 2, 3, 1))
    vt = jnp.transpose(cache_v, (0, 2, 3, 1))
    head_spec = lambda r: pl.BlockSpec((None, N_HEADS, r, HEAD_DIM), lambda i: (i, 0, 0, 0))
    cache_spec = pl.BlockSpec((None, N_HEADS, HEAD_DIM, cache_len), lambda i: (i, 0, 0, 0))
    out = pl.pallas_call(
        functools.partial(_sample_attn_kernel, seq=seq, new_valid=new_valid),
        grid=(n_seq,),
        in_specs=[head_spec(n_rows), head_spec(8), head_spec(8), cache_spec, cache_spec,
                  _const_spec((n_rows, cache_len))],
        out_specs=head_spec(n_rows),
        out_shape=jax.ShapeDtypeStruct((n_seq, N_HEADS, n_rows, HEAD_DIM), F32),
        scratch_shapes=[pltpu.VMEM((N_HEADS, n_rows, cache_len), F32),
                        pltpu.VMEM((N_HEADS, n_rows, cache_len), BF16)],
        compiler_params=_params(("parallel",)),
        name="sample_attention",
    )(q16, kn, vn, kt, vt, jnp.asarray(mask))
    b_out = out[:, :, :seq, :].transpose(0, 2, 1, 3).reshape(n_seq * seq, N_SLABS, LANES)
    return b_out.transpose(1, 0, 2)[None]


def _finish(x2d, mix_a, b_slab, g_out_b, w_o, g_ffn, w_gate, w_up, w_down, *, tm):
    n, d_model = x2d.shape
    batch, _, seq, _ = b_slab.shape
    nt = seq // tm
    d_ff = w_gate.shape[1]
    return pl.pallas_call(
        _finish_kernel,
        grid=(n // tm,),
        in_specs=[pl.BlockSpec((tm, d_model), lambda i: (i, 0)),
                  pl.BlockSpec((tm, WIDTH), lambda i: (i, 0)),
                  pl.BlockSpec((None, N_SLABS, tm, LANES), lambda i: (i // nt, 0, i % nt, 0)),
                  _const_spec((1, WIDTH)), _const_spec((2 * WIDTH, d_model)), _const_spec((1, d_model)),
                  _const_spec((d_model, d_ff)), _const_spec((d_model, d_ff)), _const_spec((d_ff, d_model))],
        out_specs=pl.BlockSpec((tm, d_model), lambda i: (i, 0)),
        out_shape=jax.ShapeDtypeStruct((n, d_model), F32),
        compiler_params=_params(("parallel",)),
        name="finish",
    )(x2d, mix_a, b_slab, g_out_b[None], w_o.astype(BF16), g_ffn[None],
      w_gate.astype(BF16), w_up.astype(BF16), w_down.astype(BF16))


def kernel(x_prompt, x_sample, cache_k, cache_v, g_attn, w_in, ln_v_g, ln_v_b, w_s, b_s, g_q, g_k,
           g_out_a, g_out_b, w_o, g_ffn, w_gate, w_up, w_down):
    depth = w_in.shape[0]
    assert depth == 1, "single-layer step"
    batch, seq, d_model = x_prompt.shape
    n_seq, dec_seq, _ = x_sample.shape
    l = 0
    shared = _shared_in_proj_args(g_attn[l], w_in[l], ln_v_g[l], ln_v_b[l], g_q[l], g_k[l])
    fin = (g_out_b[l], w_o[l], g_ffn[l], w_gate[l], w_up[l], w_down[l])

    q, k, v, k_out, v_out, mix_a = _in_proj_prompt(x_prompt, shared, w_s[l], b_s[l], g_out_a[l], tm=256)
    b_slab = _prompt_attention(q, k, v)
    y_prompt = _finish(x_prompt.reshape(batch * seq, d_model), mix_a, b_slab, *fin, tm=256)
    win = min(max(w for w, _ in DILATED_PATTERNS), seq)
    window = lambda a: a.reshape(batch, N_HEADS, HEAD_DIM, seq).transpose(0, 3, 1, 2)[:, -win:][None]
    new_k_prompt = window(k_out)
    new_v_prompt = window(v_out)

    qs, ks, vs, vns, mix_as = _in_proj_sample(x_sample, shared, w_s[l], b_s[l], g_out_a[l])
    as3 = lambda a: a.reshape(n_seq, dec_seq, WIDTH)
    bs_slab = _sample_attention(as3(qs), as3(ks), as3(vs), cache_k[l], cache_v[l], PAST_LEN)
    y_sample = _finish(x_sample.reshape(n_seq * dec_seq, d_model), mix_as, bs_slab, *fin, tm=256)

    head_shape = (1, n_seq, dec_seq, N_HEADS, HEAD_DIM)
    return (y_prompt.reshape(batch, seq, d_model), y_sample.reshape(n_seq, dec_seq, d_model),
            new_k_prompt, new_v_prompt, ks.reshape(head_shape), vs.reshape(head_shape),
            vns.reshape(1, n_seq, dec_seq, WIDTH))
```

```python
import functools

import numpy as np
import jax
import jax.numpy as jnp
from jax import lax
from jax.experimental import pallas as pl
from jax.experimental.pallas import tpu as pltpu

F32 = jnp.float32
BF16 = jnp.bfloat16

HEAD_DIM = 64
N_HEADS = 8
WIDTH = N_HEADS * HEAD_DIM
CHUNK = 128
DILATED_PATTERNS = ((128, 1), (512, 4), (2048, 16))
EPS = 1e-6
PAST_LEN = 8192
NEG_BIG = -1e30

LANES = 128
N_SLABS = WIDTH // LANES
VMEM_LIMIT = 56 * 1024 * 1024


def _dot(a, b):
    return jnp.dot(a, b, preferred_element_type=F32)


def _dot_nt(a, b):
    return lax.dot_general(a, b, (((1,), (1,)), ((), ())), preferred_element_type=F32)


def _const_spec(shape):
    return pl.BlockSpec(shape, lambda *_: (0,) * len(shape), pipeline_mode=pl.Buffered(1))


def _head_rms(z, bd_ref, g_ref):
    sq = z * z
    hi = sq.astype(BF16)
    lo = (sq - hi.astype(F32)).astype(BF16)
    msq = _dot(hi, bd_ref[...]) + _dot(lo, bd_ref[...])
    return z * lax.rsqrt(msq + EPS) * g_ref[...]


def _features(x_ref, g_attn_ref, w_in_ref, ln_g_ref, ln_b_ref, gq_ref, gk_ref, bd_ref):
    x = x_ref[...]
    ms = jnp.mean(x * x, axis=-1, keepdims=True)
    xn = (x * lax.rsqrt(ms + EPS) * g_attn_ref[...]).astype(BF16)

    def proj(j):
        return _dot(xn, w_in_ref[:, j * WIDTH:(j + 1) * WIDTH])

    u = jax.nn.gelu(proj(0))
    gv = jax.nn.gelu(proj(1))
    mu = jnp.mean(gv, axis=-1, keepdims=True)
    xc = gv - mu
    var = jnp.mean(xc * xc, axis=-1, keepdims=True)
    vn = xc * lax.rsqrt(var + EPS) * ln_g_ref[...] + ln_b_ref[...]
    q = _head_rms(proj(2), bd_ref, gq_ref)
    k = _head_rms(proj(3), bd_ref, gk_ref)
    v = proj(4)
    return u, vn, q, k, v


def _group_norm_rows(slabs, g_ref):
    ssq = sum(jnp.sum(a * a, axis=-1, keepdims=True) for a in slabs)
    r = lax.rsqrt(ssq * (1.0 / WIDTH) + EPS)
    return [a * r * g_ref[:, s * LANES:(s + 1) * LANES] for s, a in enumerate(slabs)]


def _in_proj_prompt_kernel(x_ref, g_attn_ref, w_in_ref, ln_g_ref, ln_b_ref, gq_ref, gk_ref, bd_ref,
                           ws_ref, bs_ref, g_out_a_ref,
                           q_ref, k_ref, v_ref, kout_ref, vout_ref, mixa_ref):
    u, vn, q, k, v = _features(x_ref, g_attn_ref, w_in_ref, ln_g_ref, ln_b_ref, gq_ref, gk_ref, bd_ref)
    tm = u.shape[0]
    first_head = lax.broadcasted_iota(jnp.int32, (CHUNK, LANES), 1) < HEAD_DIM
    for c in range(tm // CHUNK):
        rows = slice(c * CHUNK, (c + 1) * CHUNK)
        vn_c = vn[rows].astype(BF16)
        slabs = []
        for s in range(N_SLABS):
            cols = slice(s * LANES, (s + 1) * LANES)
            g0 = _dot(ws_ref[2 * s], vn_c[:, cols])
            g1 = _dot(ws_ref[2 * s + 1], vn_c[:, cols])
            gate = jnp.where(first_head, g0, g1) + bs_ref[:, cols]
            slabs.append(u[rows, cols] * gate)
        for s, a in enumerate(_group_norm_rows(slabs, g_out_a_ref)):
            mixa_ref[rows, s * LANES:(s + 1) * LANES] = a.astype(BF16)
    for s in range(N_SLABS):
        cols = slice(s * LANES, (s + 1) * LANES)
        q_ref[s] = q[:, cols]
        k_ref[s] = k[:, cols]
        v_ref[s] = v[:, cols]
    kout_ref[...] = k.T
    vout_ref[...] = v.T


def _in_proj_sample_kernel(x_ref, g_attn_ref, w_in_ref, ln_g_ref, ln_b_ref, gq_ref, gk_ref, bd_ref,
                           coef_ref, bias_ref, g_out_a_ref,
                           q_ref, k_ref, v_ref, vn_ref, mixa_ref, *, seq):
    u, vn, q, k, v = _features(x_ref, g_attn_ref, w_in_ref, ln_g_ref, ln_b_ref, gq_ref, gk_ref, bd_ref)
    tm = u.shape[0]
    sub = coef_ref.shape[1]
    gate = jnp.zeros((tm // sub, sub, WIDTH), F32) + bias_ref[...]
    for d in range(seq):
        shifted = vn if d == 0 else pltpu.roll(vn, d, axis=0)
        gate = gate + shifted.reshape(tm // sub, sub, WIDTH) * coef_ref[d]
    a = u * gate.reshape(tm, WIDTH)
    slabs = _group_norm_rows([a[:, s * LANES:(s + 1) * LANES] for s in range(N_SLABS)], g_out_a_ref)
    for s, a_s in enumerate(slabs):
        mixa_ref[:, s * LANES:(s + 1) * LANES] = a_s.astype(BF16)
    q_ref[...] = q
    k_ref[...] = k
    v_ref[...] = v
    vn_ref[...] = vn


def _prompt_attn_kernel(q_ref, k_ref, v_ref, o_ref, m_sc, l_sc, *, n_blocks):
    (w_near, d_near), (w_mid, d_mid), (w_far, d_far) = DILATED_PATTERNS
    assert d_near == 1 and d_far == n_blocks and n_blocks % d_mid == 0
    assert w_near // d_near == w_mid // d_mid == w_far // d_far == CHUNK
    mid_blocks = n_blocks // d_mid

    first_head = lax.broadcasted_iota(jnp.int32, (CHUNK, LANES), 1) < HEAD_DIM
    row = lax.broadcasted_iota(jnp.int32, (2 * CHUNK, 2 * CHUNK), 0) % CHUNK
    col = lax.broadcasted_iota(jnp.int32, (2 * CHUNK, 2 * CHUNK), 1)
    band = (col >= row) & (col <= row + CHUNK)
    no_prev = col >= CHUNK
    causal = (lax.broadcasted_iota(jnp.int32, (2 * CHUNK, CHUNK), 1)
              <= lax.broadcasted_iota(jnp.int32, (2 * CHUNK, CHUNK), 0) % CHUNK)

    def attend(s, cur, prev, mask):
        qb = q_ref[s, cur, :].astype(BF16)
        zero = jnp.zeros_like(qb)
        lhs = jnp.concatenate([jnp.where(first_head, qb, zero), jnp.where(first_head, zero, qb)], axis=0)
        kk = k_ref[s, cur, :]
        vv = v_ref[s, cur, :]
        if prev is not None:
            kk = jnp.concatenate([k_ref[s, prev, :], kk], axis=0)
            vv = jnp.concatenate([v_ref[s, prev, :], vv], axis=0)
        sc = jnp.where(mask, _dot_nt(lhs, kk.astype(BF16)), -jnp.inf)
        m = jnp.max(sc, axis=-1, keepdims=True)
        p = jnp.exp(sc - m)
        l = jnp.sum(p, axis=-1, keepdims=True)
        acc = _dot(p.astype(BF16), vv.astype(BF16))
        both = lambda a: jnp.where(first_head, a[:CHUNK], a[CHUNK:])
        return both(m), both(l), both(acc)

    def merged(s, cur, stats):
        m_new, l_new, acc_new = stats
        m_old = m_sc[s, cur, :]
        m_tot = jnp.maximum(m_old, m_new)
        a_old = jnp.exp(m_old - m_tot)
        a_new = jnp.exp(m_new - m_tot)
        return (m_tot, a_old * l_sc[s, cur, :] + a_new * l_new, a_old * o_ref[s, cur, :] + a_new * acc_new)

    def far_class(c, carry):
        cur = pl.ds(c, CHUNK, stride=d_far)
        for s in range(N_SLABS):
            m_sc[s, cur, :], l_sc[s, cur, :], o_ref[s, cur, :] = attend(s, cur, None, causal)
        return carry

    lax.fori_loop(0, d_far, far_class, 0)

    def span(jb, carry):
        for r in range(d_mid):
            cur = pl.ds(r + (d_mid * CHUNK) * jb, CHUNK, stride=d_mid)
            prev = pl.ds(r + (d_mid * CHUNK) * jnp.maximum(jb - 1, 0), CHUNK, stride=d_mid)
            mask = band & (no_prev | (jb > 0))
            for s in range(N_SLABS):
                m_sc[s, cur, :], l_sc[s, cur, :], o_ref[s, cur, :] = merged(s, cur, attend(s, cur, prev, mask))
        for i in range(d_mid):
            j = jb * d_mid + i
            cur = pl.ds(pl.multiple_of(j * CHUNK, CHUNK), CHUNK)
            prev = pl.ds(pl.multiple_of(jnp.maximum(j - 1, 0) * CHUNK, CHUNK), CHUNK)
            mask = band & (no_prev | (j > 0))
            for s in range(N_SLABS):
                _, l_tot, acc_tot = merged(s, cur, attend(s, cur, prev, mask))
                o_ref[s, cur, :] = acc_tot / l_tot
        return carry

    lax.fori_loop(0, mid_blocks, span, 0)


def _sample_attn_kernel(q_ref, kn_ref, vn_ref, kt_ref, vt_ref, mask_ref, o_ref, s_sc, p_sc, *, seq, new_valid):
    n_rows = q_ref.shape[1]
    n_real = len(DILATED_PATTERNS) * seq
    row = lax.broadcasted_iota(jnp.int32, (n_rows, 1), 0)
    real = row < n_real
    valid = mask_ref[...] > 0.5
    new_rows = []
    for tp in range(seq):
        sel = functools.reduce(jnp.logical_or, [row == r for r in range(n_rows) if new_valid[tp][r]])
        new_rows.append(sel)

    heads = range(N_HEADS)
    for h in heads:
        sc = _dot(q_ref[h].astype(BF16), kt_ref[h].astype(BF16))
        s_sc[h] = jnp.where(valid, sc, -jnp.inf)

    weights = []
    for h in heads:
        q = q_ref[h]
        kn = kn_ref[h]
        s_new = [jnp.where(new_rows[tp], jnp.sum(q * kn[tp:tp + 1, :], axis=-1, keepdims=True), -jnp.inf)
                 for tp in range(seq)]
        sc = s_sc[h]
        m = jnp.max(sc, axis=-1, keepdims=True)
        for sn in s_new:
            m = jnp.maximum(m, sn)
        m = jnp.where(real, m, 0.0)
        p = jnp.exp(sc - m)
        p_sc[h] = p.astype(BF16)
        p_new = [jnp.exp(sn - m) for sn in s_new]
        l = jnp.sum(p, axis=-1, keepdims=True) + sum(p_new)
        m_eff = jnp.where(real, m, NEG_BIG)
        l_eff = jnp.where(real, l, 0.0)
        m_all = m_eff
        for i in range(1, n_rows // seq):
            m_all = jnp.maximum(m_all, pltpu.roll(m_eff, i * seq, axis=0))
        c = jnp.exp(m_eff - m_all)
        cl = c * l_eff
        den = cl
        for i in range(1, n_rows // seq):
            den = den + pltpu.roll(cl, i * seq, axis=0)
        w = jnp.where(real, c / den, 0.0)
        weights.append((w, [pn * w for pn in p_new]))

    for h in heads:
        w, pw_new = weights[h]
        vn = vn_ref[h]
        out = _dot_nt(p_sc[h], vt_ref[h].astype(BF16)) * w
        for tp in range(seq):
            out = out + pw_new[tp] * vn[tp:tp + 1, :]
        tot = out
        for i in range(1, n_rows // seq):
            tot = tot + pltpu.roll(out, i * seq, axis=0)
        o_ref[h] = tot


def _finish_kernel(x_ref, mixa_ref, b_ref, g_out_b_ref, wo_ref, g_ffn_ref, wg_ref, wu_ref, wd_ref, o_ref):
    slabs = _group_norm_rows([b_ref[s] for s in range(N_SLABS)], g_out_b_ref)
    mix_b = jnp.concatenate([a.astype(BF16) for a in slabs], axis=-1)
    x1 = x_ref[...] + _dot(mixa_ref[...], wo_ref[:WIDTH, :]) + _dot(mix_b, wo_ref[WIDTH:, :])
    ms = jnp.mean(x1 * x1, axis=-1, keepdims=True)
    h = (x1 * lax.rsqrt(ms + EPS) * g_ffn_ref[...]).astype(BF16)
    act = (jax.nn.silu(_dot(h, wg_ref[...])) * _dot(h, wu_ref[...])).astype(BF16)
    o_ref[...] = x1 + _dot(act, wd_ref[...])


def _params(sem):
    return pltpu.CompilerParams(dimension_semantics=sem, vmem_limit_bytes=VMEM_LIMIT)


def _shared_in_proj_args(g_attn, w_in, ln_v_g, ln_v_b, g_q, g_k):
    d_model, in_width = w_in.shape
    gq = (jnp.tile(g_q, N_HEADS) * (HEAD_DIM ** -0.5))[None]
    gk = jnp.tile(g_k, N_HEADS)[None]
    bd = jnp.asarray(np.kron(np.eye(N_HEADS), np.full((HEAD_DIM, HEAD_DIM), 1.0 / HEAD_DIM)), BF16)
    args = (g_attn[None], w_in.astype(BF16), ln_v_g[None], ln_v_b[None], gq, gk, bd)
    specs = [_const_spec((1, d_model)), _const_spec((d_model, in_width)), _const_spec((1, WIDTH)),
             _const_spec((1, WIDTH)), _const_spec((1, WIDTH)), _const_spec((1, WIDTH)),
             _const_spec((WIDTH, WIDTH))]
    return args, specs


def _in_proj_prompt(x, shared, w_s, b_s, g_out_a, *, tm):
    batch, seq, d_model = x.shape
    n = batch * seq
    nt = seq // tm
    shared_args, shared_specs = shared
    tri = np.tril(np.ones((CHUNK, CHUNK), bool))
    ws = jnp.where(tri, w_s, 0.0).astype(BF16)
    bs = jnp.repeat(b_s.T, HEAD_DIM, axis=1)
    slab = jax.ShapeDtypeStruct((batch, N_SLABS, seq, LANES), F32)
    flat = jax.ShapeDtypeStruct((batch, WIDTH, seq), F32)
    slab_spec = pl.BlockSpec((None, N_SLABS, tm, LANES), lambda i: (i // nt, 0, i % nt, 0))
    flat_spec = pl.BlockSpec((None, WIDTH, tm), lambda i: (i // nt, 0, i % nt))
    row_spec = pl.BlockSpec((tm, WIDTH), lambda i: (i, 0))
    return pl.pallas_call(
        _in_proj_prompt_kernel,
        grid=(n // tm,),
        in_specs=[pl.BlockSpec((tm, d_model), lambda i: (i, 0))] + shared_specs + [
            _const_spec((N_HEADS, CHUNK, CHUNK)), _const_spec((CHUNK, WIDTH)), _const_spec((1, WIDTH))],
        out_specs=[slab_spec, slab_spec, slab_spec, flat_spec, flat_spec, row_spec],
        out_shape=[slab, slab, slab, flat, flat, jax.ShapeDtypeStruct((n, WIDTH), BF16)],
        compiler_params=_params(("parallel",)),
        name="in_proj_prompt",
    )(x.reshape(n, d_model), *shared_args, ws, bs, g_out_a[None])


def _in_proj_sample(x, shared, w_s, b_s, g_out_a):
    n_seq, seq, d_model = x.shape
    n = n_seq * seq
    shared_args, shared_specs = shared
    sub = 8
    assert sub % seq == 0 and seq <= CHUNK
    step = np.arange(sub) % seq
    taps = []
    for d in range(seq):
        src = np.maximum(step - d, 0)
        tap = jnp.where((step >= d)[:, None], w_s[:, step, src].T, 0.0)
        taps.append(jnp.repeat(tap, HEAD_DIM, axis=1))
    coef = jnp.stack(taps)
    bias = jnp.repeat(b_s[:, step].T, HEAD_DIM, axis=1)
    flat = jax.ShapeDtypeStruct((n, WIDTH), F32)
    row_spec = pl.BlockSpec((n, WIDTH), lambda i: (0, 0))
    return pl.pallas_call(
        functools.partial(_in_proj_sample_kernel, seq=seq),
        grid=(1,),
        in_specs=[pl.BlockSpec((n, d_model), lambda i: (0, 0))] + shared_specs + [
            _const_spec((seq, sub, WIDTH)), _const_spec((sub, WIDTH)), _const_spec((1, WIDTH))],
        out_specs=[row_spec] * 5,
        out_shape=[flat, flat, flat, flat, jax.ShapeDtypeStruct((n, WIDTH), BF16)],
        compiler_params=_params(("arbitrary",)),
        name="in_proj_sample",
    )(x.reshape(n, d_model), *shared_args, coef, bias, g_out_a[None])


def _prompt_attention(q, k, v):
    batch, _, seq, _ = q.shape
    n_blocks = seq // CHUNK
    assert all(n_blocks % dil == 0 for _, dil in DILATED_PATTERNS)
    spec = pl.BlockSpec((None, N_SLABS, seq, LANES), lambda b: (b, 0, 0, 0))
    return pl.pallas_call(
        functools.partial(_prompt_attn_kernel, n_blocks=n_blocks),
        grid=(batch,),
        in_specs=[spec, spec, spec],
        out_specs=spec,
        out_shape=jax.ShapeDtypeStruct(q.shape, F32),
        scratch_shapes=[pltpu.VMEM((N_SLABS, seq, LANES), F32), pltpu.VMEM((N_SLABS, seq, LANES), F32)],
        compiler_params=_params(("parallel",)),
        name="prompt_attention",
    )(q, k, v)


def _sample_masks(seq, cache_len, past_len, n_rows):
    cache = np.zeros((n_rows, cache_len), np.float32)
    new = [[False] * n_rows for _ in range(seq)]
    for b, (window, dil) in enumerate(DILATED_PATTERNS):
        for t in range(seq):
            for jj in range(window // dil + 1):
                idx = cache_len + t - dil * jj
                if idx < 0 or idx + (past_len - cache_len) < 0:
                    continue
                if idx < cache_len:
                    cache[b * seq + t, idx] = 1.0
                else:
                    new[idx - cache_len][b * seq + t] = True
    return cache, new


def _sample_attention_call(q, k, v, cache_k, cache_v, past_len):
    n_seq, seq, _ = q.shape
    cache_len = cache_k.shape[1]
    n_rows = 16
    n_br = len(DILATED_PATTERNS)
    assert n_br * seq <= n_rows and n_rows % seq == 0
    mask, new_valid = _sample_masks(seq, cache_len, past_len, n_rows)
    per_head = lambda a: a.reshape(n_seq, seq, N_HEADS, HEAD_DIM).transpose(0, 2, 1, 3)
    pad_rows = lambda a, rows: jnp.pad(a, ((0, 0), (0, 0), (0, rows - a.shape[2]), (0, 0)))
    q16 = pad_rows(jnp.tile(per_head(q), (1, 1, n_br, 1)), n_rows)
    kn = pad_rows(per_head(k), 8)
    vn = pad_rows(per_head(v), 8)
    kt = jnp.transpose(cache_k, (0, 2, 3, 1))
    vt = jnp.transpose(cache_v, (0, 2, 3, 1))
    head_spec = lambda r: pl.BlockSpec((None, N_HEADS, r, HEAD_DIM), lambda i: (i, 0, 0, 0))
    cache_spec = pl.BlockSpec((None, N_HEADS, HEAD_DIM, cache_len), lambda i: (i, 0, 0, 0))
    return dict(
        body=functools.partial(_sample_attn_kernel, seq=seq, new_valid=new_valid),
        args=(q16, kn, vn, kt, vt, jnp.asarray(mask)),
        in_specs=[head_spec(n_rows), head_spec(8), head_spec(8), cache_spec, cache_spec,
                  _const_spec((n_rows, cache_len))],
        out_spec=head_spec(n_rows),
        out_shape=jax.ShapeDtypeStruct((n_seq, N_HEADS, n_rows, HEAD_DIM), F32),
        scratch=[pltpu.VMEM((N_HEADS, n_rows, cache_len), F32), pltpu.VMEM((N_HEADS, n_rows, cache_len), BF16)],
    )


def _sample_attention_slabs(out, seq):
    n_seq = out.shape[0]
    b_out = out[:, :, :seq, :].transpose(0, 2, 1, 3).reshape(n_seq * seq, N_SLABS, LANES)
    return b_out.transpose(1, 0, 2)[None]


def _finish_call(x2d, mix_a, b_slab, g_out_b, w_o, g_ffn, w_gate, w_up, w_down, *, tm):
    n, d_model = x2d.shape
    nt = b_slab.shape[2] // tm
    d_ff = w_gate.shape[1]
    return dict(
        args=(x2d, mix_a, b_slab, g_out_b[None], w_o.astype(BF16), g_ffn[None],
              w_gate.astype(BF16), w_up.astype(BF16), w_down.astype(BF16)),
        in_specs=[pl.BlockSpec((tm, d_model), lambda i: (i, 0)),
                  pl.BlockSpec((tm, WIDTH), lambda i: (i, 0)),
                  pl.BlockSpec((None, N_SLABS, tm, LANES), lambda i: (i // nt, 0, i % nt, 0)),
                  _const_spec((1, WIDTH)), _const_spec((2 * WIDTH, d_model)), _const_spec((1, d_model)),
                  _const_spec((d_model, d_ff)), _const_spec((d_model, d_ff)), _const_spec((d_ff, d_model))],
        out_spec=pl.BlockSpec((tm, d_model), lambda i: (i, 0)),
        out_shape=jax.ShapeDtypeStruct((n, d_model), F32),
        steps=n // tm,
    )


def _finish(x2d, mix_a, b_slab, *weights, tm):
    c = _finish_call(x2d, mix_a, b_slab, *weights, tm=tm)
    return pl.pallas_call(
        _finish_kernel, grid=(c["steps"],), in_specs=c["in_specs"], out_specs=c["out_spec"],
        out_shape=c["out_shape"], compiler_params=_params(("parallel",)), name="finish",
    )(*c["args"])


def _finish_and_sample_attention(x2d, mix_a, b_slab, weights, q, k, v, cache_k, cache_v, past_len):
    n_seq, seq, _ = q.shape
    n = x2d.shape[0]
    assert n % n_seq == 0
    fin = _finish_call(x2d, mix_a, b_slab, *weights, tm=n // n_seq)
    att = _sample_attention_call(q, k, v, cache_k, cache_v, past_len)
    n_fin = len(fin["args"])
    n_att = len(att["args"])

    def body(*refs):
        fin_in, att_in = refs[:n_fin], refs[n_fin:n_fin + n_att]
        y_ref, o_ref = refs[n_fin + n_att:n_fin + n_att + 2]
        scratch = refs[n_fin + n_att + 2:]
        att["body"](*att_in, o_ref, *scratch)
        _finish_kernel(*fin_in, y_ref)

    y, out = pl.pallas_call(
        body, grid=(n_seq,), in_specs=fin["in_specs"] + att["in_specs"],
        out_specs=[fin["out_spec"], att["out_spec"]], out_shape=[fin["out_shape"], att["out_shape"]],
        scratch_shapes=att["scratch"], compiler_params=_params(("parallel",)),
        name="finish_and_sample_attention",
    )(*fin["args"], *att["args"])
    return y, _sample_attention_slabs(out, seq)


def kernel(x_prompt, x_sample, cache_k, cache_v, g_attn, w_in, ln_v_g, ln_v_b, w_s, b_s, g_q, g_k,
           g_out_a, g_out_b, w_o, g_ffn, w_gate, w_up, w_down):
    depth = w_in.shape[0]
    assert depth == 1, "single-layer step"
    batch, seq, d_model = x_prompt.shape
    n_seq, dec_seq, _ = x_sample.shape
    l = 0
    shared = _shared_in_proj_args(g_attn[l], w_in[l], ln_v_g[l], ln_v_b[l], g_q[l], g_k[l])
    fin = (g_out_b[l], w_o[l], g_ffn[l], w_gate[l], w_up[l], w_down[l])

    q, k, v, k_out, v_out, mix_a = _in_proj_prompt(x_prompt, shared, w_s[l], b_s[l], g_out_a[l], tm=256)
    b_slab = _prompt_attention(q, k, v)
    win = min(max(w for w, _ in DILATED_PATTERNS), seq)
    window = lambda a: a.reshape(batch, N_HEADS, HEAD_DIM, seq).transpose(0, 3, 1, 2)[:, -win:][None]
    new_k_prompt = window(k_out)
    new_v_prompt = window(v_out)

    qs, ks, vs, vns, mix_as = _in_proj_sample(x_sample, shared, w_s[l], b_s[l], g_out_a[l])
    as3 = lambda a: a.reshape(n_seq, dec_seq, WIDTH)
    y_prompt, bs_slab = _finish_and_sample_attention(
        x_prompt.reshape(batch * seq, d_model), mix_a, b_slab, fin,
        as3(qs), as3(ks), as3(vs), cache_k[l], cache_v[l], PAST_LEN)
    y_sample = _finish(x_sample.reshape(n_seq * dec_seq, d_model), mix_as, bs_slab, *fin, tm=256)

    head_shape = (1, n_seq, dec_seq, N_HEADS, HEAD_DIM)
    return (y_prompt.reshape(batch, seq, d_model), y_sample.reshape(n_seq, dec_seq, d_model),
            new_k_prompt, new_v_prompt, ks.reshape(head_shape), vs.reshape(head_shape),
            vns.reshape(1, n_seq, dec_seq, WIDTH))
```

```python
import functools

import numpy as np
import jax
import jax.numpy as jnp
from jax import lax
from jax.experimental import pallas as pl
from jax.experimental.pallas import tpu as pltpu

F32 = jnp.float32
BF16 = jnp.bfloat16

HEAD_DIM = 64
N_HEADS = 8
WIDTH = N_HEADS * HEAD_DIM
CHUNK = 128
DILATED_PATTERNS = ((128, 1), (512, 4), (2048, 16))
EPS = 1e-6
PAST_LEN = 8192
NEG_BIG = -1e30

LANES = 128
N_SLABS = WIDTH // LANES
VMEM_LIMIT = 56 * 1024 * 1024


def _dot(a, b):
    return jnp.dot(a, b, preferred_element_type=F32)


def _dot_nt(a, b):
    return lax.dot_general(a, b, (((1,), (1,)), ((), ())), preferred_element_type=F32)


def _const_spec(shape):
    return pl.BlockSpec(shape, lambda *_: (0,) * len(shape), pipeline_mode=pl.Buffered(1))


def _head_rms(z, bd_ref, g_ref):
    sq = z * z
    hi = sq.astype(BF16)
    lo = (sq - hi.astype(F32)).astype(BF16)
    msq = _dot(hi, bd_ref[...]) + _dot(lo, bd_ref[...])
    return z * lax.rsqrt(msq + EPS) * g_ref[...]


def _features(x_ref, g_attn_ref, w_in_ref, ln_g_ref, ln_b_ref, gq_ref, gk_ref, bd_ref):
    x = x_ref[...]
    ms = jnp.mean(x * x, axis=-1, keepdims=True)
    xn = (x * lax.rsqrt(ms + EPS) * g_attn_ref[...]).astype(BF16)

    def proj(j):
        return _dot(xn, w_in_ref[:, j * WIDTH:(j + 1) * WIDTH])

    u = jax.nn.gelu(proj(0))
    gv = jax.nn.gelu(proj(1))
    mu = jnp.mean(gv, axis=-1, keepdims=True)
    xc = gv - mu
    var = jnp.mean(xc * xc, axis=-1, keepdims=True)
    vn = xc * lax.rsqrt(var + EPS) * ln_g_ref[...] + ln_b_ref[...]
    q = _head_rms(proj(2), bd_ref, gq_ref)
    k = _head_rms(proj(3), bd_ref, gk_ref)
    v = proj(4)
    return u, vn, q, k, v


def _group_norm_rows(slabs, g_ref):
    ssq = sum(jnp.sum(a * a, axis=-1, keepdims=True) for a in slabs)
    r = lax.rsqrt(ssq * (1.0 / WIDTH) + EPS)
    return [a * r * g_ref[:, s * LANES:(s + 1) * LANES] for s, a in enumerate(slabs)]


def _in_proj_prompt_stages(x_ref, g_attn_ref, w_in_ref, ln_g_ref, ln_b_ref, gq_ref, gk_ref, bd_ref,
                           ws_ref, bs_ref, g_out_a_ref,
                           q_ref, k_ref, v_ref, kout_ref, vout_ref, mixa_ref):
    u, vn, q, k, v = _features(x_ref, g_attn_ref, w_in_ref, ln_g_ref, ln_b_ref, gq_ref, gk_ref, bd_ref)
    yield
    tm = u.shape[0]
    first_head = lax.broadcasted_iota(jnp.int32, (CHUNK, LANES), 1) < HEAD_DIM
    for c in range(tm // CHUNK):
        rows = slice(c * CHUNK, (c + 1) * CHUNK)
        vn_c = vn[rows].astype(BF16)
        slabs = []
        for s in range(N_SLABS):
            cols = slice(s * LANES, (s + 1) * LANES)
            g0 = _dot(ws_ref[2 * s], vn_c[:, cols])
            g1 = _dot(ws_ref[2 * s + 1], vn_c[:, cols])
            gate = jnp.where(first_head, g0, g1) + bs_ref[:, cols]
            slabs.append(u[rows, cols] * gate)
        for s, a in enumerate(_group_norm_rows(slabs, g_out_a_ref)):
            mixa_ref[rows, s * LANES:(s + 1) * LANES] = a.astype(BF16)
    for s in range(N_SLABS):
        cols = slice(s * LANES, (s + 1) * LANES)
        q_ref[s] = q[:, cols]
        k_ref[s] = k[:, cols]
        v_ref[s] = v[:, cols]
    kout_ref[...] = k.T
    vout_ref[...] = v.T


def _in_proj_sample_kernel(x_ref, g_attn_ref, w_in_ref, ln_g_ref, ln_b_ref, gq_ref, gk_ref, bd_ref,
                           coef_ref, bias_ref, g_out_a_ref,
                           q_ref, k_ref, v_ref, vn_ref, mixa_ref, *, seq):
    u, vn, q, k, v = _features(x_ref, g_attn_ref, w_in_ref, ln_g_ref, ln_b_ref, gq_ref, gk_ref, bd_ref)
    tm = u.shape[0]
    sub = coef_ref.shape[1]
    gate = jnp.zeros((tm // sub, sub, WIDTH), F32) + bias_ref[...]
    for d in range(seq):
        shifted = vn if d == 0 else pltpu.roll(vn, d, axis=0)
        gate = gate + shifted.reshape(tm // sub, sub, WIDTH) * coef_ref[d]
    a = u * gate.reshape(tm, WIDTH)
    slabs = _group_norm_rows([a[:, s * LANES:(s + 1) * LANES] for s in range(N_SLABS)], g_out_a_ref)
    for s, a_s in enumerate(slabs):
        mixa_ref[:, s * LANES:(s + 1) * LANES] = a_s.astype(BF16)
    q_ref[...] = q
    k_ref[...] = k
    v_ref[...] = v
    vn_ref[...] = vn


def _prompt_attn_kernel(q_ref, k_ref, v_ref, o_ref, m_sc, l_sc, *, n_blocks):
    (w_near, d_near), (w_mid, d_mid), (w_far, d_far) = DILATED_PATTERNS
    assert d_near == 1 and d_far == n_blocks and n_blocks % d_mid == 0
    assert w_near // d_near == w_mid // d_mid == w_far // d_far == CHUNK
    mid_blocks = n_blocks // d_mid

    first_head = lax.broadcasted_iota(jnp.int32, (CHUNK, LANES), 1) < HEAD_DIM
    row = lax.broadcasted_iota(jnp.int32, (2 * CHUNK, 2 * CHUNK), 0) % CHUNK
    col = lax.broadcasted_iota(jnp.int32, (2 * CHUNK, 2 * CHUNK), 1)
    band = (col >= row) & (col <= row + CHUNK)
    no_prev = col >= CHUNK
    causal = (lax.broadcasted_iota(jnp.int32, (2 * CHUNK, CHUNK), 1)
              <= lax.broadcasted_iota(jnp.int32, (2 * CHUNK, CHUNK), 0) % CHUNK)

    def attend(s, cur, prev, mask):
        qb = q_ref[s, cur, :].astype(BF16)
        zero = jnp.zeros_like(qb)
        lhs = jnp.concatenate([jnp.where(first_head, qb, zero), jnp.where(first_head, zero, qb)], axis=0)
        kk = k_ref[s, cur, :]
        vv = v_ref[s, cur, :]
        if prev is not None:
            kk = jnp.concatenate([k_ref[s, prev, :], kk], axis=0)
            vv = jnp.concatenate([v_ref[s, prev, :], vv], axis=0)
        sc = jnp.where(mask, _dot_nt(lhs, kk.astype(BF16)), -jnp.inf)
        m = jnp.max(sc, axis=-1, keepdims=True)
        p = jnp.exp(sc - m)
        l = jnp.sum(p, axis=-1, keepdims=True)
        acc = _dot(p.astype(BF16), vv.astype(BF16))
        both = lambda a: jnp.where(first_head, a[:CHUNK], a[CHUNK:])
        return both(m), both(l), both(acc)

    def merged(s, cur, stats):
        m_new, l_new, acc_new = stats
        m_old = m_sc[s, cur, :]
        m_tot = jnp.maximum(m_old, m_new)
        a_old = jnp.exp(m_old - m_tot)
        a_new = jnp.exp(m_new - m_tot)
        return (m_tot, a_old * l_sc[s, cur, :] + a_new * l_new, a_old * o_ref[s, cur, :] + a_new * acc_new)

    def far_class(c, carry):
        cur = pl.ds(c, CHUNK, stride=d_far)
        for s in range(N_SLABS):
            m_sc[s, cur, :], l_sc[s, cur, :], o_ref[s, cur, :] = attend(s, cur, None, causal)
        return carry

    lax.fori_loop(0, d_far, far_class, 0)

    def span(jb, carry):
        for r in range(d_mid):
            cur = pl.ds(r + (d_mid * CHUNK) * jb, CHUNK, stride=d_mid)
            prev = pl.ds(r + (d_mid * CHUNK) * jnp.maximum(jb - 1, 0), CHUNK, stride=d_mid)
            mask = band & (no_prev | (jb > 0))
            for s in range(N_SLABS):
                m_sc[s, cur, :], l_sc[s, cur, :], o_ref[s, cur, :] = merged(s, cur, attend(s, cur, prev, mask))
        for i in range(d_mid):
            j = jb * d_mid + i
            cur = pl.ds(pl.multiple_of(j * CHUNK, CHUNK), CHUNK)
            prev = pl.ds(pl.multiple_of(jnp.maximum(j - 1, 0) * CHUNK, CHUNK), CHUNK)
            mask = band & (no_prev | (j > 0))
            for s in range(N_SLABS):
                _, l_tot, acc_tot = merged(s, cur, attend(s, cur, prev, mask))
                o_ref[s, cur, :] = acc_tot / l_tot
        return carry

    lax.fori_loop(0, mid_blocks, span, 0)


def _sample_attn_stages(q_ref, kn_ref, vn_ref, kt_ref, vt_ref, mask_ref, o_ref, s_sc, p_sc, *, seq, new_valid):
    n_rows = q_ref.shape[1]
    n_real = len(DILATED_PATTERNS) * seq
    row = lax.broadcasted_iota(jnp.int32, (n_rows, 1), 0)
    real = row < n_real
    valid = mask_ref[...] > 0.5
    new_rows = []
    for tp in range(seq):
        sel = functools.reduce(jnp.logical_or, [row == r for r in range(n_rows) if new_valid[tp][r]])
        new_rows.append(sel)

    heads = range(N_HEADS)
    for h in heads:
        sc = _dot(q_ref[h].astype(BF16), kt_ref[h].astype(BF16))
        s_sc[h] = jnp.where(valid, sc, -jnp.inf)
    yield

    weights = []
    for h in heads:
        q = q_ref[h]
        kn = kn_ref[h]
        s_new = [jnp.where(new_rows[tp], jnp.sum(q * kn[tp:tp + 1, :], axis=-1, keepdims=True), -jnp.inf)
                 for tp in range(seq)]
        sc = s_sc[h]
        m = jnp.max(sc, axis=-1, keepdims=True)
        for sn in s_new:
            m = jnp.maximum(m, sn)
        m = jnp.where(real, m, 0.0)
        p = jnp.exp(sc - m)
        p_sc[h] = p.astype(BF16)
        p_new = [jnp.exp(sn - m) for sn in s_new]
        l = jnp.sum(p, axis=-1, keepdims=True) + sum(p_new)
        m_eff = jnp.where(real, m, NEG_BIG)
        l_eff = jnp.where(real, l, 0.0)
        m_all = m_eff
        for i in range(1, n_rows // seq):
            m_all = jnp.maximum(m_all, pltpu.roll(m_eff, i * seq, axis=0))
        c = jnp.exp(m_eff - m_all)
        cl = c * l_eff
        den = cl
        for i in range(1, n_rows // seq):
            den = den + pltpu.roll(cl, i * seq, axis=0)
        w = jnp.where(real, c / den, 0.0)
        weights.append((w, [pn * w for pn in p_new]))
    yield

    for h in heads:
        w, pw_new = weights[h]
        vn = vn_ref[h]
        out = _dot_nt(p_sc[h], vt_ref[h].astype(BF16)) * w
        for tp in range(seq):
            out = out + pw_new[tp] * vn[tp:tp + 1, :]
        tot = out
        for i in range(1, n_rows // seq):
            tot = tot + pltpu.roll(out, i * seq, axis=0)
        o_ref[h] = tot


def _finish_stages(x_ref, mixa_ref, b_ref, g_out_b_ref, wo_ref, g_ffn_ref, wg_ref, wu_ref, wd_ref, o_ref):
    slabs = _group_norm_rows([b_ref[s] for s in range(N_SLABS)], g_out_b_ref)
    mix_b = jnp.concatenate([a.astype(BF16) for a in slabs], axis=-1)
    x1 = x_ref[...] + _dot(mixa_ref[...], wo_ref[:WIDTH, :]) + _dot(mix_b, wo_ref[WIDTH:, :])
    ms = jnp.mean(x1 * x1, axis=-1, keepdims=True)
    h = (x1 * lax.rsqrt(ms + EPS) * g_ffn_ref[...]).astype(BF16)
    act = (jax.nn.silu(_dot(h, wg_ref[...])) * _dot(h, wu_ref[...])).astype(BF16)
    yield
    o_ref[...] = x1 + _dot(act, wd_ref[...])


def _params(sem):
    return pltpu.CompilerParams(dimension_semantics=sem, vmem_limit_bytes=VMEM_LIMIT)


_DONE = object()


def _staged_call(calls, order, name):
    steps = calls[0]["steps"]
    assert all(c["steps"] == steps for c in calls)
    counts = [(len(c["args"]), len(c["out_specs"]), len(c["scratch"])) for c in calls]

    def body(*refs):
        groups = []
        pos = 0
        for kind in range(3):
            per_call = []
            for cnt in counts:
                per_call.append(refs[pos:pos + cnt[kind]])
                pos += cnt[kind]
            groups.append(per_call)
        gens = [c["stages"](*groups[0][i], *groups[1][i], *groups[2][i]) for i, c in enumerate(calls)]
        for i in order:
            next(gens[i], None)
        assert all(next(g, _DONE) is _DONE for g in gens), "order leaves stages untraced"

    cat = lambda key: [item for c in calls for item in c[key]]
    outs = pl.pallas_call(
        body, grid=(steps,), in_specs=cat("in_specs"), out_specs=cat("out_specs"), out_shape=cat("out_shapes"),
        scratch_shapes=cat("scratch"), compiler_params=_params(("parallel",)), name=name,
    )(*cat("args"))
    split, pos = [], 0
    for _, n_out, _ in counts:
        split.append(outs[pos:pos + n_out])
        pos += n_out
    return split


def _shared_in_proj_args(g_attn, w_in, ln_v_g, ln_v_b, g_q, g_k):
    d_model, in_width = w_in.shape
    gq = (jnp.tile(g_q, N_HEADS) * (HEAD_DIM ** -0.5))[None]
    gk = jnp.tile(g_k, N_HEADS)[None]
    bd = jnp.asarray(np.kron(np.eye(N_HEADS), np.full((HEAD_DIM, HEAD_DIM), 1.0 / HEAD_DIM)), BF16)
    args = (g_attn[None], w_in.astype(BF16), ln_v_g[None], ln_v_b[None], gq, gk, bd)
    specs = [_const_spec((1, d_model)), _const_spec((d_model, in_width)), _const_spec((1, WIDTH)),
             _const_spec((1, WIDTH)), _const_spec((1, WIDTH)), _const_spec((1, WIDTH)),
             _const_spec((WIDTH, WIDTH))]
    return args, specs


def _in_proj_prompt_call(x, shared, w_s, b_s, g_out_a, *, tm):
    batch, seq, d_model = x.shape
    n = batch * seq
    nt = seq // tm
    shared_args, shared_specs = shared
    tri = np.tril(np.ones((CHUNK, CHUNK), bool))
    ws = jnp.where(tri, w_s, 0.0).astype(BF16)
    bs = jnp.repeat(b_s.T, HEAD_DIM, axis=1)
    slab = jax.ShapeDtypeStruct((batch, N_SLABS, seq, LANES), F32)
    flat = jax.ShapeDtypeStruct((batch, WIDTH, seq), F32)
    slab_spec = pl.BlockSpec((None, N_SLABS, tm, LANES), lambda i: (i // nt, 0, i % nt, 0))
    flat_spec = pl.BlockSpec((None, WIDTH, tm), lambda i: (i // nt, 0, i % nt))
    row_spec = pl.BlockSpec((tm, WIDTH), lambda i: (i, 0))
    return dict(
        stages=_in_proj_prompt_stages,
        args=(x.reshape(n, d_model), *shared_args, ws, bs, g_out_a[None]),
        in_specs=[pl.BlockSpec((tm, d_model), lambda i: (i, 0))] + shared_specs + [
            _const_spec((N_HEADS, CHUNK, CHUNK)), _const_spec((CHUNK, WIDTH)), _const_spec((1, WIDTH))],
        out_specs=[slab_spec, slab_spec, slab_spec, flat_spec, flat_spec, row_spec],
        out_shapes=[slab, slab, slab, flat, flat, jax.ShapeDtypeStruct((n, WIDTH), BF16)],
        scratch=[], steps=n // tm)


def _in_proj_sample(x, shared, w_s, b_s, g_out_a):
    n_seq, seq, d_model = x.shape
    n = n_seq * seq
    shared_args, shared_specs = shared
    sub = 8
    assert sub % seq == 0 and seq <= CHUNK
    step = np.arange(sub) % seq
    taps = []
    for d in range(seq):
        src = np.maximum(step - d, 0)
        tap = jnp.where((step >= d)[:, None], w_s[:, step, src].T, 0.0)
        taps.append(jnp.repeat(tap, HEAD_DIM, axis=1))
    coef = jnp.stack(taps)
    bias = jnp.repeat(b_s[:, step].T, HEAD_DIM, axis=1)
    flat = jax.ShapeDtypeStruct((n, WIDTH), F32)
    row_spec = pl.BlockSpec((n, WIDTH), lambda i: (0, 0))
    return pl.pallas_call(
        functools.partial(_in_proj_sample_kernel, seq=seq),
        grid=(1,),
        in_specs=[pl.BlockSpec((n, d_model), lambda i: (0, 0))] + shared_specs + [
            _const_spec((seq, sub, WIDTH)), _const_spec((sub, WIDTH)), _const_spec((1, WIDTH))],
        out_specs=[row_spec] * 5,
        out_shape=[flat, flat, flat, flat, jax.ShapeDtypeStruct((n, WIDTH), BF16)],
        compiler_params=_params(("arbitrary",)),
        name="in_proj_sample",
    )(x.reshape(n, d_model), *shared_args, coef, bias, g_out_a[None])


def _prompt_attention(q, k, v):
    batch, _, seq, _ = q.shape
    n_blocks = seq // CHUNK
    assert all(n_blocks % dil == 0 for _, dil in DILATED_PATTERNS)
    spec = pl.BlockSpec((None, N_SLABS, seq, LANES), lambda b: (b, 0, 0, 0))
    return pl.pallas_call(
        functools.partial(_prompt_attn_kernel, n_blocks=n_blocks),
        grid=(batch,),
        in_specs=[spec, spec, spec],
        out_specs=spec,
        out_shape=jax.ShapeDtypeStruct(q.shape, F32),
        scratch_shapes=[pltpu.VMEM((N_SLABS, seq, LANES), F32), pltpu.VMEM((N_SLABS, seq, LANES), F32)],
        compiler_params=_params(("parallel",)),
        name="prompt_attention",
    )(q, k, v)


def _sample_masks(seq, cache_len, past_len, n_rows):
    cache = np.zeros((n_rows, cache_len), np.float32)
    new = [[False] * n_rows for _ in range(seq)]
    for b, (window, dil) in enumerate(DILATED_PATTERNS):
        for t in range(seq):
            for jj in range(window // dil + 1):
                idx = cache_len + t - dil * jj
                if idx < 0 or idx + (past_len - cache_len) < 0:
                    continue
                if idx < cache_len:
                    cache[b * seq + t, idx] = 1.0
                else:
                    new[idx - cache_len][b * seq + t] = True
    return cache, new


def _sample_attention_operands(q, k, v, cache_k, cache_v):
    n_seq, seq, _ = q.shape
    n_rows = 16
    n_br = len(DILATED_PATTERNS)
    assert n_br * seq <= n_rows and n_rows % seq == 0
    per_head = lambda a: a.reshape(n_seq, seq, N_HEADS, HEAD_DIM).transpose(0, 2, 1, 3)
    pad_rows = lambda a, rows: jnp.pad(a, ((0, 0), (0, 0), (0, rows - a.shape[2]), (0, 0)))
    q16 = pad_rows(jnp.tile(per_head(q), (1, 1, n_br, 1)), n_rows)
    return (q16, pad_rows(per_head(k), 8), pad_rows(per_head(v), 8),
            jnp.transpose(cache_k, (0, 2, 3, 1)), jnp.transpose(cache_v, (0, 2, 3, 1)))


def _sample_attention_call(operands, seq, past_len, first, count):
    q16, kn, vn, kt, vt = operands
    n_rows = q16.shape[2]
    cache_len = kt.shape[3]
    mask, new_valid = _sample_masks(seq, cache_len, past_len, n_rows)
    at_seq = lambda i: (first + i, 0, 0, 0)
    head_spec = lambda r: pl.BlockSpec((None, N_HEADS, r, HEAD_DIM), at_seq)
    cache_spec = pl.BlockSpec((None, N_HEADS, HEAD_DIM, cache_len), at_seq)
    return dict(
        stages=functools.partial(_sample_attn_stages, seq=seq, new_valid=new_valid),
        args=(q16, kn, vn, kt, vt, jnp.asarray(mask)),
        in_specs=[head_spec(n_rows), head_spec(8), head_spec(8), cache_spec, cache_spec,
                  _const_spec((n_rows, cache_len))],
        out_specs=[pl.BlockSpec((None, N_HEADS, n_rows, HEAD_DIM), lambda i: (i, 0, 0, 0))],
        out_shapes=[jax.ShapeDtypeStruct((count, N_HEADS, n_rows, HEAD_DIM), F32)],
        scratch=[pltpu.VMEM((N_HEADS, n_rows, cache_len), F32), pltpu.VMEM((N_HEADS, n_rows, cache_len), BF16)],
        steps=count)


def _sample_attention_slabs(out, seq):
    n_seq = out.shape[0]
    b_out = out[:, :, :seq, :].transpose(0, 2, 1, 3).reshape(n_seq * seq, N_SLABS, LANES)
    return b_out.transpose(1, 0, 2)[None]


def _finish_call(x2d, mix_a, b_slab, g_out_b, w_o, g_ffn, w_gate, w_up, w_down, *, tm):
    n, d_model = x2d.shape
    nt = b_slab.shape[2] // tm
    d_ff = w_gate.shape[1]
    return dict(
        stages=_finish_stages,
        args=(x2d, mix_a, b_slab, g_out_b[None], w_o.astype(BF16), g_ffn[None],
              w_gate.astype(BF16), w_up.astype(BF16), w_down.astype(BF16)),
        in_specs=[pl.BlockSpec((tm, d_model), lambda i: (i, 0)),
                  pl.BlockSpec((tm, WIDTH), lambda i: (i, 0)),
                  pl.BlockSpec((None, N_SLABS, tm, LANES), lambda i: (i // nt, 0, i % nt, 0)),
                  _const_spec((1, WIDTH)), _const_spec((2 * WIDTH, d_model)), _const_spec((1, d_model)),
                  _const_spec((d_model, d_ff)), _const_spec((d_model, d_ff)), _const_spec((d_ff, d_model))],
        out_specs=[pl.BlockSpec((tm, d_model), lambda i: (i, 0))],
        out_shapes=[jax.ShapeDtypeStruct((n, d_model), F32)],
        scratch=[], steps=n // tm)


def kernel(x_prompt, x_sample, cache_k, cache_v, g_attn, w_in, ln_v_g, ln_v_b, w_s, b_s, g_q, g_k,
           g_out_a, g_out_b, w_o, g_ffn, w_gate, w_up, w_down):
    depth = w_in.shape[0]
    assert depth == 1, "single-layer step"
    batch, seq, d_model = x_prompt.shape
    n_seq, dec_seq, _ = x_sample.shape
    l = 0
    shared = _shared_in_proj_args(g_attn[l], w_in[l], ln_v_g[l], ln_v_b[l], g_q[l], g_k[l])
    fin = (g_out_b[l], w_o[l], g_ffn[l], w_gate[l], w_up[l], w_down[l])
    tm = 256

    qs, ks, vs, vns, mix_as = _in_proj_sample(x_sample, shared, w_s[l], b_s[l], g_out_a[l])
    as3 = lambda a: a.reshape(n_seq, dec_seq, WIDTH)
    att_ops = _sample_attention_operands(as3(qs), as3(ks), as3(vs), cache_k[l], cache_v[l])
    steps = batch * seq // tm
    assert n_seq == 2 * steps
    interleave = [1, 0, 1, 1, 0]

    in_proj = _in_proj_prompt_call(x_prompt, shared, w_s[l], b_s[l], g_out_a[l], tm=tm)
    (q, k, v, k_out, v_out, mix_a), (att_a,) = _staged_call(
        [in_proj, _sample_attention_call(att_ops, dec_seq, PAST_LEN, 0, steps)], interleave,
        "in_proj_prompt_and_sample_attention")
    b_slab = _prompt_attention(q, k, v)
    finish = _finish_call(x_prompt.reshape(batch * seq, d_model), mix_a, b_slab, *fin, tm=tm)
    (y_prompt,), (att_b,) = _staged_call(
        [finish, _sample_attention_call(att_ops, dec_seq, PAST_LEN, steps, steps)], interleave,
        "finish_and_sample_attention")
    win = min(max(w for w, _ in DILATED_PATTERNS), seq)
    window = lambda a: a.reshape(batch, N_HEADS, HEAD_DIM, seq).transpose(0, 3, 1, 2)[:, -win:][None]

    bs_slab = _sample_attention_slabs(jnp.concatenate([att_a, att_b], axis=0), dec_seq)
    finish_s = _finish_call(x_sample.reshape(n_seq * dec_seq, d_model), mix_as, bs_slab, *fin, tm=tm)
    ((y_sample,),) = _staged_call([finish_s], [0, 0], "finish")

    head_shape = (1, n_seq, dec_seq, N_HEADS, HEAD_DIM)
    return (y_prompt.reshape(batch, seq, d_model), y_sample.reshape(n_seq, dec_seq, d_model),
            window(k_out), window(v_out), ks.reshape(head_shape), vs.reshape(head_shape),
            vns.reshape(1, n_seq, dec_seq, WIDTH))
```

```python
import functools

import numpy as np
import jax
import jax.numpy as jnp
from jax import lax
from jax.experimental import pallas as pl
from jax.experimental.pallas import tpu as pltpu

F32 = jnp.float32
BF16 = jnp.bfloat16

HEAD_DIM = 64
N_HEADS = 8
WIDTH = N_HEADS * HEAD_DIM
CHUNK = 128
DILATED_PATTERNS = ((128, 1), (512, 4), (2048, 16))
EPS = 1e-6
PAST_LEN = 8192
NEG_BIG = -1e30
LOG2_E = 1.4426950408889634

LANES = 128
N_SLABS = WIDTH // LANES
VMEM_LIMIT = 56 * 1024 * 1024


def _dot(a, b):
    return jnp.dot(a, b, preferred_element_type=F32)


def _dot_nt(a, b):
    return lax.dot_general(a, b, (((1,), (1,)), ((), ())), preferred_element_type=F32)


def _const_spec(shape):
    return pl.BlockSpec(shape, lambda *_: (0,) * len(shape), pipeline_mode=pl.Buffered(1))


def _head_rms(z, bd_ref, g_ref):
    sq = z * z
    hi = sq.astype(BF16)
    lo = (sq - hi.astype(F32)).astype(BF16)
    msq = _dot(hi, bd_ref[...]) + _dot(lo, bd_ref[...])
    return z * lax.rsqrt(msq + EPS) * g_ref[...]


def _features(x_ref, g_attn_ref, w_in_ref, ln_g_ref, ln_b_ref, gq_ref, gk_ref, bd_ref):
    x = x_ref[...]
    ms = jnp.mean(x * x, axis=-1, keepdims=True)
    xn = (x * lax.rsqrt(ms + EPS) * g_attn_ref[...]).astype(BF16)

    def proj(j):
        return _dot(xn, w_in_ref[:, j * WIDTH:(j + 1) * WIDTH])

    u = jax.nn.gelu(proj(0))
    gv = jax.nn.gelu(proj(1))
    mu = jnp.mean(gv, axis=-1, keepdims=True)
    xc = gv - mu
    var = jnp.mean(xc * xc, axis=-1, keepdims=True)
    vn = xc * lax.rsqrt(var + EPS) * ln_g_ref[...] + ln_b_ref[...]
    q = _head_rms(proj(2), bd_ref, gq_ref)
    k = _head_rms(proj(3), bd_ref, gk_ref)
    v = proj(4)
    return u, vn, q, k, v


def _group_norm_rows(slabs, g_ref):
    ssq = sum(jnp.sum(a * a, axis=-1, keepdims=True) for a in slabs)
    r = lax.rsqrt(ssq * (1.0 / WIDTH) + EPS)
    return [a * r * g_ref[:, s * LANES:(s + 1) * LANES] for s, a in enumerate(slabs)]


def _in_proj_prompt_stages(x_ref, g_attn_ref, w_in_ref, ln_g_ref, ln_b_ref, gq_ref, gk_ref, bd_ref,
                           ws_ref, bs_ref, g_out_a_ref,
                           q_ref, k_ref, v_ref, kout_ref, vout_ref, mixa_ref):
    u, vn, q, k, v = _features(x_ref, g_attn_ref, w_in_ref, ln_g_ref, ln_b_ref, gq_ref, gk_ref, bd_ref)
    yield
    tm = u.shape[0]
    first_head = lax.broadcasted_iota(jnp.int32, (CHUNK, LANES), 1) < HEAD_DIM
    for c in range(tm // CHUNK):
        rows = slice(c * CHUNK, (c + 1) * CHUNK)
        vn_c = vn[rows].astype(BF16)
        slabs = []
        for s in range(N_SLABS):
            cols = slice(s * LANES, (s + 1) * LANES)
            g0 = _dot(ws_ref[2 * s], vn_c[:, cols])
            g1 = _dot(ws_ref[2 * s + 1], vn_c[:, cols])
            gate = jnp.where(first_head, g0, g1) + bs_ref[:, cols]
            slabs.append(u[rows, cols] * gate)
        for s, a in enumerate(_group_norm_rows(slabs, g_out_a_ref)):
            mixa_ref[rows, s * LANES:(s + 1) * LANES] = a.astype(BF16)
    for s in range(N_SLABS):
        cols = slice(s * LANES, (s + 1) * LANES)
        q_ref[s] = q[:, cols]
        k_ref[s] = k[:, cols]
        v_ref[s] = v[:, cols]
    kout_ref[...] = k.T
    vout_ref[...] = v.T


def _in_proj_sample_kernel(x_ref, g_attn_ref, w_in_ref, ln_g_ref, ln_b_ref, gq_ref, gk_ref, bd_ref,
                           coef_ref, bias_ref, g_out_a_ref,
                           q_ref, k_ref, v_ref, vn_ref, mixa_ref, *, seq):
    u, vn, q, k, v = _features(x_ref, g_attn_ref, w_in_ref, ln_g_ref, ln_b_ref, gq_ref, gk_ref, bd_ref)
    tm = u.shape[0]
    sub = coef_ref.shape[1]
    gate = jnp.zeros((tm // sub, sub, WIDTH), F32) + bias_ref[...]
    for d in range(seq):
        shifted = vn if d == 0 else pltpu.roll(vn, d, axis=0)
        gate = gate + shifted.reshape(tm // sub, sub, WIDTH) * coef_ref[d]
    a = u * gate.reshape(tm, WIDTH)
    slabs = _group_norm_rows([a[:, s * LANES:(s + 1) * LANES] for s in range(N_SLABS)], g_out_a_ref)
    for s, a_s in enumerate(slabs):
        mixa_ref[:, s * LANES:(s + 1) * LANES] = a_s.astype(BF16)
    q_ref[...] = q
    k_ref[...] = k
    v_ref[...] = v
    vn_ref[...] = vn


def _prompt_attn_kernel(q_ref, k_ref, v_ref, o_ref, m_sc, l_sc, s_sc, p_sc, *, n_blocks):
    (w_near, d_near), (w_mid, d_mid), (w_far, d_far) = DILATED_PATTERNS
    assert d_near == 1 and d_far == n_blocks and n_blocks % d_mid == 0
    assert w_near // d_near == w_mid // d_mid == w_far // d_far == CHUNK
    mid_blocks = n_blocks // d_mid
    sub = CHUNK // 2

    first_head = lax.broadcasted_iota(jnp.int32, (CHUNK, LANES), 1) < HEAD_DIM
    ones = jnp.ones((2 * CHUNK, LANES), BF16)

    def valid(c, n_keys, has_prev):
        row = (lax.broadcasted_iota(jnp.int32, (sub, n_keys), 0) + c * sub) % CHUNK
        col = lax.broadcasted_iota(jnp.int32, (sub, n_keys), 1)
        if n_keys == CHUNK:
            return col <= row
        return (col >= row) & (col <= row + CHUNK) & ((col >= CHUNK) | has_prev)

    def attend(s, cur, prev, has_prev):
        qb = q_ref[s, cur, :].astype(BF16)
        zero = jnp.zeros_like(qb)
        lhs = jnp.concatenate([jnp.where(first_head, qb, zero), jnp.where(first_head, zero, qb)], axis=0)
        kk = k_ref[s, cur, :]
        vv = v_ref[s, cur, :]
        if prev is not None:
            kk = jnp.concatenate([k_ref[s, prev, :], kk], axis=0)
            vv = jnp.concatenate([v_ref[s, prev, :], vv], axis=0)
        n_keys = kk.shape[0]
        s_sc[:, :n_keys] = _dot_nt(lhs, kk.astype(BF16))
        m_parts = []
        for c in range(2 * CHUNK // sub):
            rows = slice(c * sub, (c + 1) * sub)
            sc = jnp.where(valid(c, n_keys, has_prev), s_sc[rows, :n_keys], -jnp.inf)
            m = jnp.max(sc, axis=-1, keepdims=True)
            p_sc[rows, :n_keys] = jnp.exp2(sc - m).astype(BF16)
            m_parts.append(jnp.broadcast_to(m, (sub, LANES)))
        v_ext = jnp.concatenate([vv.astype(BF16), ones[:n_keys]], axis=1)
        out = _dot(p_sc[:, :n_keys], v_ext)
        both = lambda a: jnp.where(first_head, a[:CHUNK], a[CHUNK:])
        return both(jnp.concatenate(m_parts, axis=0)), both(out[:, LANES:]), both(out[:, :LANES])

    def merged(s, cur, stats):
        m_new, l_new, acc_new = stats
        m_old = m_sc[s, cur, :]
        m_tot = jnp.maximum(m_old, m_new)
        a_old = jnp.exp2(m_old - m_tot)
        a_new = jnp.exp2(m_new - m_tot)
        return (m_tot, a_old * l_sc[s, cur, :] + a_new * l_new, a_old * o_ref[s, cur, :] + a_new * acc_new)

    def far_class(c, carry):
        cur = pl.ds(c, CHUNK, stride=d_far)
        for s in range(N_SLABS):
            m_sc[s, cur, :], l_sc[s, cur, :], o_ref[s, cur, :] = attend(s, cur, None, True)
        return carry

    lax.fori_loop(0, d_far, far_class, 0)

    def span(jb, carry):
        for r in range(d_mid):
            cur = pl.ds(r + (d_mid * CHUNK) * jb, CHUNK, stride=d_mid)
            prev = pl.ds(r + (d_mid * CHUNK) * jnp.maximum(jb - 1, 0), CHUNK, stride=d_mid)
            for s in range(N_SLABS):
                m_sc[s, cur, :], l_sc[s, cur, :], o_ref[s, cur, :] = merged(s, cur, attend(s, cur, prev, jb > 0))
        for i in range(d_mid):
            j = jb * d_mid + i
            cur = pl.ds(pl.multiple_of(j * CHUNK, CHUNK), CHUNK)
            prev = pl.ds(pl.multiple_of(jnp.maximum(j - 1, 0) * CHUNK, CHUNK), CHUNK)
            for s in range(N_SLABS):
                _, l_tot, acc_tot = merged(s, cur, attend(s, cur, prev, j > 0))
                o_ref[s, cur, :] = acc_tot / l_tot
        return carry

    lax.fori_loop(0, mid_blocks, span, 0)


def _sample_attn_stages(q_ref, kn_ref, vn_ref, kt_ref, vt_ref, mask_ref, o_ref, s_sc, p_sc, *, seq, new_valid):
    n_rows = q_ref.shape[1]
    n_real = len(DILATED_PATTERNS) * seq
    row = lax.broadcasted_iota(jnp.int32, (n_rows, 1), 0)
    real = row < n_real
    valid = mask_ref[...] > 0.5
    new_rows = []
    for tp in range(seq):
        sel = functools.reduce(jnp.logical_or, [row == r for r in range(n_rows) if new_valid[tp][r]])
        new_rows.append(sel)

    heads = range(N_HEADS)
    for h in heads:
        sc = _dot(q_ref[h].astype(BF16), kt_ref[h].astype(BF16))
        s_sc[h] = jnp.where(valid, sc, -jnp.inf)
    yield

    weights = []
    for h in heads:
        q = q_ref[h]
        kn = kn_ref[h]
        s_new = [jnp.where(new_rows[tp], jnp.sum(q * kn[tp:tp + 1, :], axis=-1, keepdims=True), -jnp.inf)
                 for tp in range(seq)]
        sc = s_sc[h]
        m = jnp.max(sc, axis=-1, keepdims=True)
        for sn in s_new:
            m = jnp.maximum(m, sn)
        m = jnp.where(real, m, 0.0)
        p = jnp.exp2(sc - m)
        p_sc[h] = p.astype(BF16)
        p_new = [jnp.exp2(sn - m) for sn in s_new]
        l = jnp.sum(p, axis=-1, keepdims=True) + sum(p_new)
        m_eff = jnp.where(real, m, NEG_BIG)
        l_eff = jnp.where(real, l, 0.0)
        m_all = m_eff
        for i in range(1, n_rows // seq):
            m_all = jnp.maximum(m_all, pltpu.roll(m_eff, i * seq, axis=0))
        c = jnp.exp2(m_eff - m_all)
        cl = c * l_eff
        den = cl
        for i in range(1, n_rows // seq):
            den = den + pltpu.roll(cl, i * seq, axis=0)
        w = jnp.where(real, c / den, 0.0)
        weights.append((w, [pn * w for pn in p_new]))
    yield

    for h in heads:
        w, pw_new = weights[h]
        vn = vn_ref[h]
        out = _dot_nt(p_sc[h], vt_ref[h].astype(BF16)) * w
        for tp in range(seq):
            out = out + pw_new[tp] * vn[tp:tp + 1, :]
        tot = out
        for i in range(1, n_rows // seq):
            tot = tot + pltpu.roll(out, i * seq, axis=0)
        o_ref[h] = tot


def _finish_stages(x_ref, mixa_ref, b_ref, g_out_b_ref, wo_ref, g_ffn_ref, wg_ref, wu_ref, wd_ref, o_ref):
    slabs = _group_norm_rows([b_ref[s] for s in range(N_SLABS)], g_out_b_ref)
    mix_b = jnp.concatenate([a.astype(BF16) for a in slabs], axis=-1)
    x1 = x_ref[...] + _dot(mixa_ref[...], wo_ref[:WIDTH, :]) + _dot(mix_b, wo_ref[WIDTH:, :])
    ms = jnp.mean(x1 * x1, axis=-1, keepdims=True)
    h = (x1 * lax.rsqrt(ms + EPS) * g_ffn_ref[...]).astype(BF16)
    act = (jax.nn.silu(_dot(h, wg_ref[...])) * _dot(h, wu_ref[...])).astype(BF16)
    yield
    o_ref[...] = x1 + _dot(act, wd_ref[...])


def _params(sem):
    return pltpu.CompilerParams(dimension_semantics=sem, vmem_limit_bytes=VMEM_LIMIT)


_DONE = object()


def _staged_call(calls, order, name):
    steps = calls[0]["steps"]
    assert all(c["steps"] == steps for c in calls)
    counts = [(len(c["args"]), len(c["out_specs"]), len(c["scratch"])) for c in calls]

    def body(*refs):
        groups = []
        pos = 0
        for kind in range(3):
            per_call = []
            for cnt in counts:
                per_call.append(refs[pos:pos + cnt[kind]])
                pos += cnt[kind]
            groups.append(per_call)
        gens = [c["stages"](*groups[0][i], *groups[1][i], *groups[2][i]) for i, c in enumerate(calls)]
        for i in order:
            next(gens[i], None)
        assert all(next(g, _DONE) is _DONE for g in gens), "order leaves stages untraced"

    cat = lambda key: [item for c in calls for item in c[key]]
    outs = pl.pallas_call(
        body, grid=(steps,), in_specs=cat("in_specs"), out_specs=cat("out_specs"), out_shape=cat("out_shapes"),
        scratch_shapes=cat("scratch"), compiler_params=_params(("parallel",)), name=name,
    )(*cat("args"))
    split, pos = [], 0
    for _, n_out, _ in counts:
        split.append(outs[pos:pos + n_out])
        pos += n_out
    return split


def _shared_in_proj_args(g_attn, w_in, ln_v_g, ln_v_b, g_q, g_k):
    d_model, in_width = w_in.shape
    gq = (jnp.tile(g_q, N_HEADS) * (HEAD_DIM ** -0.5 * LOG2_E))[None]
    gk = jnp.tile(g_k, N_HEADS)[None]
    bd = jnp.asarray(np.kron(np.eye(N_HEADS), np.full((HEAD_DIM, HEAD_DIM), 1.0 / HEAD_DIM)), BF16)
    args = (g_attn[None], w_in.astype(BF16), ln_v_g[None], ln_v_b[None], gq, gk, bd)
    specs = [_const_spec((1, d_model)), _const_spec((d_model, in_width)), _const_spec((1, WIDTH)),
             _const_spec((1, WIDTH)), _const_spec((1, WIDTH)), _const_spec((1, WIDTH)),
             _const_spec((WIDTH, WIDTH))]
    return args, specs


def _in_proj_prompt_call(x, shared, w_s, b_s, g_out_a, *, tm):
    batch, seq, d_model = x.shape
    n = batch * seq
    nt = seq // tm
    shared_args, shared_specs = shared
    tri = np.tril(np.ones((CHUNK, CHUNK), bool))
    ws = jnp.where(tri, w_s, 0.0).astype(BF16)
    bs = jnp.repeat(b_s.T, HEAD_DIM, axis=1)
    slab = jax.ShapeDtypeStruct((batch, N_SLABS, seq, LANES), F32)
    flat = jax.ShapeDtypeStruct((batch, WIDTH, seq), F32)
    slab_spec = pl.BlockSpec((None, N_SLABS, tm, LANES), lambda i: (i // nt, 0, i % nt, 0))
    flat_spec = pl.BlockSpec((None, WIDTH, tm), lambda i: (i // nt, 0, i % nt))
    row_spec = pl.BlockSpec((tm, WIDTH), lambda i: (i, 0))
    return dict(
        stages=_in_proj_prompt_stages,
        args=(x.reshape(n, d_model), *shared_args, ws, bs, g_out_a[None]),
        in_specs=[pl.BlockSpec((tm, d_model), lambda i: (i, 0))] + shared_specs + [
            _const_spec((N_HEADS, CHUNK, CHUNK)), _const_spec((CHUNK, WIDTH)), _const_spec((1, WIDTH))],
        out_specs=[slab_spec, slab_spec, slab_spec, flat_spec, flat_spec, row_spec],
        out_shapes=[slab, slab, slab, flat, flat, jax.ShapeDtypeStruct((n, WIDTH), BF16)],
        scratch=[], steps=n // tm)


def _in_proj_sample(x, shared, w_s, b_s, g_out_a):
    n_seq, seq, d_model = x.shape
    n = n_seq * seq
    shared_args, shared_specs = shared
    sub = 8
    assert sub % seq == 0 and seq <= CHUNK
    step = np.arange(sub) % seq
    taps = []
    for d in range(seq):
        src = np.maximum(step - d, 0)
        tap = jnp.where((step >= d)[:, None], w_s[:, step, src].T, 0.0)
        taps.append(jnp.repeat(tap, HEAD_DIM, axis=1))
    coef = jnp.stack(taps)
    bias = jnp.repeat(b_s[:, step].T, HEAD_DIM, axis=1)
    flat = jax.ShapeDtypeStruct((n, WIDTH), F32)
    row_spec = pl.BlockSpec((n, WIDTH), lambda i: (0, 0))
    return pl.pallas_call(
        functools.partial(_in_proj_sample_kernel, seq=seq),
        grid=(1,),
        in_specs=[pl.BlockSpec((n, d_model), lambda i: (0, 0))] + shared_specs + [
            _const_spec((seq, sub, WIDTH)), _const_spec((sub, WIDTH)), _const_spec((1, WIDTH))],
        out_specs=[row_spec] * 5,
        out_shape=[flat, flat, flat, flat, jax.ShapeDtypeStruct((n, WIDTH), BF16)],
        compiler_params=_params(("arbitrary",)),
        name="in_proj_sample",
    )(x.reshape(n, d_model), *shared_args, coef, bias, g_out_a[None])


def _prompt_attention(q, k, v):
    batch, _, seq, _ = q.shape
    n_blocks = seq // CHUNK
    assert all(n_blocks % dil == 0 for _, dil in DILATED_PATTERNS)
    spec = pl.BlockSpec((None, N_SLABS, seq, LANES), lambda b: (b, 0, 0, 0))
    return pl.pallas_call(
        functools.partial(_prompt_attn_kernel, n_blocks=n_blocks),
        grid=(batch,),
        in_specs=[spec, spec, spec],
        out_specs=spec,
        out_shape=jax.ShapeDtypeStruct(q.shape, F32),
        scratch_shapes=[pltpu.VMEM((N_SLABS, seq, LANES), F32), pltpu.VMEM((N_SLABS, seq, LANES), F32),
                        pltpu.VMEM((2 * CHUNK, 2 * CHUNK), F32), pltpu.VMEM((2 * CHUNK, 2 * CHUNK), BF16)],
        compiler_params=_params(("parallel",)),
        name="prompt_attention",
    )(q, k, v)


def _sample_masks(seq, cache_len, past_len, n_rows):
    cache = np.zeros((n_rows, cache_len), np.float32)
    new = [[False] * n_rows for _ in range(seq)]
    for b, (window, dil) in enumerate(DILATED_PATTERNS):
        for t in range(seq):
            for jj in range(window // dil + 1):
                idx = cache_len + t - dil * jj
                if idx < 0 or idx + (past_len - cache_len) < 0:
                    continue
                if idx < cache_len:
                    cache[b * seq + t, idx] = 1.0
                else:
                    new[idx - cache_len][b * seq + t] = True
    return cache, new


def _sample_attention_operands(q, k, v, cache_k, cache_v):
    n_seq, seq, _ = q.shape
    n_rows = 16
    n_br = len(DILATED_PATTERNS)
    assert n_br * seq <= n_rows and n_rows % seq == 0
    per_head = lambda a: a.reshape(n_seq, seq, N_HEADS, HEAD_DIM).transpose(0, 2, 1, 3)
    pad_rows = lambda a, rows: jnp.pad(a, ((0, 0), (0, 0), (0, rows - a.shape[2]), (0, 0)))
    q16 = pad_rows(jnp.tile(per_head(q), (1, 1, n_br, 1)), n_rows)
    return (q16, pad_rows(per_head(k), 8), pad_rows(per_head(v), 8),
            jnp.transpose(cache_k, (0, 2, 3, 1)), jnp.transpose(cache_v, (0, 2, 3, 1)))


def _sample_attention_call(operands, seq, past_len, first, count):
    q16, kn, vn, kt, vt = operands
    n_rows = q16.shape[2]
    cache_len = kt.shape[3]
    mask, new_valid = _sample_masks(seq, cache_len, past_len, n_rows)
    at_seq = lambda i: (first + i, 0, 0, 0)
    head_spec = lambda r: pl.BlockSpec((None, N_HEADS, r, HEAD_DIM), at_seq)
    cache_spec = pl.BlockSpec((None, N_HEADS, HEAD_DIM, cache_len), at_seq)
    return dict(
        stages=functools.partial(_sample_attn_stages, seq=seq, new_valid=new_valid),
        args=(q16, kn, vn, kt, vt, jnp.asarray(mask)),
        in_specs=[head_spec(n_rows), head_spec(8), head_spec(8), cache_spec, cache_spec,
                  _const_spec((n_rows, cache_len))],
        out_specs=[pl.BlockSpec((None, N_HEADS, n_rows, HEAD_DIM), lambda i: (i, 0, 0, 0))],
        out_shapes=[jax.ShapeDtypeStruct((count, N_HEADS, n_rows, HEAD_DIM), F32)],
        scratch=[pltpu.VMEM((N_HEADS, n_rows, cache_len), F32), pltpu.VMEM((N_HEADS, n_rows, cache_len), BF16)],
        steps=count)


def _sample_attention_slabs(out, seq):
    n_seq = out.shape[0]
    b_out = out[:, :, :seq, :].transpose(0, 2, 1, 3).reshape(n_seq * seq, N_SLABS, LANES)
    return b_out.transpose(1, 0, 2)[None]


def _finish_call(x2d, mix_a, b_slab, g_out_b, w_o, g_ffn, w_gate, w_up, w_down, *, tm):
    n, d_model = x2d.shape
    nt = b_slab.shape[2] // tm
    d_ff = w_gate.shape[1]
    return dict(
        stages=_finish_stages,
        args=(x2d, mix_a, b_slab, g_out_b[None], w_o.astype(BF16), g_ffn[None],
              w_gate.astype(BF16), w_up.astype(BF16), w_down.astype(BF16)),
        in_specs=[pl.BlockSpec((tm, d_model), lambda i: (i, 0)),
                  pl.BlockSpec((tm, WIDTH), lambda i: (i, 0)),
                  pl.BlockSpec((None, N_SLABS, tm, LANES), lambda i: (i // nt, 0, i % nt, 0)),
                  _const_spec((1, WIDTH)), _const_spec((2 * WIDTH, d_model)), _const_spec((1, d_model)),
                  _const_spec((d_model, d_ff)), _const_spec((d_model, d_ff)), _const_spec((d_ff, d_model))],
        out_specs=[pl.BlockSpec((tm, d_model), lambda i: (i, 0))],
        out_shapes=[jax.ShapeDtypeStruct((n, d_model), F32)],
        scratch=[], steps=n // tm)


def kernel(x_prompt, x_sample, cache_k, cache_v, g_attn, w_in, ln_v_g, ln_v_b, w_s, b_s, g_q, g_k,
           g_out_a, g_out_b, w_o, g_ffn, w_gate, w_up, w_down):
    depth = w_in.shape[0]
    assert depth == 1, "single-layer step"
    batch, seq, d_model = x_prompt.shape
    n_seq, dec_seq, _ = x_sample.shape
    l = 0
    shared = _shared_in_proj_args(g_attn[l], w_in[l], ln_v_g[l], ln_v_b[l], g_q[l], g_k[l])
    fin = (g_out_b[l], w_o[l], g_ffn[l], w_gate[l], w_up[l], w_down[l])
    tm = 256

    qs, ks, vs, vns, mix_as = _in_proj_sample(x_sample, shared, w_s[l], b_s[l], g_out_a[l])
    as3 = lambda a: a.reshape(n_seq, dec_seq, WIDTH)
    att_ops = _sample_attention_operands(as3(qs), as3(ks), as3(vs), cache_k[l], cache_v[l])
    steps = batch * seq // tm
    assert n_seq == 2 * steps
    interleave = [1, 0, 1, 1, 0]

    in_proj = _in_proj_prompt_call(x_prompt, shared, w_s[l], b_s[l], g_out_a[l], tm=tm)
    (q, k, v, k_out, v_out, mix_a), (att_a,) = _staged_call(
        [in_proj, _sample_attention_call(att_ops, dec_seq, PAST_LEN, 0, steps)], interleave,
        "in_proj_prompt_and_sample_attention")
    b_slab = _prompt_attention(q, k, v)
    finish = _finish_call(x_prompt.reshape(batch * seq, d_model), mix_a, b_slab, *fin, tm=tm)
    (y_prompt,), (att_b,) = _staged_call(
        [finish, _sample_attention_call(att_ops, dec_seq, PAST_LEN, steps, steps)], interleave,
        "finish_and_sample_attention")
    win = min(max(w for w, _ in DILATED_PATTERNS), seq)
    window = lambda a: a.reshape(batch, N_HEADS, HEAD_DIM, seq).transpose(0, 3, 1, 2)[:, -win:][None]

    bs_slab = _sample_attention_slabs(jnp.concatenate([att_a, att_b], axis=0), dec_seq)
    finish_s = _finish_call(x_sample.reshape(n_seq * dec_seq, d_model), mix_as, bs_slab, *fin, tm=tm)
    ((y_sample,),) = _staged_call([finish_s], [0, 0], "finish")

    head_shape = (1, n_seq, dec_seq, N_HEADS, HEAD_DIM)
    return (y_prompt.reshape(batch, seq, d_model), y_sample.reshape(n_seq, dec_seq, d_model),
            window(k_out), window(v_out), ks.reshape(head_shape), vs.reshape(head_shape),
            vns.reshape(1, n_seq, dec_seq, WIDTH))
```

```python
import functools

import numpy as np
import jax
import jax.numpy as jnp
from jax import lax
from jax.experimental import pallas as pl
from jax.experimental.pallas import tpu as pltpu

F32 = jnp.float32
BF16 = jnp.bfloat16

HEAD_DIM = 64
N_HEADS = 8
WIDTH = N_HEADS * HEAD_DIM
CHUNK = 128
DILATED_PATTERNS = ((128, 1), (512, 4), (2048, 16))
EPS = 1e-6
PAST_LEN = 8192
NEG_BIG = -1e30
LOG2_E = 1.4426950408889634

LANES = 128
N_SLABS = WIDTH // LANES
VMEM_LIMIT = 56 * 1024 * 1024


def _dot(a, b):
    return jnp.dot(a, b, preferred_element_type=F32)


def _dot_nt(a, b):
    return lax.dot_general(a, b, (((1,), (1,)), ((), ())), preferred_element_type=F32)


def _const_spec(shape):
    return pl.BlockSpec(shape, lambda *_: (0,) * len(shape), pipeline_mode=pl.Buffered(1))


def _head_rms(z, bd_ref, g_ref):
    sq = z * z
    hi = sq.astype(BF16)
    lo = (sq - hi.astype(F32)).astype(BF16)
    msq = _dot(hi, bd_ref[...]) + _dot(lo, bd_ref[...])
    return z * lax.rsqrt(msq + EPS) * g_ref[...]


def _features(x_ref, g_attn_ref, w_in_ref, ln_g_ref, ln_b_ref, gq_ref, gk_ref, bd_ref):
    x = x_ref[...]
    ms = jnp.mean(x * x, axis=-1, keepdims=True)
    xn = (x * lax.rsqrt(ms + EPS) * g_attn_ref[...]).astype(BF16)

    def proj(j):
        return _dot(xn, w_in_ref[:, j * WIDTH:(j + 1) * WIDTH])

    u = jax.nn.gelu(proj(0))
    gv = jax.nn.gelu(proj(1))
    mu = jnp.mean(gv, axis=-1, keepdims=True)
    xc = gv - mu
    var = jnp.mean(xc * xc, axis=-1, keepdims=True)
    vn = xc * lax.rsqrt(var + EPS) * ln_g_ref[...] + ln_b_ref[...]
    q = _head_rms(proj(2), bd_ref, gq_ref)
    k = _head_rms(proj(3), bd_ref, gk_ref)
    v = proj(4)
    return u, vn, q, k, v


def _group_norm_rows(slabs, g_ref):
    ssq = sum(jnp.sum(a * a, axis=-1, keepdims=True) for a in slabs)
    r = lax.rsqrt(ssq * (1.0 / WIDTH) + EPS)
    return [a * r * g_ref[:, s * LANES:(s + 1) * LANES] for s, a in enumerate(slabs)]


def _in_proj_prompt_stages(x_ref, g_attn_ref, w_in_ref, ln_g_ref, ln_b_ref, gq_ref, gk_ref, bd_ref,
                           ws_ref, bs_ref, g_out_a_ref,
                           q_ref, k_ref, v_ref, kout_ref, vout_ref, mixa_ref):
    u, vn, q, k, v = _features(x_ref, g_attn_ref, w_in_ref, ln_g_ref, ln_b_ref, gq_ref, gk_ref, bd_ref)
    yield
    tm = u.shape[0]
    first_head = lax.broadcasted_iota(jnp.int32, (CHUNK, LANES), 1) < HEAD_DIM
    for c in range(tm // CHUNK):
        rows = slice(c * CHUNK, (c + 1) * CHUNK)
        vn_c = vn[rows].astype(BF16)
        slabs = []
        for s in range(N_SLABS):
            cols = slice(s * LANES, (s + 1) * LANES)
            g0 = _dot(ws_ref[2 * s], vn_c[:, cols])
            g1 = _dot(ws_ref[2 * s + 1], vn_c[:, cols])
            gate = jnp.where(first_head, g0, g1) + bs_ref[:, cols]
            slabs.append(u[rows, cols] * gate)
        for s, a in enumerate(_group_norm_rows(slabs, g_out_a_ref)):
            mixa_ref[rows, s * LANES:(s + 1) * LANES] = a.astype(BF16)
    for s in range(N_SLABS):
        cols = slice(s * LANES, (s + 1) * LANES)
        q_ref[s] = q[:, cols]
        k_ref[s] = k[:, cols]
        v_ref[s] = v[:, cols]
    kout_ref[...] = k.T
    vout_ref[...] = v.T


def _in_proj_sample_kernel(x_ref, g_attn_ref, w_in_ref, ln_g_ref, ln_b_ref, gq_ref, gk_ref, bd_ref,
                           coef_ref, bias_ref, g_out_a_ref,
                           q_ref, k_ref, v_ref, vn_ref, mixa_ref, *, seq):
    u, vn, q, k, v = _features(x_ref, g_attn_ref, w_in_ref, ln_g_ref, ln_b_ref, gq_ref, gk_ref, bd_ref)
    tm = u.shape[0]
    sub = coef_ref.shape[1]
    gate = jnp.zeros((tm // sub, sub, WIDTH), F32) + bias_ref[...]
    for d in range(seq):
        shifted = vn if d == 0 else pltpu.roll(vn, d, axis=0)
        gate = gate + shifted.reshape(tm // sub, sub, WIDTH) * coef_ref[d]
    a = u * gate.reshape(tm, WIDTH)
    slabs = _group_norm_rows([a[:, s * LANES:(s + 1) * LANES] for s in range(N_SLABS)], g_out_a_ref)
    for s, a_s in enumerate(slabs):
        mixa_ref[:, s * LANES:(s + 1) * LANES] = a_s.astype(BF16)
    q_ref[...] = q
    k_ref[...] = k
    v_ref[...] = v
    vn_ref[...] = vn


def _prompt_attn_kernel(q_ref, k_ref, v_ref, o_ref, m_sc, l_sc, s_sc, p_sc, *, n_blocks):
    (w_near, d_near), (w_mid, d_mid), (w_far, d_far) = DILATED_PATTERNS
    assert d_near == 1 and d_far == n_blocks and n_blocks % d_mid == 0
    assert w_near // d_near == w_mid // d_mid == w_far // d_far == CHUNK
    mid_blocks = n_blocks // d_mid
    sub = CHUNK // 2

    first_head = lax.broadcasted_iota(jnp.int32, (CHUNK, LANES), 1) < HEAD_DIM
    ones = jnp.ones((2 * CHUNK, LANES), BF16)

    def valid(c, n_keys, has_prev):
        row = (lax.broadcasted_iota(jnp.int32, (sub, n_keys), 0) + c * sub) % CHUNK
        col = lax.broadcasted_iota(jnp.int32, (sub, n_keys), 1)
        if n_keys == CHUNK:
            return col <= row
        return (col >= row) & (col <= row + CHUNK) & ((col >= CHUNK) | has_prev)

    def attend(s, cur, prev, has_prev):
        qb = q_ref[s, cur, :].astype(BF16)
        zero = jnp.zeros_like(qb)
        lhs = jnp.concatenate([jnp.where(first_head, qb, zero), jnp.where(first_head, zero, qb)], axis=0)
        kk = k_ref[s, cur, :]
        vv = v_ref[s, cur, :]
        if prev is not None:
            kk = jnp.concatenate([k_ref[s, prev, :], kk], axis=0)
            vv = jnp.concatenate([v_ref[s, prev, :], vv], axis=0)
        n_keys = kk.shape[0]
        s_sc[:, :n_keys] = _dot_nt(lhs, kk.astype(BF16))
        m_parts = []
        for c in range(2 * CHUNK // sub):
            rows = slice(c * sub, (c + 1) * sub)
            sc = jnp.where(valid(c, n_keys, has_prev), s_sc[rows, :n_keys], -jnp.inf)
            m = jnp.max(sc, axis=-1, keepdims=True)
            p_sc[rows, :n_keys] = jnp.exp2(sc - m).astype(BF16)
            m_parts.append(jnp.broadcast_to(m, (sub, LANES)))
        v_ext = jnp.concatenate([vv.astype(BF16), ones[:n_keys]], axis=1)
        out = _dot(p_sc[:, :n_keys], v_ext)
        both = lambda a: jnp.where(first_head, a[:CHUNK], a[CHUNK:])
        return both(jnp.concatenate(m_parts, axis=0)), both(out[:, LANES:]), both(out[:, :LANES])

    def merged(s, cur, stats):
        m_new, l_new, acc_new = stats
        m_old = m_sc[s, cur, :]
        m_tot = jnp.maximum(m_old, m_new)
        a_old = jnp.exp2(m_old - m_tot)
        a_new = jnp.exp2(m_new - m_tot)
        return (m_tot, a_old * l_sc[s, cur, :] + a_new * l_new, a_old * o_ref[s, cur, :] + a_new * acc_new)

    def far_class(c, carry):
        cur = pl.ds(c, CHUNK, stride=d_far)
        for s in range(N_SLABS):
            m_sc[s, cur, :], l_sc[s, cur, :], o_ref[s, cur, :] = attend(s, cur, None, True)
        return carry

    lax.fori_loop(0, d_far, far_class, 0)

    def span(jb, carry):
        for r in range(d_mid):
            cur = pl.ds(r + (d_mid * CHUNK) * jb, CHUNK, stride=d_mid)
            prev = pl.ds(r + (d_mid * CHUNK) * jnp.maximum(jb - 1, 0), CHUNK, stride=d_mid)
            for s in range(N_SLABS):
                m_sc[s, cur, :], l_sc[s, cur, :], o_ref[s, cur, :] = merged(s, cur, attend(s, cur, prev, jb > 0))
        for i in range(d_mid):
            j = jb * d_mid + i
            cur = pl.ds(pl.multiple_of(j * CHUNK, CHUNK), CHUNK)
            prev = pl.ds(pl.multiple_of(jnp.maximum(j - 1, 0) * CHUNK, CHUNK), CHUNK)
            for s in range(N_SLABS):
                _, l_tot, acc_tot = merged(s, cur, attend(s, cur, prev, j > 0))
                o_ref[s, cur, :] = acc_tot / l_tot
        return carry

    lax.fori_loop(0, mid_blocks, span, 0)


def _sample_attn_stages(q_ref, kn_ref, vn_ref, kt_ref, vt_ref, mask_ref, o_ref, s_sc, p_sc, *, seq, new_valid):
    n_rows = mask_ref.shape[0]
    n_real = len(DILATED_PATTERNS) * seq
    assert q_ref.shape[1] == 2 * seq and n_rows == 4 * seq

    def branch_rows(a):
        return jnp.concatenate([a + pltpu.roll(a, seq, axis=0), a], axis=0)

    row = lax.broadcasted_iota(jnp.int32, (n_rows, 1), 0)
    real = row < n_real
    valid = mask_ref[...] > 0.5
    new_rows = []
    for tp in range(seq):
        sel = functools.reduce(jnp.logical_or, [row == r for r in range(n_rows) if new_valid[tp][r]])
        new_rows.append(sel)

    heads = range(q_ref.shape[0])
    for h in heads:
        sc = _dot(q_ref[h].astype(BF16), kt_ref[h].astype(BF16))
        s_sc[h] = jnp.where(valid, branch_rows(sc), -jnp.inf)
    yield

    weights = []
    for h in heads:
        q = branch_rows(q_ref[h])
        kn = kn_ref[h]
        s_new = [jnp.where(new_rows[tp], jnp.sum(q * kn[tp:tp + 1, :], axis=-1, keepdims=True), -jnp.inf)
                 for tp in range(seq)]
        sc = s_sc[h]
        m = jnp.max(sc, axis=-1, keepdims=True)
        for sn in s_new:
            m = jnp.maximum(m, sn)
        m = jnp.where(real, m, 0.0)
        p = jnp.exp2(sc - m)
        p_sc[h] = p.astype(BF16)
        p_new = [jnp.exp2(sn - m) for sn in s_new]
        l = jnp.sum(p, axis=-1, keepdims=True) + sum(p_new)
        m_eff = jnp.where(real, m, NEG_BIG)
        l_eff = jnp.where(real, l, 0.0)
        m_all = m_eff
        for i in range(1, n_rows // seq):
            m_all = jnp.maximum(m_all, pltpu.roll(m_eff, i * seq, axis=0))
        c = jnp.exp2(m_eff - m_all)
        cl = c * l_eff
        den = cl
        for i in range(1, n_rows // seq):
            den = den + pltpu.roll(cl, i * seq, axis=0)
        w = jnp.where(real, c / den, 0.0)
        weights.append((w, [pn * w for pn in p_new]))
    yield

    for h in heads:
        w, pw_new = weights[h]
        vn = vn_ref[h]
        out = _dot_nt(p_sc[h], vt_ref[h].astype(BF16)) * w
        for tp in range(seq):
            out = out + pw_new[tp] * vn[tp:tp + 1, :]
        tot = out
        for i in range(1, n_rows // seq):
            tot = tot + pltpu.roll(out, i * seq, axis=0)
        o_ref[h] = tot


def _finish_stages(x_ref, mixa_ref, b_ref, g_out_b_ref, wo_ref, g_ffn_ref, wg_ref, wu_ref, wd_ref, o_ref):
    slabs = _group_norm_rows([b_ref[s] for s in range(N_SLABS)], g_out_b_ref)
    mix_b = jnp.concatenate([a.astype(BF16) for a in slabs], axis=-1)
    x1 = x_ref[...] + _dot(mixa_ref[...], wo_ref[:WIDTH, :]) + _dot(mix_b, wo_ref[WIDTH:, :])
    ms = jnp.mean(x1 * x1, axis=-1, keepdims=True)
    h = (x1 * lax.rsqrt(ms + EPS) * g_ffn_ref[...]).astype(BF16)
    act = (jax.nn.silu(_dot(h, wg_ref[...])) * _dot(h, wu_ref[...])).astype(BF16)
    yield
    o_ref[...] = x1 + _dot(act, wd_ref[...])


def _params(sem):
    return pltpu.CompilerParams(dimension_semantics=sem, vmem_limit_bytes=VMEM_LIMIT)


_DONE = object()


def _staged_call(calls, order, name):
    steps = calls[0]["steps"]
    assert all(c["steps"] == steps for c in calls)
    counts = [(len(c["args"]), len(c["out_specs"]), len(c["scratch"])) for c in calls]

    def body(*refs):
        groups = []
        pos = 0
        for kind in range(3):
            per_call = []
            for cnt in counts:
                per_call.append(refs[pos:pos + cnt[kind]])
                pos += cnt[kind]
            groups.append(per_call)
        gens = [c["stages"](*groups[0][i], *groups[1][i], *groups[2][i]) for i, c in enumerate(calls)]
        for i in order:
            next(gens[i], None)
        assert all(next(g, _DONE) is _DONE for g in gens), "order leaves stages untraced"

    cat = lambda key: [item for c in calls for item in c[key]]
    outs = pl.pallas_call(
        body, grid=(steps,), in_specs=cat("in_specs"), out_specs=cat("out_specs"), out_shape=cat("out_shapes"),
        scratch_shapes=cat("scratch"), compiler_params=_params(("parallel",)), name=name,
    )(*cat("args"))
    split, pos = [], 0
    for _, n_out, _ in counts:
        split.append(outs[pos:pos + n_out])
        pos += n_out
    return split


def _shared_in_proj_args(g_attn, w_in, ln_v_g, ln_v_b, g_q, g_k):
    d_model, in_width = w_in.shape
    gq = (jnp.tile(g_q, N_HEADS) * (HEAD_DIM ** -0.5 * LOG2_E))[None]
    gk = jnp.tile(g_k, N_HEADS)[None]
    bd = jnp.asarray(np.kron(np.eye(N_HEADS), np.full((HEAD_DIM, HEAD_DIM), 1.0 / HEAD_DIM)), BF16)
    args = (g_attn[None], w_in.astype(BF16), ln_v_g[None], ln_v_b[None], gq, gk, bd)
    specs = [_const_spec((1, d_model)), _const_spec((d_model, in_width)), _const_spec((1, WIDTH)),
             _const_spec((1, WIDTH)), _const_spec((1, WIDTH)), _const_spec((1, WIDTH)),
             _const_spec((WIDTH, WIDTH))]
    return args, specs


def _in_proj_prompt_call(x, shared, w_s, b_s, g_out_a, *, tm):
    batch, seq, d_model = x.shape
    n = batch * seq
    nt = seq // tm
    shared_args, shared_specs = shared
    tri = np.tril(np.ones((CHUNK, CHUNK), bool))
    ws = jnp.where(tri, w_s, 0.0).astype(BF16)
    bs = jnp.repeat(b_s.T, HEAD_DIM, axis=1)
    slab = jax.ShapeDtypeStruct((batch, N_SLABS, seq, LANES), F32)
    flat = jax.ShapeDtypeStruct((batch, WIDTH, seq), F32)
    slab_spec = pl.BlockSpec((None, N_SLABS, tm, LANES), lambda i: (i // nt, 0, i % nt, 0))
    flat_spec = pl.BlockSpec((None, WIDTH, tm), lambda i: (i // nt, 0, i % nt))
    row_spec = pl.BlockSpec((tm, WIDTH), lambda i: (i, 0))
    return dict(
        stages=_in_proj_prompt_stages,
        args=(x.reshape(n, d_model), *shared_args, ws, bs, g_out_a[None]),
        in_specs=[pl.BlockSpec((tm, d_model), lambda i: (i, 0))] + shared_specs + [
            _const_spec((N_HEADS, CHUNK, CHUNK)), _const_spec((CHUNK, WIDTH)), _const_spec((1, WIDTH))],
        out_specs=[slab_spec, slab_spec, slab_spec, flat_spec, flat_spec, row_spec],
        out_shapes=[slab, slab, slab, flat, flat, jax.ShapeDtypeStruct((n, WIDTH), BF16)],
        scratch=[], steps=n // tm)


def _in_proj_sample(x, shared, w_s, b_s, g_out_a):
    n_seq, seq, d_model = x.shape
    n = n_seq * seq
    shared_args, shared_specs = shared
    sub = 8
    assert sub % seq == 0 and seq <= CHUNK
    step = np.arange(sub) % seq
    corner = w_s[:, :seq, :seq]
    taps = []
    for d in range(seq):
        src = np.maximum(step - d, 0)
        tap = jnp.where((step >= d)[:, None], corner[:, step, src].T, 0.0)
        taps.append(jnp.repeat(tap, HEAD_DIM, axis=1))
    coef = jnp.stack(taps)
    bias = jnp.repeat(b_s[:, step].T, HEAD_DIM, axis=1)
    flat = jax.ShapeDtypeStruct((n, WIDTH), F32)
    row_spec = pl.BlockSpec((n, WIDTH), lambda i: (0, 0))
    return pl.pallas_call(
        functools.partial(_in_proj_sample_kernel, seq=seq),
        grid=(1,),
        in_specs=[pl.BlockSpec((n, d_model), lambda i: (0, 0))] + shared_specs + [
            _const_spec((seq, sub, WIDTH)), _const_spec((sub, WIDTH)), _const_spec((1, WIDTH))],
        out_specs=[row_spec] * 5,
        out_shape=[flat, flat, flat, flat, jax.ShapeDtypeStruct((n, WIDTH), BF16)],
        compiler_params=_params(("arbitrary",)),
        name="in_proj_sample",
    )(x.reshape(n, d_model), *shared_args, coef, bias, g_out_a[None])


def _prompt_attention(q, k, v):
    batch, _, seq, _ = q.shape
    n_blocks = seq // CHUNK
    assert all(n_blocks % dil == 0 for _, dil in DILATED_PATTERNS)
    spec = pl.BlockSpec((None, N_SLABS, seq, LANES), lambda b: (b, 0, 0, 0))
    return pl.pallas_call(
        functools.partial(_prompt_attn_kernel, n_blocks=n_blocks),
        grid=(batch,),
        in_specs=[spec, spec, spec],
        out_specs=spec,
        out_shape=jax.ShapeDtypeStruct(q.shape, F32),
        scratch_shapes=[pltpu.VMEM((N_SLABS, seq, LANES), F32), pltpu.VMEM((N_SLABS, seq, LANES), F32),
                        pltpu.VMEM((2 * CHUNK, 2 * CHUNK), F32), pltpu.VMEM((2 * CHUNK, 2 * CHUNK), BF16)],
        compiler_params=_params(("parallel",)),
        name="prompt_attention",
    )(q, k, v)


def _sample_masks(seq, cache_len, past_len, n_rows):
    cache = np.zeros((n_rows, cache_len), np.float32)
    new = [[False] * n_rows for _ in range(seq)]
    for b, (window, dil) in enumerate(DILATED_PATTERNS):
        for t in range(seq):
            for jj in range(window // dil + 1):
                idx = cache_len + t - dil * jj
                if idx < 0 or idx + (past_len - cache_len) < 0:
                    continue
                if idx < cache_len:
                    cache[b * seq + t, idx] = 1.0
                else:
                    new[idx - cache_len][b * seq + t] = True
    return cache, new


UNIT_HEADS = 4
UNITS_PER_SEQ = N_HEADS // UNIT_HEADS


def _sample_attention_operands(q, k, v, cache_k, cache_v):
    n_seq, seq, _ = q.shape
    per_head = lambda a: jnp.pad(a.reshape(n_seq, seq, N_HEADS, HEAD_DIM).transpose(0, 2, 1, 3),
                                 ((0, 0), (0, 0), (0, 2 * seq - seq), (0, 0)))
    return (per_head(q), per_head(k), per_head(v),
            jnp.transpose(cache_k, (0, 2, 3, 1)), jnp.transpose(cache_v, (0, 2, 3, 1)))


def _sample_attention_call(operands, seq, past_len, first, count):
    q8, kn, vn, kt, vt = operands
    n_rows = 4 * seq
    assert len(DILATED_PATTERNS) * seq <= n_rows
    cache_len = kt.shape[3]
    mask, new_valid = _sample_masks(seq, cache_len, past_len, n_rows)
    at_unit = lambda i: ((first + i) // UNITS_PER_SEQ, (first + i) % UNITS_PER_SEQ, 0, 0)
    new_spec = pl.BlockSpec((None, UNIT_HEADS, 2 * seq, HEAD_DIM), at_unit)
    cache_spec = pl.BlockSpec((None, UNIT_HEADS, HEAD_DIM, cache_len), at_unit)
    return dict(
        stages=functools.partial(_sample_attn_stages, seq=seq, new_valid=new_valid),
        args=(q8, kn, vn, kt, vt, jnp.asarray(mask)),
        in_specs=[new_spec, new_spec, new_spec, cache_spec, cache_spec, _const_spec((n_rows, cache_len))],
        out_specs=[pl.BlockSpec((None, UNIT_HEADS, n_rows, HEAD_DIM), lambda i: (i, 0, 0, 0))],
        out_shapes=[jax.ShapeDtypeStruct((count, UNIT_HEADS, n_rows, HEAD_DIM), F32)],
        scratch=[pltpu.VMEM((UNIT_HEADS, n_rows, cache_len), F32), pltpu.VMEM((UNIT_HEADS, n_rows, cache_len), BF16)],
        steps=count)


def _sample_attention_slabs(unit_outs, seq):
    out = jnp.concatenate(unit_outs, axis=0)
    out = out.reshape(-1, N_HEADS, out.shape[2], HEAD_DIM)
    n_seq = out.shape[0]
    b_out = out[:, :, :seq, :].transpose(0, 2, 1, 3).reshape(n_seq * seq, N_SLABS, LANES)
    return b_out.transpose(1, 0, 2)[None]


def _finish_call(x2d, mix_a, b_slab, g_out_b, w_o, g_ffn, w_gate, w_up, w_down, *, tm):
    n, d_model = x2d.shape
    nt = b_slab.shape[2] // tm
    d_ff = w_gate.shape[1]
    return dict(
        stages=_finish_stages,
        args=(x2d, mix_a, b_slab, g_out_b[None], w_o.astype(BF16), g_ffn[None],
              w_gate.astype(BF16), w_up.astype(BF16), w_down.astype(BF16)),
        in_specs=[pl.BlockSpec((tm, d_model), lambda i: (i, 0)),
                  pl.BlockSpec((tm, WIDTH), lambda i: (i, 0)),
                  pl.BlockSpec((None, N_SLABS, tm, LANES), lambda i: (i // nt, 0, i % nt, 0)),
                  _const_spec((1, WIDTH)), _const_spec((2 * WIDTH, d_model)), _const_spec((1, d_model)),
                  _const_spec((d_model, d_ff)), _const_spec((d_model, d_ff)), _const_spec((d_ff, d_model))],
        out_specs=[pl.BlockSpec((tm, d_model), lambda i: (i, 0))],
        out_shapes=[jax.ShapeDtypeStruct((n, d_model), F32)],
        scratch=[], steps=n // tm)


def kernel(x_prompt, x_sample, cache_k, cache_v, g_attn, w_in, ln_v_g, ln_v_b, w_s, b_s, g_q, g_k,
           g_out_a, g_out_b, w_o, g_ffn, w_gate, w_up, w_down):
    depth = w_in.shape[0]
    assert depth == 1, "single-layer step"
    batch, seq, d_model = x_prompt.shape
    n_seq, dec_seq, _ = x_sample.shape
    l = 0
    shared = _shared_in_proj_args(g_attn[l], w_in[l], ln_v_g[l], ln_v_b[l], g_q[l], g_k[l])
    fin = (g_out_b[l], w_o[l], g_ffn[l], w_gate[l], w_up[l], w_down[l])
    tm = 256

    qs, ks, vs, vns, mix_as = _in_proj_sample(x_sample, shared, w_s[l], b_s[l], g_out_a[l])
    as3 = lambda a: a.reshape(n_seq, dec_seq, WIDTH)
    att_ops = _sample_attention_operands(as3(qs), as3(ks), as3(vs), cache_k[l], cache_v[l])
    steps = batch * seq // tm
    streams = n_seq * UNITS_PER_SEQ // steps
    assert streams * steps == n_seq * UNITS_PER_SEQ and streams >= 2
    unit_stream = lambda k: _sample_attention_call(att_ops, dec_seq, PAST_LEN, k * steps, steps)

    def hosted(host, hosted_streams, name):
        att = list(range(1, len(hosted_streams) + 1))
        outs = _staged_call([host] + hosted_streams, att + [0] + att + att + [0], name)
        return outs[0], [o[0] for o in outs[1:]]

    in_proj = _in_proj_prompt_call(x_prompt, shared, w_s[l], b_s[l], g_out_a[l], tm=tm)
    (q, k, v, k_out, v_out, mix_a), att_a = hosted(in_proj, [unit_stream(0)], "in_proj_prompt_and_sample_attention")
    b_slab = _prompt_attention(q, k, v)
    finish = _finish_call(x_prompt.reshape(batch * seq, d_model), mix_a, b_slab, *fin, tm=tm)
    (y_prompt,), att_b = hosted(finish, [unit_stream(k) for k in range(1, streams)], "finish_and_sample_attention")
    win = min(max(w for w, _ in DILATED_PATTERNS), seq)
    window = lambda a: a.reshape(batch, N_HEADS, HEAD_DIM, seq).transpose(0, 3, 1, 2)[:, -win:][None]

    bs_slab = _sample_attention_slabs(att_a + att_b, dec_seq)
    finish_s = _finish_call(x_sample.reshape(n_seq * dec_seq, d_model), mix_as, bs_slab, *fin, tm=tm)
    ((y_sample,),) = _staged_call([finish_s], [0, 0], "finish")

    head_shape = (1, n_seq, dec_seq, N_HEADS, HEAD_DIM)
    return (y_prompt.reshape(batch, seq, d_model), y_sample.reshape(n_seq, dec_seq, d_model),
            window(k_out), window(v_out), ks.reshape(head_shape), vs.reshape(head_shape),
            vns.reshape(1, n_seq, dec_seq, WIDTH))
```

```python
import functools

import numpy as np
import jax
import jax.numpy as jnp
from jax import lax
from jax.experimental import pallas as pl
from jax.experimental.pallas import tpu as pltpu

F32 = jnp.float32
BF16 = jnp.bfloat16

HEAD_DIM = 64
N_HEADS = 8
WIDTH = N_HEADS * HEAD_DIM
CHUNK = 128
DILATED_PATTERNS = ((128, 1), (512, 4), (2048, 16))
EPS = 1e-6
PAST_LEN = 8192
NEG_BIG = -1e30
LOG2_E = 1.4426950408889634

LANES = 128
N_SLABS = WIDTH // LANES
VMEM_LIMIT = 56 * 1024 * 1024


def _dot(a, b):
    return jnp.dot(a, b, preferred_element_type=F32)


def _dot_nt(a, b):
    return lax.dot_general(a, b, (((1,), (1,)), ((), ())), preferred_element_type=F32)


def _const_spec(shape):
    return pl.BlockSpec(shape, lambda *_: (0,) * len(shape), pipeline_mode=pl.Buffered(1))


def _head_rms(z, bd_ref, g_ref):
    msq = _dot((z * z).astype(BF16), bd_ref[...])
    return z * lax.rsqrt(msq + EPS) * g_ref[...]


def _features(x_ref, g_attn_ref, w_in_ref, ln_g_ref, ln_b_ref, gq_ref, gk_ref, bd_ref):
    x = x_ref[...]
    ms = jnp.mean(x * x, axis=-1, keepdims=True)
    xn = (x * lax.rsqrt(ms + EPS) * g_attn_ref[...]).astype(BF16)

    def proj(j):
        return _dot(xn, w_in_ref[:, j * WIDTH:(j + 1) * WIDTH])

    u = jax.nn.gelu(proj(0))
    gv = jax.nn.gelu(proj(1))
    mu = jnp.mean(gv, axis=-1, keepdims=True)
    xc = gv - mu
    var = jnp.mean(xc * xc, axis=-1, keepdims=True)
    vn = xc * lax.rsqrt(var + EPS) * ln_g_ref[...] + ln_b_ref[...]
    q = _head_rms(proj(2), bd_ref, gq_ref)
    k = _head_rms(proj(3), bd_ref, gk_ref)
    v = proj(4)
    return u, vn, q, k, v


def _group_norm_rows(slabs, g_ref):
    ssq = sum(jnp.sum(a * a, axis=-1, keepdims=True) for a in slabs)
    r = lax.rsqrt(ssq * (1.0 / WIDTH) + EPS)
    return [a * r * g_ref[:, s * LANES:(s + 1) * LANES] for s, a in enumerate(slabs)]


def _in_proj_prompt_stages(x_ref, g_attn_ref, w_in_ref, ln_g_ref, ln_b_ref, gq_ref, gk_ref, bd_ref,
                           ws_ref, bs_ref, g_out_a_ref,
                           q_ref, k_ref, v_ref, kout_ref, vout_ref, mixa_ref):
    u, vn, q, k, v = _features(x_ref, g_attn_ref, w_in_ref, ln_g_ref, ln_b_ref, gq_ref, gk_ref, bd_ref)
    yield
    tm = u.shape[0]
    first_head = lax.broadcasted_iota(jnp.int32, (CHUNK, LANES), 1) < HEAD_DIM
    for c in range(tm // CHUNK):
        rows = slice(c * CHUNK, (c + 1) * CHUNK)
        vn_c = vn[rows].astype(BF16)
        slabs = []
        for s in range(N_SLABS):
            cols = slice(s * LANES, (s + 1) * LANES)
            g0 = _dot(ws_ref[2 * s], vn_c[:, cols])
            g1 = _dot(ws_ref[2 * s + 1], vn_c[:, cols])
            gate = jnp.where(first_head, g0, g1) + bs_ref[:, cols]
            slabs.append(u[rows, cols] * gate)
        for s, a in enumerate(_group_norm_rows(slabs, g_out_a_ref)):
            mixa_ref[rows, s * LANES:(s + 1) * LANES] = a.astype(BF16)
    for s in range(N_SLABS):
        cols = slice(s * LANES, (s + 1) * LANES)
        q_ref[s] = q[:, cols]
        k_ref[s] = k[:, cols]
        v_ref[s] = v[:, cols]
    kout_ref[...] = k.T
    vout_ref[...] = v.T


def _in_proj_sample_kernel(x_ref, g_attn_ref, w_in_ref, ln_g_ref, ln_b_ref, gq_ref, gk_ref, bd_ref,
                           coef_ref, bias_ref, g_out_a_ref,
                           q_ref, k_ref, v_ref, vn_ref, mixa_ref, *, seq):
    u, vn, q, k, v = _features(x_ref, g_attn_ref, w_in_ref, ln_g_ref, ln_b_ref, gq_ref, gk_ref, bd_ref)
    tm = u.shape[0]
    sub = coef_ref.shape[1]
    gate = jnp.zeros((tm // sub, sub, WIDTH), F32) + bias_ref[...]
    for d in range(seq):
        shifted = vn if d == 0 else pltpu.roll(vn, d, axis=0)
        gate = gate + shifted.reshape(tm // sub, sub, WIDTH) * coef_ref[d]
    a = u * gate.reshape(tm, WIDTH)
    slabs = _group_norm_rows([a[:, s * LANES:(s + 1) * LANES] for s in range(N_SLABS)], g_out_a_ref)
    for s, a_s in enumerate(slabs):
        mixa_ref[:, s * LANES:(s + 1) * LANES] = a_s.astype(BF16)
    q_ref[...] = q
    k_ref[...] = k
    v_ref[...] = v
    vn_ref[...] = vn


def _prompt_attn_kernel(q_ref, k_ref, v_ref, band_ref, causal_ref, o_ref, m_sc, l_sc, s_sc, p_sc, *, n_blocks):
    (w_near, d_near), (w_mid, d_mid), (w_far, d_far) = DILATED_PATTERNS
    assert d_near == 1 and d_far == n_blocks and n_blocks % d_mid == 0
    assert w_near // d_near == w_mid // d_mid == w_far // d_far == CHUNK
    mid_blocks = n_blocks // d_mid
    sub = CHUNK // 2

    first_head = lax.broadcasted_iota(jnp.int32, (CHUNK, LANES), 1) < HEAD_DIM
    ones = jnp.ones((2 * CHUNK, LANES), BF16)

    def attend(s, cur, prev, has_prev):
        qb = q_ref[s, cur, :].astype(BF16)
        zero = jnp.zeros_like(qb)
        lhs = jnp.concatenate([jnp.where(first_head, qb, zero), jnp.where(first_head, zero, qb)], axis=0)
        kk = k_ref[s, cur, :]
        vv = v_ref[s, cur, :]
        if prev is not None:
            kk = jnp.concatenate([k_ref[s, prev, :], kk], axis=0)
            vv = jnp.concatenate([v_ref[s, prev, :], vv], axis=0)
        n_keys = kk.shape[0]
        s_sc[:, :n_keys] = _dot_nt(lhs, kk.astype(BF16))
        m_parts = []
        for c in range(2 * CHUNK // sub):
            rows = slice(c * sub, (c + 1) * sub)
            bias = causal_ref[rows, :] if prev is None else band_ref[has_prev, rows, :]
            sc = s_sc[rows, :n_keys] + bias
            m = jnp.max(sc, axis=-1, keepdims=True)
            p_sc[rows, :n_keys] = jnp.exp2(sc - m).astype(BF16)
            m_parts.append(jnp.broadcast_to(m, (sub, LANES)))
        v_ext = jnp.concatenate([vv.astype(BF16), ones[:n_keys]], axis=1)
        out = _dot(p_sc[:, :n_keys], v_ext)
        both = lambda a: jnp.where(first_head, a[:CHUNK], a[CHUNK:])
        return both(jnp.concatenate(m_parts, axis=0)), both(out[:, LANES:]), both(out[:, :LANES])

    def merged(s, cur, stats):
        m_new, l_new, acc_new = stats
        m_old = m_sc[s, cur, :]
        m_tot = jnp.maximum(m_old, m_new)
        a_old = jnp.exp2(m_old - m_tot)
        a_new = jnp.exp2(m_new - m_tot)
        return (m_tot, a_old * l_sc[s, cur, :] + a_new * l_new, a_old * o_ref[s, cur, :] + a_new * acc_new)

    def far_class(c, carry):
        cur = pl.ds(c, CHUNK, stride=d_far)
        for s in range(N_SLABS):
            m_sc[s, cur, :], l_sc[s, cur, :], o_ref[s, cur, :] = attend(s, cur, None, None)
        return carry

    lax.fori_loop(0, d_far, far_class, 0)

    def span(jb, carry):
        for r in range(d_mid):
            cur = pl.ds(r + (d_mid * CHUNK) * jb, CHUNK, stride=d_mid)
            prev = pl.ds(r + (d_mid * CHUNK) * jnp.maximum(jb - 1, 0), CHUNK, stride=d_mid)
            for s in range(N_SLABS):
                m_sc[s, cur, :], l_sc[s, cur, :], o_ref[s, cur, :] = merged(s, cur, attend(s, cur, prev, jnp.minimum(jb, 1)))
        for i in range(d_mid):
            j = jb * d_mid + i
            cur = pl.ds(pl.multiple_of(j * CHUNK, CHUNK), CHUNK)
            prev = pl.ds(pl.multiple_of(jnp.maximum(j - 1, 0) * CHUNK, CHUNK), CHUNK)
            for s in range(N_SLABS):
                _, l_tot, acc_tot = merged(s, cur, attend(s, cur, prev, jnp.minimum(j, 1)))
                o_ref[s, cur, :] = acc_tot / l_tot
        return carry

    lax.fori_loop(0, mid_blocks, span, 0)


def _sample_attn_stages(q_ref, kn_ref, vn_ref, kt_ref, vt_ref, mask_ref, o_ref, s_sc, p_sc, *, seq, new_valid):
    n_rows = mask_ref.shape[0]
    n_real = len(DILATED_PATTERNS) * seq
    assert q_ref.shape[1] == 2 * seq and n_rows == 4 * seq

    def branch_rows(a):
        return jnp.concatenate([a + pltpu.roll(a, seq, axis=0), a], axis=0)

    row = lax.broadcasted_iota(jnp.int32, (n_rows, 1), 0)
    real = row < n_real
    valid = mask_ref[...] > 0.5
    new_rows = []
    for tp in range(seq):
        sel = functools.reduce(jnp.logical_or, [row == r for r in range(n_rows) if new_valid[tp][r]])
        new_rows.append(sel)

    heads = range(q_ref.shape[0])
    for h in heads:
        sc = _dot(q_ref[h].astype(BF16), kt_ref[h].astype(BF16))
        s_sc[h] = jnp.where(valid, branch_rows(sc), -jnp.inf)
    yield

    weights = []
    for h in heads:
        q = branch_rows(q_ref[h])
        kn = kn_ref[h]
        s_new = [jnp.where(new_rows[tp], jnp.sum(q * kn[tp:tp + 1, :], axis=-1, keepdims=True), -jnp.inf)
                 for tp in range(seq)]
        sc = s_sc[h]
        m = jnp.max(sc, axis=-1, keepdims=True)
        for sn in s_new:
            m = jnp.maximum(m, sn)
        m = jnp.where(real, m, 0.0)
        p = jnp.exp2(sc - m)
        p_sc[h] = p.astype(BF16)
        p_new = [jnp.exp2(sn - m) for sn in s_new]
        l = jnp.sum(p, axis=-1, keepdims=True) + sum(p_new)
        m_eff = jnp.where(real, m, NEG_BIG)
        l_eff = jnp.where(real, l, 0.0)
        m_all = m_eff
        for i in range(1, n_rows // seq):
            m_all = jnp.maximum(m_all, pltpu.roll(m_eff, i * seq, axis=0))
        c = jnp.exp2(m_eff - m_all)
        cl = c * l_eff
        den = cl
        for i in range(1, n_rows // seq):
            den = den + pltpu.roll(cl, i * seq, axis=0)
        w = jnp.where(real, c / den, 0.0)
        weights.append((w, [pn * w for pn in p_new]))
    yield

    for h in heads:
        w, pw_new = weights[h]
        vn = vn_ref[h]
        out = _dot_nt(p_sc[h], vt_ref[h].astype(BF16)) * w
        for tp in range(seq):
            out = out + pw_new[tp] * vn[tp:tp + 1, :]
        tot = out
        for i in range(1, n_rows // seq):
            tot = tot + pltpu.roll(out, i * seq, axis=0)
        o_ref[h] = tot


def _finish_stages(x_ref, mixa_ref, b_ref, g_out_b_ref, wo_ref, g_ffn_ref, wg_ref, wu_ref, wd_ref, o_ref):
    slabs = _group_norm_rows([b_ref[s] for s in range(N_SLABS)], g_out_b_ref)
    mix_b = jnp.concatenate([a.astype(BF16) for a in slabs], axis=-1)
    x1 = x_ref[...] + _dot(mixa_ref[...], wo_ref[:WIDTH, :]) + _dot(mix_b, wo_ref[WIDTH:, :])
    ms = jnp.mean(x1 * x1, axis=-1, keepdims=True)
    h = (x1 * lax.rsqrt(ms + EPS) * g_ffn_ref[...]).astype(BF16)
    act = (jax.nn.silu(_dot(h, wg_ref[...])) * _dot(h, wu_ref[...])).astype(BF16)
    yield
    o_ref[...] = x1 + _dot(act, wd_ref[...])


def _params(sem):
    return pltpu.CompilerParams(dimension_semantics=sem, vmem_limit_bytes=VMEM_LIMIT)


_DONE = object()


def _staged_call(calls, order, name):
    steps = calls[0]["steps"]
    assert all(c["steps"] == steps for c in calls)
    counts = [(len(c["args"]), len(c["out_specs"]), len(c["scratch"])) for c in calls]

    def body(*refs):
        groups = []
        pos = 0
        for kind in range(3):
            per_call = []
            for cnt in counts:
                per_call.append(refs[pos:pos + cnt[kind]])
                pos += cnt[kind]
            groups.append(per_call)
        gens = [c["stages"](*groups[0][i], *groups[1][i], *groups[2][i]) for i, c in enumerate(calls)]
        for i in order:
            next(gens[i], None)
        assert all(next(g, _DONE) is _DONE for g in gens), "order leaves stages untraced"

    cat = lambda key: [item for c in calls for item in c[key]]
    outs = pl.pallas_call(
        body, grid=(steps,), in_specs=cat("in_specs"), out_specs=cat("out_specs"), out_shape=cat("out_shapes"),
        scratch_shapes=cat("scratch"), compiler_params=_params(("parallel",)), name=name,
    )(*cat("args"))
    split, pos = [], 0
    for _, n_out, _ in counts:
        split.append(outs[pos:pos + n_out])
        pos += n_out
    return split


def _shared_in_proj_args(g_attn, w_in, ln_v_g, ln_v_b, g_q, g_k):
    d_model, in_width = w_in.shape
    gq = (jnp.tile(g_q, N_HEADS) * (HEAD_DIM ** -0.5 * LOG2_E))[None]
    gk = jnp.tile(g_k, N_HEADS)[None]
    bd = jnp.asarray(np.kron(np.eye(N_HEADS), np.full((HEAD_DIM, HEAD_DIM), 1.0 / HEAD_DIM)), BF16)
    args = (g_attn[None], w_in.astype(BF16), ln_v_g[None], ln_v_b[None], gq, gk, bd)
    specs = [_const_spec((1, d_model)), _const_spec((d_model, in_width)), _const_spec((1, WIDTH)),
             _const_spec((1, WIDTH)), _const_spec((1, WIDTH)), _const_spec((1, WIDTH)),
             _const_spec((WIDTH, WIDTH))]
    return args, specs


def _in_proj_prompt_call(x, shared, w_s, b_s, g_out_a, *, tm):
    batch, seq, d_model = x.shape
    n = batch * seq
    nt = seq // tm
    shared_args, shared_specs = shared
    tri = np.tril(np.ones((CHUNK, CHUNK), bool))
    ws = jnp.where(tri, w_s, 0.0).astype(BF16)
    bs = jnp.repeat(b_s.T, HEAD_DIM, axis=1)
    slab = jax.ShapeDtypeStruct((batch, N_SLABS, seq, LANES), F32)
    flat = jax.ShapeDtypeStruct((batch, WIDTH, seq), F32)
    slab_spec = pl.BlockSpec((None, N_SLABS, tm, LANES), lambda i: (i // nt, 0, i % nt, 0))
    flat_spec = pl.BlockSpec((None, WIDTH, tm), lambda i: (i // nt, 0, i % nt))
    row_spec = pl.BlockSpec((tm, WIDTH), lambda i: (i, 0))
    return dict(
        stages=_in_proj_prompt_stages,
        args=(x.reshape(n, d_model), *shared_args, ws, bs, g_out_a[None]),
        in_specs=[pl.BlockSpec((tm, d_model), lambda i: (i, 0))] + shared_specs + [
            _const_spec((N_HEADS, CHUNK, CHUNK)), _const_spec((CHUNK, WIDTH)), _const_spec((1, WIDTH))],
        out_specs=[slab_spec, slab_spec, slab_spec, flat_spec, flat_spec, row_spec],
        out_shapes=[slab, slab, slab, flat, flat, jax.ShapeDtypeStruct((n, WIDTH), BF16)],
        scratch=[], steps=n // tm)


def _in_proj_sample(x, shared, w_s, b_s, g_out_a):
    n_seq, seq, d_model = x.shape
    n = n_seq * seq
    shared_args, shared_specs = shared
    sub = 8
    assert sub % seq == 0 and seq <= CHUNK
    step = np.arange(sub) % seq
    corner = w_s[:, :seq, :seq]
    taps = []
    for d in range(seq):
        src = np.maximum(step - d, 0)
        tap = jnp.where((step >= d)[:, None], corner[:, step, src].T, 0.0)
        taps.append(jnp.repeat(tap, HEAD_DIM, axis=1))
    coef = jnp.stack(taps)
    bias = jnp.repeat(b_s[:, step].T, HEAD_DIM, axis=1)
    flat = jax.ShapeDtypeStruct((n, WIDTH), F32)
    row_spec = pl.BlockSpec((n, WIDTH), lambda i: (0, 0))
    return pl.pallas_call(
        functools.partial(_in_proj_sample_kernel, seq=seq),
        grid=(1,),
        in_specs=[pl.BlockSpec((n, d_model), lambda i: (0, 0))] + shared_specs + [
            _const_spec((seq, sub, WIDTH)), _const_spec((sub, WIDTH)), _const_spec((1, WIDTH))],
        out_specs=[row_spec] * 5,
        out_shape=[flat, flat, flat, flat, jax.ShapeDtypeStruct((n, WIDTH), BF16)],
        compiler_params=_params(("arbitrary",)),
        name="in_proj_sample",
    )(x.reshape(n, d_model), *shared_args, coef, bias, g_out_a[None])


def _score_biases():
    row = (np.arange(2 * CHUNK) % CHUNK)[:, None]
    col = np.arange(2 * CHUNK)[None, :]
    in_band = (col >= row) & (col <= row + CHUNK)
    band = np.stack([in_band & (col >= CHUNK), in_band])
    causal = col[:, :CHUNK] <= row
    to_bias = lambda ok: jnp.asarray(np.where(ok, 0.0, -np.inf), F32)
    return to_bias(band), to_bias(causal)


def _prompt_attention(q, k, v):
    batch, _, seq, _ = q.shape
    n_blocks = seq // CHUNK
    assert all(n_blocks % dil == 0 for _, dil in DILATED_PATTERNS)
    spec = pl.BlockSpec((None, N_SLABS, seq, LANES), lambda b: (b, 0, 0, 0))
    band, causal = _score_biases()
    return pl.pallas_call(
        functools.partial(_prompt_attn_kernel, n_blocks=n_blocks),
        grid=(batch,),
        in_specs=[spec, spec, spec, _const_spec(band.shape), _const_spec(causal.shape)],
        out_specs=spec,
        out_shape=jax.ShapeDtypeStruct(q.shape, F32),
        scratch_shapes=[pltpu.VMEM((N_SLABS, seq, LANES), F32), pltpu.VMEM((N_SLABS, seq, LANES), F32),
                        pltpu.VMEM((2 * CHUNK, 2 * CHUNK), F32), pltpu.VMEM((2 * CHUNK, 2 * CHUNK), BF16)],
        compiler_params=_params(("parallel",)),
        name="prompt_attention",
    )(q, k, v, band, causal)


def _sample_masks(seq, cache_len, past_len, n_rows):
    cache = np.zeros((n_rows, cache_len), np.float32)
    new = [[False] * n_rows for _ in range(seq)]
    for b, (window, dil) in enumerate(DILATED_PATTERNS):
        for t in range(seq):
            for jj in range(window // dil + 1):
                idx = cache_len + t - dil * jj
                if idx < 0 or idx + (past_len - cache_len) < 0:
                    continue
                if idx < cache_len:
                    cache[b * seq + t, idx] = 1.0
                else:
                    new[idx - cache_len][b * seq + t] = True
    return cache, new


UNIT_HEADS = 4
UNITS_PER_SEQ = N_HEADS // UNIT_HEADS


def _sample_attention_operands(q, k, v, cache_k, cache_v):
    n_seq, seq, _ = q.shape
    per_head = lambda a: jnp.pad(a.reshape(n_seq, seq, N_HEADS, HEAD_DIM).transpose(0, 2, 1, 3),
                                 ((0, 0), (0, 0), (0, 2 * seq - seq), (0, 0)))
    return (per_head(q), per_head(k), per_head(v),
            jnp.transpose(cache_k, (0, 2, 3, 1)), jnp.transpose(cache_v, (0, 2, 3, 1)))


def _sample_attention_call(operands, seq, past_len, first, count):
    q8, kn, vn, kt, vt = operands
    n_rows = 4 * seq
    assert len(DILATED_PATTERNS) * seq <= n_rows
    cache_len = kt.shape[3]
    mask, new_valid = _sample_masks(seq, cache_len, past_len, n_rows)
    at_unit = lambda i: ((first + i) // UNITS_PER_SEQ, (first + i) % UNITS_PER_SEQ, 0, 0)
    new_spec = pl.BlockSpec((None, UNIT_HEADS, 2 * seq, HEAD_DIM), at_unit)
    cache_spec = pl.BlockSpec((None, UNIT_HEADS, HEAD_DIM, cache_len), at_unit)
    return dict(
        stages=functools.partial(_sample_attn_stages, seq=seq, new_valid=new_valid),
        args=(q8, kn, vn, kt, vt, jnp.asarray(mask)),
        in_specs=[new_spec, new_spec, new_spec, cache_spec, cache_spec, _const_spec((n_rows, cache_len))],
        out_specs=[pl.BlockSpec((None, UNIT_HEADS, n_rows, HEAD_DIM), lambda i: (i, 0, 0, 0))],
        out_shapes=[jax.ShapeDtypeStruct((count, UNIT_HEADS, n_rows, HEAD_DIM), F32)],
        scratch=[pltpu.VMEM((UNIT_HEADS, n_rows, cache_len), F32), pltpu.VMEM((UNIT_HEADS, n_rows, cache_len), BF16)],
        steps=count)


def _sample_attention_slabs(unit_outs, seq):
    out = jnp.concatenate(unit_outs, axis=0)
    out = out.reshape(-1, N_HEADS, out.shape[2], HEAD_DIM)
    n_seq = out.shape[0]
    b_out = out[:, :, :seq, :].transpose(0, 2, 1, 3).reshape(n_seq * seq, N_SLABS, LANES)
    return b_out.transpose(1, 0, 2)[None]


def _finish_call(x2d, mix_a, b_slab, g_out_b, w_o, g_ffn, w_gate, w_up, w_down, *, tm):
    n, d_model = x2d.shape
    nt = b_slab.shape[2] // tm
    d_ff = w_gate.shape[1]
    return dict(
        stages=_finish_stages,
        args=(x2d, mix_a, b_slab, g_out_b[None], w_o.astype(BF16), g_ffn[None],
              w_gate.astype(BF16), w_up.astype(BF16), w_down.astype(BF16)),
        in_specs=[pl.BlockSpec((tm, d_model), lambda i: (i, 0)),
                  pl.BlockSpec((tm, WIDTH), lambda i: (i, 0)),
                  pl.BlockSpec((None, N_SLABS, tm, LANES), lambda i: (i // nt, 0, i % nt, 0)),
                  _const_spec((1, WIDTH)), _const_spec((2 * WIDTH, d_model)), _const_spec((1, d_model)),
                  _const_spec((d_model, d_ff)), _const_spec((d_model, d_ff)), _const_spec((d_ff, d_model))],
        out_specs=[pl.BlockSpec((tm, d_model), lambda i: (i, 0))],
        out_shapes=[jax.ShapeDtypeStruct((n, d_model), F32)],
        scratch=[], steps=n // tm)


def kernel(x_prompt, x_sample, cache_k, cache_v, g_attn, w_in, ln_v_g, ln_v_b, w_s, b_s, g_q, g_k,
           g_out_a, g_out_b, w_o, g_ffn, w_gate, w_up, w_down):
    depth = w_in.shape[0]
    assert depth == 1, "single-layer step"
    batch, seq, d_model = x_prompt.shape
    n_seq, dec_seq, _ = x_sample.shape
    l = 0
    shared = _shared_in_proj_args(g_attn[l], w_in[l], ln_v_g[l], ln_v_b[l], g_q[l], g_k[l])
    fin = (g_out_b[l], w_o[l], g_ffn[l], w_gate[l], w_up[l], w_down[l])
    tm = 256

    qs, ks, vs, vns, mix_as = _in_proj_sample(x_sample, shared, w_s[l], b_s[l], g_out_a[l])
    as3 = lambda a: a.reshape(n_seq, dec_seq, WIDTH)
    att_ops = _sample_attention_operands(as3(qs), as3(ks), as3(vs), cache_k[l], cache_v[l])
    steps = batch * seq // tm
    streams = n_seq * UNITS_PER_SEQ // steps
    assert streams * steps == n_seq * UNITS_PER_SEQ and streams >= 2
    unit_stream = lambda k: _sample_attention_call(att_ops, dec_seq, PAST_LEN, k * steps, steps)

    def hosted(host, hosted_streams, name):
        att = list(range(1, len(hosted_streams) + 1))
        outs = _staged_call([host] + hosted_streams, att + [0] + att + att + [0], name)
        return outs[0], [o[0] for o in outs[1:]]

    in_proj = _in_proj_prompt_call(x_prompt, shared, w_s[l], b_s[l], g_out_a[l], tm=tm)
    (q, k, v, k_out, v_out, mix_a), att_a = hosted(in_proj, [unit_stream(0)], "in_proj_prompt_and_sample_attention")
    b_slab = _prompt_attention(q, k, v)
    finish = _finish_call(x_prompt.reshape(batch * seq, d_model), mix_a, b_slab, *fin, tm=tm)
    (y_prompt,), att_b = hosted(finish, [unit_stream(k) for k in range(1, streams)], "finish_and_sample_attention")
    win = min(max(w for w, _ in DILATED_PATTERNS), seq)
    window = lambda a: a.reshape(batch, N_HEADS, HEAD_DIM, seq).transpose(0, 3, 1, 2)[:, -win:][None]

    bs_slab = _sample_attention_slabs(att_a + att_b, dec_seq)
    finish_s = _finish_call(x_sample.reshape(n_seq * dec_seq, d_model), mix_as, bs_slab, *fin, tm=tm)
    ((y_sample,),) = _staged_call([finish_s], [0, 0], "finish")

    head_shape = (1, n_seq, dec_seq, N_HEADS, HEAD_DIM)
    return (y_prompt.reshape(batch, seq, d_model), y_sample.reshape(n_seq, dec_seq, d_model),
            window(k_out), window(v_out), ks.reshape(head_shape), vs.reshape(head_shape),
            vns.reshape(1, n_seq, dec_seq, WIDTH))
```

```python
import functools

import numpy as np
import jax
import jax.numpy as jnp
from jax import lax
from jax.experimental import pallas as pl
from jax.experimental.pallas import tpu as pltpu

F32 = jnp.float32
BF16 = jnp.bfloat16

HEAD_DIM = 64
N_HEADS = 8
WIDTH = N_HEADS * HEAD_DIM
CHUNK = 128
DILATED_PATTERNS = ((128, 1), (512, 4), (2048, 16))
EPS = 1e-6
PAST_LEN = 8192
NEG_BIG = -1e30
LOG2_E = 1.4426950408889634

LANES = 128
N_SLABS = WIDTH // LANES
VMEM_LIMIT = 56 * 1024 * 1024


def _dot(a, b):
    return jnp.dot(a, b, preferred_element_type=F32)


def _dot_nt(a, b):
    return lax.dot_general(a, b, (((1,), (1,)), ((), ())), preferred_element_type=F32)


def _const_spec(shape):
    return pl.BlockSpec(shape, lambda *_: (0,) * len(shape), pipeline_mode=pl.Buffered(1))


def _head_rms(z, bd_ref, g_ref):
    msq = _dot((z * z).astype(BF16), bd_ref[...])
    return z * lax.rsqrt(msq + EPS) * g_ref[...]


def _features(x_ref, g_attn_ref, w_in_ref, ln_g_ref, ln_b_ref, gq_ref, gk_ref, bd_ref):
    x = x_ref[...]
    ms = jnp.mean(x * x, axis=-1, keepdims=True)
    xn = (x * lax.rsqrt(ms + EPS) * g_attn_ref[...]).astype(BF16)

    def proj(j):
        return _dot(xn, w_in_ref[:, j * WIDTH:(j + 1) * WIDTH])

    u = jax.nn.gelu(proj(0))
    gv = jax.nn.gelu(proj(1))
    mu = jnp.mean(gv, axis=-1, keepdims=True)
    xc = gv - mu
    var = jnp.mean(xc * xc, axis=-1, keepdims=True)
    vn = xc * lax.rsqrt(var + EPS) * ln_g_ref[...] + ln_b_ref[...]
    q = _head_rms(proj(2), bd_ref, gq_ref)
    k = _head_rms(proj(3), bd_ref, gk_ref)
    v = proj(4)
    return u, vn, q, k, v


def _group_norm_rows(slabs, g_ref):
    ssq = sum(jnp.sum(a * a, axis=-1, keepdims=True) for a in slabs)
    r = lax.rsqrt(ssq * (1.0 / WIDTH) + EPS)
    return [a * r * g_ref[:, s * LANES:(s + 1) * LANES] for s, a in enumerate(slabs)]


def _in_proj_prompt_stages(x_ref, g_attn_ref, w_in_ref, ln_g_ref, ln_b_ref, gq_ref, gk_ref, bd_ref,
                           ws_ref, bs_ref, g_out_a_ref,
                           q_ref, k_ref, v_ref, kout_ref, vout_ref, mixa_ref):
    u, vn, q, k, v = _features(x_ref, g_attn_ref, w_in_ref, ln_g_ref, ln_b_ref, gq_ref, gk_ref, bd_ref)
    yield
    tm = u.shape[0]
    first_head = lax.broadcasted_iota(jnp.int32, (CHUNK, LANES), 1) < HEAD_DIM
    for c in range(tm // CHUNK):
        rows = slice(c * CHUNK, (c + 1) * CHUNK)
        vn_c = vn[rows].astype(BF16)
        slabs = []
        for s in range(N_SLABS):
            cols = slice(s * LANES, (s + 1) * LANES)
            g0 = _dot(ws_ref[2 * s], vn_c[:, cols])
            g1 = _dot(ws_ref[2 * s + 1], vn_c[:, cols])
            gate = jnp.where(first_head, g0, g1) + bs_ref[:, cols]
            slabs.append(u[rows, cols] * gate)
        for s, a in enumerate(_group_norm_rows(slabs, g_out_a_ref)):
            mixa_ref[rows, s * LANES:(s + 1) * LANES] = a.astype(BF16)
    for s in range(N_SLABS):
        cols = slice(s * LANES, (s + 1) * LANES)
        q_ref[s] = q[:, cols]
        k_ref[s] = k[:, cols]
        v_ref[s] = v[:, cols]
    kout_ref[...] = k.T
    vout_ref[...] = v.T


def _in_proj_sample_kernel(x_ref, g_attn_ref, w_in_ref, ln_g_ref, ln_b_ref, gq_ref, gk_ref, bd_ref,
                           coef_ref, bias_ref, g_out_a_ref,
                           q_ref, k_ref, v_ref, vn_ref, mixa_ref, *, seq):
    u, vn, q, k, v = _features(x_ref, g_attn_ref, w_in_ref, ln_g_ref, ln_b_ref, gq_ref, gk_ref, bd_ref)
    tm = u.shape[0]
    sub = coef_ref.shape[1]
    gate = jnp.zeros((tm // sub, sub, WIDTH), F32) + bias_ref[...]
    for d in range(seq):
        shifted = vn if d == 0 else pltpu.roll(vn, d, axis=0)
        gate = gate + shifted.reshape(tm // sub, sub, WIDTH) * coef_ref[d]
    a = u * gate.reshape(tm, WIDTH)
    slabs = _group_norm_rows([a[:, s * LANES:(s + 1) * LANES] for s in range(N_SLABS)], g_out_a_ref)
    for s, a_s in enumerate(slabs):
        mixa_ref[:, s * LANES:(s + 1) * LANES] = a_s.astype(BF16)
    q_ref[...] = q
    k_ref[...] = k
    v_ref[...] = v
    vn_ref[...] = vn


def _prompt_attn_kernel(q_ref, k_ref, v_ref, band_ref, causal_ref, o_ref, m_sc, l_sc, s_sc, p_sc, *, n_blocks):
    (w_near, d_near), (w_mid, d_mid), (w_far, d_far) = DILATED_PATTERNS
    assert d_near == 1 and d_far == n_blocks and n_blocks % d_mid == 0
    assert w_near // d_near == w_mid // d_mid == w_far // d_far == CHUNK
    mid_blocks = n_blocks // d_mid
    sub = CHUNK // 2

    first_head = lax.broadcasted_iota(jnp.int32, (CHUNK, LANES), 1) < HEAD_DIM
    ones = jnp.ones((2 * CHUNK, LANES), BF16)

    def attend(s, cur, prev, has_prev):
        qb = q_ref[s, cur, :].astype(BF16)
        zero = jnp.zeros_like(qb)
        lhs = jnp.concatenate([jnp.where(first_head, qb, zero), jnp.where(first_head, zero, qb)], axis=0)
        kk = k_ref[s, cur, :]
        vv = v_ref[s, cur, :]
        if prev is not None:
            kk = jnp.concatenate([k_ref[s, prev, :], kk], axis=0)
            vv = jnp.concatenate([v_ref[s, prev, :], vv], axis=0)
        n_keys = kk.shape[0]
        s_sc[s, :, :n_keys] = _dot_nt(lhs, kk.astype(BF16))
        m_parts = []
        for c in range(2 * CHUNK // sub):
            rows = slice(c * sub, (c + 1) * sub)
            bias = causal_ref[rows, :] if prev is None else band_ref[has_prev, rows, :]
            sc = s_sc[s, rows, :n_keys] + bias
            m = jnp.max(sc, axis=-1, keepdims=True)
            p_sc[s, rows, :n_keys] = jnp.exp2(sc - m).astype(BF16)
            m_parts.append(jnp.broadcast_to(m, (sub, LANES)))
        v_ext = jnp.concatenate([vv.astype(BF16), ones[:n_keys]], axis=1)
        out = _dot(p_sc[s, :, :n_keys], v_ext)
        both = lambda a: jnp.where(first_head, a[:CHUNK], a[CHUNK:])
        return both(jnp.concatenate(m_parts, axis=0)), both(out[:, LANES:]), both(out[:, :LANES])

    def merged(s, cur, stats):
        m_new, l_new, acc_new = stats
        m_old = m_sc[s, cur, :]
        m_tot = jnp.maximum(m_old, m_new)
        a_old = jnp.exp2(m_old - m_tot)
        a_new = jnp.exp2(m_new - m_tot)
        return (m_tot, a_old * l_sc[s, cur, :] + a_new * l_new, a_old * o_ref[s, cur, :] + a_new * acc_new)

    def far_class(c, carry):
        cur = pl.ds(c, CHUNK, stride=d_far)
        for s in range(N_SLABS):
            m_sc[s, cur, :], l_sc[s, cur, :], o_ref[s, cur, :] = attend(s, cur, None, None)
        return carry

    lax.fori_loop(0, d_far, far_class, 0, unroll=4)

    def span(jb, carry):
        for r in range(d_mid):
            cur = pl.ds(r + (d_mid * CHUNK) * jb, CHUNK, stride=d_mid)
            prev = pl.ds(r + (d_mid * CHUNK) * jnp.maximum(jb - 1, 0), CHUNK, stride=d_mid)
            for s in range(N_SLABS):
                m_sc[s, cur, :], l_sc[s, cur, :], o_ref[s, cur, :] = merged(s, cur, attend(s, cur, prev, jnp.minimum(jb, 1)))
        for i in range(d_mid):
            j = jb * d_mid + i
            cur = pl.ds(pl.multiple_of(j * CHUNK, CHUNK), CHUNK)
            prev = pl.ds(pl.multiple_of(jnp.maximum(j - 1, 0) * CHUNK, CHUNK), CHUNK)
            for s in range(N_SLABS):
                _, l_tot, acc_tot = merged(s, cur, attend(s, cur, prev, jnp.minimum(j, 1)))
                o_ref[s, cur, :] = acc_tot / l_tot
        return carry

    lax.fori_loop(0, mid_blocks, span, 0)


def _sample_attn_stages(q_ref, kn_ref, vn_ref, kt_ref, vt_ref, mask_ref, o_ref, s_sc, p_sc, *, seq, new_valid):
    n_rows = mask_ref.shape[0]
    n_real = len(DILATED_PATTERNS) * seq
    assert q_ref.shape[1] == 2 * seq and n_rows == 4 * seq

    def branch_rows(a):
        return jnp.concatenate([a + pltpu.roll(a, seq, axis=0), a], axis=0)

    row = lax.broadcasted_iota(jnp.int32, (n_rows, 1), 0)
    real = row < n_real
    valid = mask_ref[...] > 0.5
    new_rows = []
    for tp in range(seq):
        sel = functools.reduce(jnp.logical_or, [row == r for r in range(n_rows) if new_valid[tp][r]])
        new_rows.append(sel)

    heads = range(q_ref.shape[0])
    for h in heads:
        sc = _dot(q_ref[h].astype(BF16), kt_ref[h].astype(BF16))
        s_sc[h] = jnp.where(valid, branch_rows(sc), -jnp.inf)
    yield

    weights = []
    for h in heads:
        q = branch_rows(q_ref[h])
        kn = kn_ref[h]
        s_new = [jnp.where(new_rows[tp], jnp.sum(q * kn[tp:tp + 1, :], axis=-1, keepdims=True), -jnp.inf)
                 for tp in range(seq)]
        sc = s_sc[h]
        m = jnp.max(sc, axis=-1, keepdims=True)
        for sn in s_new:
            m = jnp.maximum(m, sn)
        m = jnp.where(real, m, 0.0)
        p = jnp.exp2(sc - m)
        p_sc[h] = p.astype(BF16)
        p_new = [jnp.exp2(sn - m) for sn in s_new]
        l = jnp.sum(p, axis=-1, keepdims=True) + sum(p_new)
        m_eff = jnp.where(real, m, NEG_BIG)
        l_eff = jnp.where(real, l, 0.0)
        m_all = m_eff
        for i in range(1, n_rows // seq):
            m_all = jnp.maximum(m_all, pltpu.roll(m_eff, i * seq, axis=0))
        c = jnp.exp2(m_eff - m_all)
        cl = c * l_eff
        den = cl
        for i in range(1, n_rows // seq):
            den = den + pltpu.roll(cl, i * seq, axis=0)
        w = jnp.where(real, c / den, 0.0)
        weights.append((w, [pn * w for pn in p_new]))
    yield

    for h in heads:
        w, pw_new = weights[h]
        vn = vn_ref[h]
        out = _dot_nt(p_sc[h], vt_ref[h].astype(BF16)) * w
        for tp in range(seq):
            out = out + pw_new[tp] * vn[tp:tp + 1, :]
        tot = out
        for i in range(1, n_rows // seq):
            tot = tot + pltpu.roll(out, i * seq, axis=0)
        o_ref[h] = tot


def _finish_stages(x_ref, mixa_ref, b_ref, g_out_b_ref, wo_ref, g_ffn_ref, wg_ref, wu_ref, wd_ref, o_ref):
    slabs = _group_norm_rows([b_ref[s] for s in range(N_SLABS)], g_out_b_ref)
    mix_b = jnp.concatenate([a.astype(BF16) for a in slabs], axis=-1)
    x1 = x_ref[...] + _dot(mixa_ref[...], wo_ref[:WIDTH, :]) + _dot(mix_b, wo_ref[WIDTH:, :])
    ms = jnp.mean(x1 * x1, axis=-1, keepdims=True)
    h = (x1 * lax.rsqrt(ms + EPS) * g_ffn_ref[...]).astype(BF16)
    act = (jax.nn.silu(_dot(h, wg_ref[...])) * _dot(h, wu_ref[...])).astype(BF16)
    yield
    o_ref[...] = x1 + _dot(act, wd_ref[...])


def _params(sem):
    return pltpu.CompilerParams(dimension_semantics=sem, vmem_limit_bytes=VMEM_LIMIT)


_DONE = object()


def _staged_call(calls, order, name):
    steps = calls[0]["steps"]
    assert all(c["steps"] == steps for c in calls)
    counts = [(len(c["args"]), len(c["out_specs"]), len(c["scratch"])) for c in calls]

    def body(*refs):
        groups = []
        pos = 0
        for kind in range(3):
            per_call = []
            for cnt in counts:
                per_call.append(refs[pos:pos + cnt[kind]])
                pos += cnt[kind]
            groups.append(per_call)
        gens = [c["stages"](*groups[0][i], *groups[1][i], *groups[2][i]) for i, c in enumerate(calls)]
        for i in order:
            next(gens[i], None)
        assert all(next(g, _DONE) is _DONE for g in gens), "order leaves stages untraced"

    cat = lambda key: [item for c in calls for item in c[key]]
    outs = pl.pallas_call(
        body, grid=(steps,), in_specs=cat("in_specs"), out_specs=cat("out_specs"), out_shape=cat("out_shapes"),
        scratch_shapes=cat("scratch"), compiler_params=_params(("parallel",)), name=name,
    )(*cat("args"))
    split, pos = [], 0
    for _, n_out, _ in counts:
        split.append(outs[pos:pos + n_out])
        pos += n_out
    return split


def _shared_in_proj_args(g_attn, w_in, ln_v_g, ln_v_b, g_q, g_k):
    d_model, in_width = w_in.shape
    gq = (jnp.tile(g_q, N_HEADS) * (HEAD_DIM ** -0.5 * LOG2_E))[None]
    gk = jnp.tile(g_k, N_HEADS)[None]
    bd = jnp.asarray(np.kron(np.eye(N_HEADS), np.full((HEAD_DIM, HEAD_DIM), 1.0 / HEAD_DIM)), BF16)
    args = (g_attn[None], w_in.astype(BF16), ln_v_g[None], ln_v_b[None], gq, gk, bd)
    specs = [_const_spec((1, d_model)), _const_spec((d_model, in_width)), _const_spec((1, WIDTH)),
             _const_spec((1, WIDTH)), _const_spec((1, WIDTH)), _const_spec((1, WIDTH)),
             _const_spec((WIDTH, WIDTH))]
    return args, specs


def _in_proj_prompt_call(x, shared, w_s, b_s, g_out_a, *, tm):
    batch, seq, d_model = x.shape
    n = batch * seq
    nt = seq // tm
    shared_args, shared_specs = shared
    tri = np.tril(np.ones((CHUNK, CHUNK), bool))
    ws = jnp.where(tri, w_s, 0.0).astype(BF16)
    bs = jnp.repeat(b_s.T, HEAD_DIM, axis=1)
    slab = jax.ShapeDtypeStruct((batch, N_SLABS, seq, LANES), F32)
    flat = jax.ShapeDtypeStruct((batch, WIDTH, seq), F32)
    slab_spec = pl.BlockSpec((None, N_SLABS, tm, LANES), lambda i: (i // nt, 0, i % nt, 0))
    flat_spec = pl.BlockSpec((None, WIDTH, tm), lambda i: (i // nt, 0, i % nt))
    row_spec = pl.BlockSpec((tm, WIDTH), lambda i: (i, 0))
    return dict(
        stages=_in_proj_prompt_stages,
        args=(x.reshape(n, d_model), *shared_args, ws, bs, g_out_a[None]),
        in_specs=[pl.BlockSpec((tm, d_model), lambda i: (i, 0))] + shared_specs + [
            _const_spec((N_HEADS, CHUNK, CHUNK)), _const_spec((CHUNK, WIDTH)), _const_spec((1, WIDTH))],
        out_specs=[slab_spec, slab_spec, slab_spec, flat_spec, flat_spec, row_spec],
        out_shapes=[slab, slab, slab, flat, flat, jax.ShapeDtypeStruct((n, WIDTH), BF16)],
        scratch=[], steps=n // tm)


def _in_proj_sample(x, shared, w_s, b_s, g_out_a):
    n_seq, seq, d_model = x.shape
    n = n_seq * seq
    shared_args, shared_specs = shared
    sub = 8
    assert sub % seq == 0 and seq <= CHUNK
    step = np.arange(sub) % seq
    corner = w_s[:, :seq, :seq]
    taps = []
    for d in range(seq):
        src = np.maximum(step - d, 0)
        tap = jnp.where((step >= d)[:, None], corner[:, step, src].T, 0.0)
        taps.append(jnp.repeat(tap, HEAD_DIM, axis=1))
    coef = jnp.stack(taps)
    bias = jnp.repeat(b_s[:, step].T, HEAD_DIM, axis=1)
    flat = jax.ShapeDtypeStruct((n, WIDTH), F32)
    row_spec = pl.BlockSpec((n, WIDTH), lambda i: (0, 0))
    return pl.pallas_call(
        functools.partial(_in_proj_sample_kernel, seq=seq),
        grid=(1,),
        in_specs=[pl.BlockSpec((n, d_model), lambda i: (0, 0))] + shared_specs + [
            _const_spec((seq, sub, WIDTH)), _const_spec((sub, WIDTH)), _const_spec((1, WIDTH))],
        out_specs=[row_spec] * 5,
        out_shape=[flat, flat, flat, flat, jax.ShapeDtypeStruct((n, WIDTH), BF16)],
        compiler_params=_params(("arbitrary",)),
        name="in_proj_sample",
    )(x.reshape(n, d_model), *shared_args, coef, bias, g_out_a[None])


def _score_biases():
    row = (np.arange(2 * CHUNK) % CHUNK)[:, None]
    col = np.arange(2 * CHUNK)[None, :]
    in_band = (col >= row) & (col <= row + CHUNK)
    band = np.stack([in_band & (col >= CHUNK), in_band])
    causal = col[:, :CHUNK] <= row
    to_bias = lambda ok: jnp.asarray(np.where(ok, 0.0, -np.inf), F32)
    return to_bias(band), to_bias(causal)


def _prompt_attention(q, k, v):
    batch, _, seq, _ = q.shape
    n_blocks = seq // CHUNK
    assert all(n_blocks % dil == 0 for _, dil in DILATED_PATTERNS)
    spec = pl.BlockSpec((None, N_SLABS, seq, LANES), lambda b: (b, 0, 0, 0))
    band, causal = _score_biases()
    return pl.pallas_call(
        functools.partial(_prompt_attn_kernel, n_blocks=n_blocks),
        grid=(batch,),
        in_specs=[spec, spec, spec, _const_spec(band.shape), _const_spec(causal.shape)],
        out_specs=spec,
        out_shape=jax.ShapeDtypeStruct(q.shape, F32),
        scratch_shapes=[pltpu.VMEM((N_SLABS, seq, LANES), F32), pltpu.VMEM((N_SLABS, seq, LANES), F32),
                        pltpu.VMEM((N_SLABS, 2 * CHUNK, 2 * CHUNK), F32),
                        pltpu.VMEM((N_SLABS, 2 * CHUNK, 2 * CHUNK), BF16)],
        compiler_params=_params(("parallel",)),
        name="prompt_attention",
    )(q, k, v, band, causal)


def _sample_masks(seq, cache_len, past_len, n_rows):
    cache = np.zeros((n_rows, cache_len), np.float32)
    new = [[False] * n_rows for _ in range(seq)]
    for b, (window, dil) in enumerate(DILATED_PATTERNS):
        for t in range(seq):
            for jj in range(window // dil + 1):
                idx = cache_len + t - dil * jj
                if idx < 0 or idx + (past_len - cache_len) < 0:
                    continue
                if idx < cache_len:
                    cache[b * seq + t, idx] = 1.0
                else:
                    new[idx - cache_len][b * seq + t] = True
    return cache, new


UNIT_HEADS = 4
UNITS_PER_SEQ = N_HEADS // UNIT_HEADS


def _sample_attention_operands(q, k, v, cache_k, cache_v):
    n_seq, seq, _ = q.shape
    per_head = lambda a: jnp.pad(a.reshape(n_seq, seq, N_HEADS, HEAD_DIM).transpose(0, 2, 1, 3),
                                 ((0, 0), (0, 0), (0, 2 * seq - seq), (0, 0)))
    return (per_head(q), per_head(k), per_head(v),
            jnp.transpose(cache_k, (0, 2, 3, 1)), jnp.transpose(cache_v, (0, 2, 3, 1)))


def _sample_attention_call(operands, seq, past_len, first, count):
    q8, kn, vn, kt, vt = operands
    n_rows = 4 * seq
    assert len(DILATED_PATTERNS) * seq <= n_rows
    cache_len = kt.shape[3]
    mask, new_valid = _sample_masks(seq, cache_len, past_len, n_rows)
    at_unit = lambda i: ((first + i) // UNITS_PER_SEQ, (first + i) % UNITS_PER_SEQ, 0, 0)
    new_spec = pl.BlockSpec((None, UNIT_HEADS, 2 * seq, HEAD_DIM), at_unit)
    cache_spec = pl.BlockSpec((None, UNIT_HEADS, HEAD_DIM, cache_len), at_unit)
    return dict(
        stages=functools.partial(_sample_attn_stages, seq=seq, new_valid=new_valid),
        args=(q8, kn, vn, kt, vt, jnp.asarray(mask)),
        in_specs=[new_spec, new_spec, new_spec, cache_spec, cache_spec, _const_spec((n_rows, cache_len))],
        out_specs=[pl.BlockSpec((None, UNIT_HEADS, n_rows, HEAD_DIM), lambda i: (i, 0, 0, 0))],
        out_shapes=[jax.ShapeDtypeStruct((count, UNIT_HEADS, n_rows, HEAD_DIM), F32)],
        scratch=[pltpu.VMEM((UNIT_HEADS, n_rows, cache_len), F32), pltpu.VMEM((UNIT_HEADS, n_rows, cache_len), BF16)],
        steps=count)


def _sample_attention_slabs(unit_outs, seq):
    out = jnp.concatenate(unit_outs, axis=0)
    out = out.reshape(-1, N_HEADS, out.shape[2], HEAD_DIM)
    n_seq = out.shape[0]
    b_out = out[:, :, :seq, :].transpose(0, 2, 1, 3).reshape(n_seq * seq, N_SLABS, LANES)
    return b_out.transpose(1, 0, 2)[None]


def _finish_call(x2d, mix_a, b_slab, g_out_b, w_o, g_ffn, w_gate, w_up, w_down, *, tm):
    n, d_model = x2d.shape
    nt = b_slab.shape[2] // tm
    d_ff = w_gate.shape[1]
    return dict(
        stages=_finish_stages,
        args=(x2d, mix_a, b_slab, g_out_b[None], w_o.astype(BF16), g_ffn[None],
              w_gate.astype(BF16), w_up.astype(BF16), w_down.astype(BF16)),
        in_specs=[pl.BlockSpec((tm, d_model), lambda i: (i, 0)),
                  pl.BlockSpec((tm, WIDTH), lambda i: (i, 0)),
                  pl.BlockSpec((None, N_SLABS, tm, LANES), lambda i: (i // nt, 0, i % nt, 0)),
                  _const_spec((1, WIDTH)), _const_spec((2 * WIDTH, d_model)), _const_spec((1, d_model)),
                  _const_spec((d_model, d_ff)), _const_spec((d_model, d_ff)), _const_spec((d_ff, d_model))],
        out_specs=[pl.BlockSpec((tm, d_model), lambda i: (i, 0))],
        out_shapes=[jax.ShapeDtypeStruct((n, d_model), F32)],
        scratch=[], steps=n // tm)


def kernel(x_prompt, x_sample, cache_k, cache_v, g_attn, w_in, ln_v_g, ln_v_b, w_s, b_s, g_q, g_k,
           g_out_a, g_out_b, w_o, g_ffn, w_gate, w_up, w_down):
    depth = w_in.shape[0]
    assert depth == 1, "single-layer step"
    batch, seq, d_model = x_prompt.shape
    n_seq, dec_seq, _ = x_sample.shape
    l = 0
    shared = _shared_in_proj_args(g_attn[l], w_in[l], ln_v_g[l], ln_v_b[l], g_q[l], g_k[l])
    fin = (g_out_b[l], w_o[l], g_ffn[l], w_gate[l], w_up[l], w_down[l])
    tm = 256

    qs, ks, vs, vns, mix_as = _in_proj_sample(x_sample, shared, w_s[l], b_s[l], g_out_a[l])
    as3 = lambda a: a.reshape(n_seq, dec_seq, WIDTH)
    att_ops = _sample_attention_operands(as3(qs), as3(ks), as3(vs), cache_k[l], cache_v[l])
    steps = batch * seq // tm
    streams = n_seq * UNITS_PER_SEQ // steps
    assert streams * steps == n_seq * UNITS_PER_SEQ and streams >= 2
    unit_stream = lambda k: _sample_attention_call(att_ops, dec_seq, PAST_LEN, k * steps, steps)

    def hosted(host, hosted_streams, name):
        att = list(range(1, len(hosted_streams) + 1))
        outs = _staged_call([host] + hosted_streams, att + [0] + att + att + [0], name)
        return outs[0], [o[0] for o in outs[1:]]

    in_proj = _in_proj_prompt_call(x_prompt, shared, w_s[l], b_s[l], g_out_a[l], tm=tm)
    (q, k, v, k_out, v_out, mix_a), att_a = hosted(in_proj, [unit_stream(0)], "in_proj_prompt_and_sample_attention")
    b_slab = _prompt_attention(q, k, v)
    finish = _finish_call(x_prompt.reshape(batch * seq, d_model), mix_a, b_slab, *fin, tm=tm)
    (y_prompt,), att_b = hosted(finish, [unit_stream(k) for k in range(1, streams)], "finish_and_sample_attention")
    win = min(max(w for w, _ in DILATED_PATTERNS), seq)
    window = lambda a: a.reshape(batch, N_HEADS, HEAD_DIM, seq).transpose(0, 3, 1, 2)[:, -win:][None]

    bs_slab = _sample_attention_slabs(att_a + att_b, dec_seq)
    finish_s = _finish_call(x_sample.reshape(n_seq * dec_seq, d_model), mix_as, bs_slab, *fin, tm=tm)
    ((y_sample,),) = _staged_call([finish_s], [0, 0], "finish")

    head_shape = (1, n_seq, dec_seq, N_HEADS, HEAD_DIM)
    return (y_prompt.reshape(batch, seq, d_model), y_sample.reshape(n_seq, dec_seq, d_model),
            window(k_out), window(v_out), ks.reshape(head_shape), vs.reshape(head_shape),
            vns.reshape(1, n_seq, dec_seq, WIDTH))
```

```python
import functools

import numpy as np
import jax
import jax.numpy as jnp
from jax import lax
from jax.experimental import pallas as pl
from jax.experimental.pallas import tpu as pltpu

F32 = jnp.float32
BF16 = jnp.bfloat16

HEAD_DIM = 64
N_HEADS = 8
WIDTH = N_HEADS * HEAD_DIM
CHUNK = 128
DILATED_PATTERNS = ((128, 1), (512, 4), (2048, 16))
EPS = 1e-6
PAST_LEN = 8192
NEG_BIG = -1e30
LOG2_E = 1.4426950408889634

LANES = 128
N_SLABS = WIDTH // LANES
VMEM_LIMIT = 56 * 1024 * 1024


def _dot(a, b):
    return jnp.dot(a, b, preferred_element_type=F32)


def _dot_nt(a, b):
    return lax.dot_general(a, b, (((1,), (1,)), ((), ())), preferred_element_type=F32)


def _const_spec(shape):
    return pl.BlockSpec(shape, lambda *_: (0,) * len(shape), pipeline_mode=pl.Buffered(1))


def _head_rms(z, bd_ref, g_ref):
    msq = _dot((z * z).astype(BF16), bd_ref[...])
    return z * lax.rsqrt(msq + EPS) * g_ref[...]


def _features(x_ref, g_attn_ref, w_in_ref, ln_g_ref, ln_b_ref, gq_ref, gk_ref, bd_ref):
    x = x_ref[...]
    ms = jnp.mean(x * x, axis=-1, keepdims=True)
    xn = (x * lax.rsqrt(ms + EPS) * g_attn_ref[...]).astype(BF16)

    def proj(j):
        return _dot(xn, w_in_ref[:, j * WIDTH:(j + 1) * WIDTH])

    u = jax.nn.gelu(proj(0))
    gv = jax.nn.gelu(proj(1))
    mu = jnp.mean(gv, axis=-1, keepdims=True)
    xc = gv - mu
    var = jnp.mean(xc * xc, axis=-1, keepdims=True)
    vn = xc * lax.rsqrt(var + EPS) * ln_g_ref[...] + ln_b_ref[...]
    q = _head_rms(proj(2), bd_ref, gq_ref)
    k = _head_rms(proj(3), bd_ref, gk_ref)
    v = proj(4)
    return u, vn, q, k, v


def _group_norm_rows(slabs, g_ref):
    ssq = sum(jnp.sum(a * a, axis=-1, keepdims=True) for a in slabs)
    r = lax.rsqrt(ssq * (1.0 / WIDTH) + EPS)
    return [a * r * g_ref[:, s * LANES:(s + 1) * LANES] for s, a in enumerate(slabs)]


def _in_proj_prompt_stages(x_ref, g_attn_ref, w_in_ref, ln_g_ref, ln_b_ref, gq_ref, gk_ref, bd_ref,
                           ws_ref, bs_ref, g_out_a_ref,
                           q_ref, k_ref, v_ref, kout_ref, vout_ref, mixa_ref):
    u, vn, q, k, v = _features(x_ref, g_attn_ref, w_in_ref, ln_g_ref, ln_b_ref, gq_ref, gk_ref, bd_ref)
    yield
    tm = u.shape[0]
    first_head = lax.broadcasted_iota(jnp.int32, (CHUNK, LANES), 1) < HEAD_DIM
    for c in range(tm // CHUNK):
        rows = slice(c * CHUNK, (c + 1) * CHUNK)
        vn_c = vn[rows].astype(BF16)
        slabs = []
        for s in range(N_SLABS):
            cols = slice(s * LANES, (s + 1) * LANES)
            g0 = _dot(ws_ref[2 * s], vn_c[:, cols])
            g1 = _dot(ws_ref[2 * s + 1], vn_c[:, cols])
            gate = jnp.where(first_head, g0, g1) + bs_ref[:, cols]
            slabs.append(u[rows, cols] * gate)
        for s, a in enumerate(_group_norm_rows(slabs, g_out_a_ref)):
            mixa_ref[rows, s * LANES:(s + 1) * LANES] = a.astype(BF16)
    for s in range(N_SLABS):
        cols = slice(s * LANES, (s + 1) * LANES)
        q_ref[s] = q[:, cols]
        k_ref[s] = k[:, cols]
        v_ref[s] = v[:, cols]
    kout_ref[...] = k.T
    vout_ref[...] = v.T


def _in_proj_sample_kernel(x_ref, g_attn_ref, w_in_ref, ln_g_ref, ln_b_ref, gq_ref, gk_ref, bd_ref,
                           coef_ref, bias_ref, g_out_a_ref,
                           q_ref, k_ref, v_ref, vn_ref, mixa_ref, *, seq):
    u, vn, q, k, v = _features(x_ref, g_attn_ref, w_in_ref, ln_g_ref, ln_b_ref, gq_ref, gk_ref, bd_ref)
    tm = u.shape[0]
    sub = coef_ref.shape[1]
    gate = jnp.zeros((tm // sub, sub, WIDTH), F32) + bias_ref[...]
    for d in range(seq):
        shifted = vn if d == 0 else pltpu.roll(vn, d, axis=0)
        gate = gate + shifted.reshape(tm // sub, sub, WIDTH) * coef_ref[d]
    a = u * gate.reshape(tm, WIDTH)
    slabs = _group_norm_rows([a[:, s * LANES:(s + 1) * LANES] for s in range(N_SLABS)], g_out_a_ref)
    for s, a_s in enumerate(slabs):
        mixa_ref[:, s * LANES:(s + 1) * LANES] = a_s.astype(BF16)
    q_ref[...] = q
    k_ref[...] = k
    v_ref[...] = v
    vn_ref[...] = vn


def _prompt_attn_kernel(q_ref, k_ref, v_ref, band_ref, causal_ref, o_ref, m_sc, l_sc, s_sc, p_sc, grp_sc, sf_sc, pf_sc, *, n_blocks):
    (w_near, d_near), (w_mid, d_mid), (w_far, d_far) = DILATED_PATTERNS
    assert d_near == 1 and d_far == n_blocks and n_blocks % d_mid == 0
    assert w_near // d_near == w_mid // d_mid == w_far // d_far == CHUNK
    mid_blocks = n_blocks // d_mid
    sub = CHUNK // 2

    first_head = lax.broadcasted_iota(jnp.int32, (CHUNK, LANES), 1) < HEAD_DIM
    ones = jnp.ones((2 * CHUNK, LANES), BF16)

    both = lambda a: jnp.where(first_head, a[:CHUNK], a[CHUNK:])

    def keys_values(ref, s, cur, prev):
        a = ref[s, cur, :]
        return a if prev is None else jnp.concatenate([ref[s, prev, :], a], axis=0)

    def scores(s, cur, prev, src, s_tile):
        q_src, k_src, _ = src
        qb = q_src[s, cur, :].astype(BF16)
        zero = jnp.zeros_like(qb)
        lhs = jnp.concatenate([jnp.where(first_head, qb, zero), jnp.where(first_head, zero, qb)], axis=0)
        kk = keys_values(k_src, s, cur, prev).astype(BF16)
        s_tile[:, :kk.shape[0]] = _dot_nt(lhs, kk)

    def softmax(prev, has_prev, s_tile, p_tile):
        n_keys = CHUNK if prev is None else 2 * CHUNK
        m_parts = []
        for c in range(2 * CHUNK // sub):
            rows = slice(c * sub, (c + 1) * sub)
            bias = causal_ref[rows, :] if prev is None else band_ref[has_prev, rows, :]
            sc = s_tile[rows, :n_keys] + bias
            m = jnp.max(sc, axis=-1, keepdims=True)
            p_tile[rows, :n_keys] = jnp.exp2(sc - m).astype(BF16)
            m_parts.append(jnp.broadcast_to(m, (sub, LANES)))
        return both(jnp.concatenate(m_parts, axis=0))

    def values(s, cur, prev, src, p_tile):
        vv = keys_values(src[2], s, cur, prev).astype(BF16)
        n_keys = vv.shape[0]
        out = _dot(p_tile[:, :n_keys], jnp.concatenate([vv, ones[:n_keys]], axis=1))
        return both(out[:, LANES:]), both(out[:, :LANES])

    def attend(s, cur, prev, has_prev):
        src = (q_ref, k_ref, v_ref)
        scores(s, cur, prev, src, s_sc.at[s])
        m = softmax(prev, has_prev, s_sc.at[s], p_sc.at[s])
        return (m,) + values(s, cur, prev, src, p_sc.at[s])

    def merged(s, cur, stats):
        m_new, l_new, acc_new = stats
        m_old = m_sc[s, cur, :]
        m_tot = jnp.maximum(m_old, m_new)
        a_old = jnp.exp2(m_old - m_tot)
        a_new = jnp.exp2(m_new - m_tot)
        return (m_tot, a_old * l_sc[s, cur, :] + a_new * l_new, a_old * o_ref[s, cur, :] + a_new * acc_new)

    staged = [grp_sc.at[i] for i in range(6)]
    sub_classes = d_far // d_mid

    def far_group(r, carry):
        grp = pl.ds(r, o_ref.shape[1] // d_mid, stride=d_mid)
        for s in range(N_SLABS):
            for src, dst in zip((q_ref, k_ref, v_ref), staged[:3]):
                dst[s] = src[s, grp, :]
        blocks = [(c * N_SLABS + s, s, pl.ds(c, CHUNK, stride=sub_classes))
                  for c in range(sub_classes) for s in range(N_SLABS)]
        st_m, st_l, st_acc = staged[3:]
        for tile, s, cur in blocks:
            scores(s, cur, None, staged[:3], sf_sc.at[tile])
        for tile, s, cur in blocks:
            st_m[s, cur, :] = softmax(None, None, sf_sc.at[tile], pf_sc.at[tile])
        for tile, s, cur in blocks:
            st_l[s, cur, :], st_acc[s, cur, :] = values(s, cur, None, staged[:3], pf_sc.at[tile])
        for s in range(N_SLABS):
            for dst, src in zip((m_sc, l_sc, o_ref), staged[3:]):
                dst[s, grp, :] = src[s]
        return carry

    lax.fori_loop(0, d_mid, far_group, 0)

    def span(jb, carry):
        for r in range(d_mid):
            cur = pl.ds(r + (d_mid * CHUNK) * jb, CHUNK, stride=d_mid)
            prev = pl.ds(r + (d_mid * CHUNK) * jnp.maximum(jb - 1, 0), CHUNK, stride=d_mid)
            for s in range(N_SLABS):
                m_sc[s, cur, :], l_sc[s, cur, :], o_ref[s, cur, :] = merged(s, cur, attend(s, cur, prev, jnp.minimum(jb, 1)))
        for i in range(d_mid):
            j = jb * d_mid + i
            cur = pl.ds(pl.multiple_of(j * CHUNK, CHUNK), CHUNK)
            prev = pl.ds(pl.multiple_of(jnp.maximum(j - 1, 0) * CHUNK, CHUNK), CHUNK)
            for s in range(N_SLABS):
                _, l_tot, acc_tot = merged(s, cur, attend(s, cur, prev, jnp.minimum(j, 1)))
                o_ref[s, cur, :] = acc_tot / l_tot
        return carry

    lax.fori_loop(0, mid_blocks, span, 0)


def _sample_attn_stages(q_ref, kn_ref, vn_ref, kt_ref, vt_ref, mask_ref, o_ref, s_sc, p_sc, *, seq, new_valid):
    n_rows = mask_ref.shape[0]
    n_real = len(DILATED_PATTERNS) * seq
    assert q_ref.shape[1] == 2 * seq and n_rows == 4 * seq

    def branch_rows(a):
        return jnp.concatenate([a + pltpu.roll(a, seq, axis=0), a], axis=0)

    row = lax.broadcasted_iota(jnp.int32, (n_rows, 1), 0)
    real = row < n_real
    valid = mask_ref[...] > 0.5
    new_rows = []
    for tp in range(seq):
        sel = functools.reduce(jnp.logical_or, [row == r for r in range(n_rows) if new_valid[tp][r]])
        new_rows.append(sel)

    heads = range(q_ref.shape[0])
    for h in heads:
        sc = _dot(q_ref[h].astype(BF16), kt_ref[h].astype(BF16))
        s_sc[h] = jnp.where(valid, branch_rows(sc), -jnp.inf)
    yield

    weights = []
    for h in heads:
        q = branch_rows(q_ref[h])
        kn = kn_ref[h]
        s_new = [jnp.where(new_rows[tp], jnp.sum(q * kn[tp:tp + 1, :], axis=-1, keepdims=True), -jnp.inf)
                 for tp in range(seq)]
        sc = s_sc[h]
        m = jnp.max(sc, axis=-1, keepdims=True)
        for sn in s_new:
            m = jnp.maximum(m, sn)
        m = jnp.where(real, m, 0.0)
        p = jnp.exp2(sc - m)
        p_sc[h] = p.astype(BF16)
        p_new = [jnp.exp2(sn - m) for sn in s_new]
        l = jnp.sum(p, axis=-1, keepdims=True) + sum(p_new)
        m_eff = jnp.where(real, m, NEG_BIG)
        l_eff = jnp.where(real, l, 0.0)
        m_all = m_eff
        for i in range(1, n_rows // seq):
            m_all = jnp.maximum(m_all, pltpu.roll(m_eff, i * seq, axis=0))
        c = jnp.exp2(m_eff - m_all)
        cl = c * l_eff
        den = cl
        for i in range(1, n_rows // seq):
            den = den + pltpu.roll(cl, i * seq, axis=0)
        w = jnp.where(real, c / den, 0.0)
        weights.append((w, [pn * w for pn in p_new]))
    yield

    for h in heads:
        w, pw_new = weights[h]
        vn = vn_ref[h]
        out = _dot_nt(p_sc[h], vt_ref[h].astype(BF16)) * w
        for tp in range(seq):
            out = out + pw_new[tp] * vn[tp:tp + 1, :]
        tot = out
        for i in range(1, n_rows // seq):
            tot = tot + pltpu.roll(out, i * seq, axis=0)
        o_ref[h] = tot


def _finish_stages(x_ref, mixa_ref, b_ref, g_out_b_ref, wo_ref, g_ffn_ref, wg_ref, wu_ref, wd_ref, o_ref):
    slabs = _group_norm_rows([b_ref[s] for s in range(N_SLABS)], g_out_b_ref)
    mix_b = jnp.concatenate([a.astype(BF16) for a in slabs], axis=-1)
    x1 = x_ref[...] + _dot(mixa_ref[...], wo_ref[:WIDTH, :]) + _dot(mix_b, wo_ref[WIDTH:, :])
    ms = jnp.mean(x1 * x1, axis=-1, keepdims=True)
    h = (x1 * lax.rsqrt(ms + EPS) * g_ffn_ref[...]).astype(BF16)
    act = (jax.nn.silu(_dot(h, wg_ref[...])) * _dot(h, wu_ref[...])).astype(BF16)
    yield
    o_ref[...] = x1 + _dot(act, wd_ref[...])


def _params(sem):
    return pltpu.CompilerParams(dimension_semantics=sem, vmem_limit_bytes=VMEM_LIMIT)


_DONE = object()


def _staged_call(calls, order, name):
    steps = calls[0]["steps"]
    assert all(c["steps"] == steps for c in calls)
    counts = [(len(c["args"]), len(c["out_specs"]), len(c["scratch"])) for c in calls]

    def body(*refs):
        groups = []
        pos = 0
        for kind in range(3):
            per_call = []
            for cnt in counts:
                per_call.append(refs[pos:pos + cnt[kind]])
                pos += cnt[kind]
            groups.append(per_call)
        gens = [c["stages"](*groups[0][i], *groups[1][i], *groups[2][i]) for i, c in enumerate(calls)]
        for i in order:
            next(gens[i], None)
        assert all(next(g, _DONE) is _DONE for g in gens), "order leaves stages untraced"

    cat = lambda key: [item for c in calls for item in c[key]]
    outs = pl.pallas_call(
        body, grid=(steps,), in_specs=cat("in_specs"), out_specs=cat("out_specs"), out_shape=cat("out_shapes"),
        scratch_shapes=cat("scratch"), compiler_params=_params(("parallel",)), name=name,
    )(*cat("args"))
    split, pos = [], 0
    for _, n_out, _ in counts:
        split.append(outs[pos:pos + n_out])
        pos += n_out
    return split


def _shared_in_proj_args(g_attn, w_in, ln_v_g, ln_v_b, g_q, g_k):
    d_model, in_width = w_in.shape
    gq = (jnp.tile(g_q, N_HEADS) * (HEAD_DIM ** -0.5 * LOG2_E))[None]
    gk = jnp.tile(g_k, N_HEADS)[None]
    bd = jnp.asarray(np.kron(np.eye(N_HEADS), np.full((HEAD_DIM, HEAD_DIM), 1.0 / HEAD_DIM)), BF16)
    args = (g_attn[None], w_in.astype(BF16), ln_v_g[None], ln_v_b[None], gq, gk, bd)
    specs = [_const_spec((1, d_model)), _const_spec((d_model, in_width)), _const_spec((1, WIDTH)),
             _const_spec((1, WIDTH)), _const_spec((1, WIDTH)), _const_spec((1, WIDTH)),
             _const_spec((WIDTH, WIDTH))]
    return args, specs


def _in_proj_prompt_call(x, shared, w_s, b_s, g_out_a, *, tm):
    batch, seq, d_model = x.shape
    n = batch * seq
    nt = seq // tm
    shared_args, shared_specs = shared
    tri = np.tril(np.ones((CHUNK, CHUNK), bool))
    ws = jnp.where(tri, w_s, 0.0).astype(BF16)
    bs = jnp.repeat(b_s.T, HEAD_DIM, axis=1)
    slab = jax.ShapeDtypeStruct((batch, N_SLABS, seq, LANES), F32)
    flat = jax.ShapeDtypeStruct((batch, WIDTH, seq), F32)
    slab_spec = pl.BlockSpec((None, N_SLABS, tm, LANES), lambda i: (i // nt, 0, i % nt, 0))
    flat_spec = pl.BlockSpec((None, WIDTH, tm), lambda i: (i // nt, 0, i % nt))
    row_spec = pl.BlockSpec((tm, WIDTH), lambda i: (i, 0))
    return dict(
        stages=_in_proj_prompt_stages,
        args=(x.reshape(n, d_model), *shared_args, ws, bs, g_out_a[None]),
        in_specs=[pl.BlockSpec((tm, d_model), lambda i: (i, 0))] + shared_specs + [
            _const_spec((N_HEADS, CHUNK, CHUNK)), _const_spec((CHUNK, WIDTH)), _const_spec((1, WIDTH))],
        out_specs=[slab_spec, slab_spec, slab_spec, flat_spec, flat_spec, row_spec],
        out_shapes=[slab, slab, slab, flat, flat, jax.ShapeDtypeStruct((n, WIDTH), BF16)],
        scratch=[], steps=n // tm)


def _in_proj_sample(x, shared, w_s, b_s, g_out_a):
    n_seq, seq, d_model = x.shape
    n = n_seq * seq
    shared_args, shared_specs = shared
    sub = 8
    assert sub % seq == 0 and seq <= CHUNK
    step = np.arange(sub) % seq
    corner = w_s[:, :seq, :seq]
    taps = []
    for d in range(seq):
        src = np.maximum(step - d, 0)
        tap = jnp.where((step >= d)[:, None], corner[:, step, src].T, 0.0)
        taps.append(jnp.repeat(tap, HEAD_DIM, axis=1))
    coef = jnp.stack(taps)
    bias = jnp.repeat(b_s[:, step].T, HEAD_DIM, axis=1)
    flat = jax.ShapeDtypeStruct((n, WIDTH), F32)
    row_spec = pl.BlockSpec((n, WIDTH), lambda i: (0, 0))
    return pl.pallas_call(
        functools.partial(_in_proj_sample_kernel, seq=seq),
        grid=(1,),
        in_specs=[pl.BlockSpec((n, d_model), lambda i: (0, 0))] + shared_specs + [
            _const_spec((seq, sub, WIDTH)), _const_spec((sub, WIDTH)), _const_spec((1, WIDTH))],
        out_specs=[row_spec] * 5,
        out_shape=[flat, flat, flat, flat, jax.ShapeDtypeStruct((n, WIDTH), BF16)],
        compiler_params=_params(("arbitrary",)),
        name="in_proj_sample",
    )(x.reshape(n, d_model), *shared_args, coef, bias, g_out_a[None])


def _score_biases():
    row = (np.arange(2 * CHUNK) % CHUNK)[:, None]
    col = np.arange(2 * CHUNK)[None, :]
    in_band = (col >= row) & (col <= row + CHUNK)
    band = np.stack([in_band & (col >= CHUNK), in_band])
    causal = col[:, :CHUNK] <= row
    to_bias = lambda ok: jnp.asarray(np.where(ok, 0.0, -np.inf), F32)
    return to_bias(band), to_bias(causal)


def _prompt_attention(q, k, v):
    batch, _, seq, _ = q.shape
    n_blocks = seq // CHUNK
    assert all(n_blocks % dil == 0 for _, dil in DILATED_PATTERNS)
    spec = pl.BlockSpec((None, N_SLABS, seq, LANES), lambda b: (b, 0, 0, 0))
    band, causal = _score_biases()
    far_tiles = N_SLABS * DILATED_PATTERNS[2][1] // DILATED_PATTERNS[1][1]
    return pl.pallas_call(
        functools.partial(_prompt_attn_kernel, n_blocks=n_blocks),
        grid=(batch,),
        in_specs=[spec, spec, spec, _const_spec(band.shape), _const_spec(causal.shape)],
        out_specs=spec,
        out_shape=jax.ShapeDtypeStruct(q.shape, F32),
        scratch_shapes=[pltpu.VMEM((N_SLABS, seq, LANES), F32), pltpu.VMEM((N_SLABS, seq, LANES), F32),
                        pltpu.VMEM((N_SLABS, 2 * CHUNK, 2 * CHUNK), F32),
                        pltpu.VMEM((N_SLABS, 2 * CHUNK, 2 * CHUNK), BF16),
                        pltpu.VMEM((6, N_SLABS, seq // DILATED_PATTERNS[1][1], LANES), F32),
                        pltpu.VMEM((far_tiles, 2 * CHUNK, CHUNK), F32), pltpu.VMEM((far_tiles, 2 * CHUNK, CHUNK), BF16)],
        compiler_params=_params(("parallel",)),
        name="prompt_attention",
    )(q, k, v, band, causal)


def _sample_masks(seq, cache_len, past_len, n_rows):
    cache = np.zeros((n_rows, cache_len), np.float32)
    new = [[False] * n_rows for _ in range(seq)]
    for b, (window, dil) in enumerate(DILATED_PATTERNS):
        for t in range(seq):
            for jj in range(window // dil + 1):
                idx = cache_len + t - dil * jj
                if idx < 0 or idx + (past_len - cache_len) < 0:
                    continue
                if idx < cache_len:
                    cache[b * seq + t, idx] = 1.0
                else:
                    new[idx - cache_len][b * seq + t] = True
    return cache, new


UNIT_HEADS = 4
UNITS_PER_SEQ = N_HEADS // UNIT_HEADS


def _sample_attention_operands(q, k, v, cache_k, cache_v):
    n_seq, seq, _ = q.shape
    per_head = lambda a: jnp.pad(a.reshape(n_seq, seq, N_HEADS, HEAD_DIM).transpose(0, 2, 1, 3),
                                 ((0, 0), (0, 0), (0, 2 * seq - seq), (0, 0)))
    return (per_head(q), per_head(k), per_head(v),
            jnp.transpose(cache_k, (0, 2, 3, 1)), jnp.transpose(cache_v, (0, 2, 3, 1)))


def _sample_attention_call(operands, seq, past_len, first, count):
    q8, kn, vn, kt, vt = operands
    n_rows = 4 * seq
    assert len(DILATED_PATTERNS) * seq <= n_rows
    cache_len = kt.shape[3]
    mask, new_valid = _sample_masks(seq, cache_len, past_len, n_rows)
    at_unit = lambda i: ((first + i) // UNITS_PER_SEQ, (first + i) % UNITS_PER_SEQ, 0, 0)
    new_spec = pl.BlockSpec((None, UNIT_HEADS, 2 * seq, HEAD_DIM), at_unit)
    cache_spec = pl.BlockSpec((None, UNIT_HEADS, HEAD_DIM, cache_len), at_unit)
    return dict(
        stages=functools.partial(_sample_attn_stages, seq=seq, new_valid=new_valid),
        args=(q8, kn, vn, kt, vt, jnp.asarray(mask)),
        in_specs=[new_spec, new_spec, new_spec, cache_spec, cache_spec, _const_spec((n_rows, cache_len))],
        out_specs=[pl.BlockSpec((None, UNIT_HEADS, n_rows, HEAD_DIM), lambda i: (i, 0, 0, 0))],
        out_shapes=[jax.ShapeDtypeStruct((count, UNIT_HEADS, n_rows, HEAD_DIM), F32)],
        scratch=[pltpu.VMEM((UNIT_HEADS, n_rows, cache_len), F32), pltpu.VMEM((UNIT_HEADS, n_rows, cache_len), BF16)],
        steps=count)


def _sample_attention_slabs(unit_outs, seq):
    out = jnp.concatenate(unit_outs, axis=0)
    out = out.reshape(-1, N_HEADS, out.shape[2], HEAD_DIM)
    n_seq = out.shape[0]
    b_out = out[:, :, :seq, :].transpose(0, 2, 1, 3).reshape(n_seq * seq, N_SLABS, LANES)
    return b_out.transpose(1, 0, 2)[None]


def _finish_call(x2d, mix_a, b_slab, g_out_b, w_o, g_ffn, w_gate, w_up, w_down, *, tm):
    n, d_model = x2d.shape
    nt = b_slab.shape[2] // tm
    d_ff = w_gate.shape[1]
    return dict(
        stages=_finish_stages,
        args=(x2d, mix_a, b_slab, g_out_b[None], w_o.astype(BF16), g_ffn[None],
              w_gate.astype(BF16), w_up.astype(BF16), w_down.astype(BF16)),
        in_specs=[pl.BlockSpec((tm, d_model), lambda i: (i, 0)),
                  pl.BlockSpec((tm, WIDTH), lambda i: (i, 0)),
                  pl.BlockSpec((None, N_SLABS, tm, LANES), lambda i: (i // nt, 0, i % nt, 0)),
                  _const_spec((1, WIDTH)), _const_spec((2 * WIDTH, d_model)), _const_spec((1, d_model)),
                  _const_spec((d_model, d_ff)), _const_spec((d_model, d_ff)), _const_spec((d_ff, d_model))],
        out_specs=[pl.BlockSpec((tm, d_model), lambda i: (i, 0))],
        out_shapes=[jax.ShapeDtypeStruct((n, d_model), F32)],
        scratch=[], steps=n // tm)


def kernel(x_prompt, x_sample, cache_k, cache_v, g_attn, w_in, ln_v_g, ln_v_b, w_s, b_s, g_q, g_k,
           g_out_a, g_out_b, w_o, g_ffn, w_gate, w_up, w_down):
    depth = w_in.shape[0]
    assert depth == 1, "single-layer step"
    batch, seq, d_model = x_prompt.shape
    n_seq, dec_seq, _ = x_sample.shape
    l = 0
    shared = _shared_in_proj_args(g_attn[l], w_in[l], ln_v_g[l], ln_v_b[l], g_q[l], g_k[l])
    fin = (g_out_b[l], w_o[l], g_ffn[l], w_gate[l], w_up[l], w_down[l])
    tm = 256

    qs, ks, vs, vns, mix_as = _in_proj_sample(x_sample, shared, w_s[l], b_s[l], g_out_a[l])
    as3 = lambda a: a.reshape(n_seq, dec_seq, WIDTH)
    att_ops = _sample_attention_operands(as3(qs), as3(ks), as3(vs), cache_k[l], cache_v[l])
    steps = batch * seq // tm
    streams = n_seq * UNITS_PER_SEQ // steps
    assert streams * steps == n_seq * UNITS_PER_SEQ and streams >= 2
    unit_stream = lambda k: _sample_attention_call(att_ops, dec_seq, PAST_LEN, k * steps, steps)

    def hosted(host, hosted_streams, name):
        att = list(range(1, len(hosted_streams) + 1))
        outs = _staged_call([host] + hosted_streams, att + [0] + att + att + [0], name)
        return outs[0], [o[0] for o in outs[1:]]

    in_proj = _in_proj_prompt_call(x_prompt, shared, w_s[l], b_s[l], g_out_a[l], tm=tm)
    (q, k, v, k_out, v_out, mix_a), att_a = hosted(in_proj, [unit_stream(0)], "in_proj_prompt_and_sample_attention")
    b_slab = _prompt_attention(q, k, v)
    finish = _finish_call(x_prompt.reshape(batch * seq, d_model), mix_a, b_slab, *fin, tm=tm)
    (y_prompt,), att_b = hosted(finish, [unit_stream(k) for k in range(1, streams)], "finish_and_sample_attention")
    win = min(max(w for w, _ in DILATED_PATTERNS), seq)
    window = lambda a: a.reshape(batch, N_HEADS, HEAD_DIM, seq).transpose(0, 3, 1, 2)[:, -win:][None]

    bs_slab = _sample_attention_slabs(att_a + att_b, dec_seq)
    finish_s = _finish_call(x_sample.reshape(n_seq * dec_seq, d_model), mix_as, bs_slab, *fin, tm=tm)
    ((y_sample,),) = _staged_call([finish_s], [0, 0], "finish")

    head_shape = (1, n_seq, dec_seq, N_HEADS, HEAD_DIM)
    return (y_prompt.reshape(batch, seq, d_model), y_sample.reshape(n_seq, dec_seq, d_model),
            window(k_out), window(v_out), ks.reshape(head_shape), vs.reshape(head_shape),
            vns.reshape(1, n_seq, dec_seq, WIDTH))
```

```python
import functools

import numpy as np
import jax
import jax.numpy as jnp
from jax import lax
from jax.experimental import pallas as pl
from jax.experimental.pallas import tpu as pltpu

F32 = jnp.float32
BF16 = jnp.bfloat16

HEAD_DIM = 64
N_HEADS = 8
WIDTH = N_HEADS * HEAD_DIM
CHUNK = 128
DILATED_PATTERNS = ((128, 1), (512, 4), (2048, 16))
EPS = 1e-6
PAST_LEN = 8192
NEG_BIG = -1e30
LOG2_E = 1.4426950408889634

LANES = 128
N_SLABS = WIDTH // LANES
VMEM_LIMIT = 56 * 1024 * 1024


def _dot(a, b):
    return jnp.dot(a, b, preferred_element_type=F32)


def _dot_nt(a, b):
    return lax.dot_general(a, b, (((1,), (1,)), ((), ())), preferred_element_type=F32)


def _const_spec(shape):
    return pl.BlockSpec(shape, lambda *_: (0,) * len(shape), pipeline_mode=pl.Buffered(1))


def _head_rms(z, bd_ref, g_ref):
    msq = _dot((z * z).astype(BF16), bd_ref[...])
    return z * lax.rsqrt(msq + EPS) * g_ref[...]


def _features(x_ref, g_attn_ref, w_in_ref, ln_g_ref, ln_b_ref, gq_ref, gk_ref, bd_ref):
    x = x_ref[...]
    ms = jnp.mean(x * x, axis=-1, keepdims=True)
    xn = (x * lax.rsqrt(ms + EPS) * g_attn_ref[...]).astype(BF16)

    def proj(j):
        return _dot(xn, w_in_ref[:, j * WIDTH:(j + 1) * WIDTH])

    zu, zv, zq, zk, v = (proj(j) for j in range(5))
    u = jax.nn.gelu(zu)
    gv = jax.nn.gelu(zv)
    mu = jnp.mean(gv, axis=-1, keepdims=True)
    xc = gv - mu
    var = jnp.mean(xc * xc, axis=-1, keepdims=True)
    vn = xc * lax.rsqrt(var + EPS) * ln_g_ref[...] + ln_b_ref[...]
    q = _head_rms(zq, bd_ref, gq_ref)
    k = _head_rms(zk, bd_ref, gk_ref)
    return u, vn, q, k, v


def _group_norm_rows(slabs, g_ref):
    ssq = sum(jnp.sum(a * a, axis=-1, keepdims=True) for a in slabs)
    r = lax.rsqrt(ssq * (1.0 / WIDTH) + EPS)
    return [a * r * g_ref[:, s * LANES:(s + 1) * LANES] for s, a in enumerate(slabs)]


def _in_proj_prompt_stages(x_ref, g_attn_ref, w_in_ref, ln_g_ref, ln_b_ref, gq_ref, gk_ref, bd_ref,
                           ws_ref, bs_ref, g_out_a_ref,
                           q_ref, k_ref, v_ref, kout_ref, vout_ref, mixa_ref):
    u, vn, q, k, v = _features(x_ref, g_attn_ref, w_in_ref, ln_g_ref, ln_b_ref, gq_ref, gk_ref, bd_ref)
    yield
    tm = u.shape[0]
    first_head = lax.broadcasted_iota(jnp.int32, (CHUNK, LANES), 1) < HEAD_DIM
    for c in range(tm // CHUNK):
        rows = slice(c * CHUNK, (c + 1) * CHUNK)
        vn_c = vn[rows].astype(BF16)
        slabs = []
        for s in range(N_SLABS):
            cols = slice(s * LANES, (s + 1) * LANES)
            g0 = _dot(ws_ref[2 * s], vn_c[:, cols])
            g1 = _dot(ws_ref[2 * s + 1], vn_c[:, cols])
            gate = jnp.where(first_head, g0, g1) + bs_ref[:, cols]
            slabs.append(u[rows, cols] * gate)
        for s, a in enumerate(_group_norm_rows(slabs, g_out_a_ref)):
            mixa_ref[rows, s * LANES:(s + 1) * LANES] = a.astype(BF16)
    for s in range(N_SLABS):
        cols = slice(s * LANES, (s + 1) * LANES)
        q_ref[s] = q[:, cols]
        k_ref[s] = k[:, cols]
        v_ref[s] = v[:, cols]
    kout_ref[...] = k.T
    vout_ref[...] = v.T


def _in_proj_sample_kernel(x_ref, g_attn_ref, w_in_ref, ln_g_ref, ln_b_ref, gq_ref, gk_ref, bd_ref,
                           coef_ref, bias_ref, g_out_a_ref,
                           q_ref, k_ref, v_ref, vn_ref, mixa_ref, *, seq):
    u, vn, q, k, v = _features(x_ref, g_attn_ref, w_in_ref, ln_g_ref, ln_b_ref, gq_ref, gk_ref, bd_ref)
    tm = u.shape[0]
    sub = coef_ref.shape[1]
    gate = jnp.zeros((tm // sub, sub, WIDTH), F32) + bias_ref[...]
    for d in range(seq):
        shifted = vn if d == 0 else pltpu.roll(vn, d, axis=0)
        gate = gate + shifted.reshape(tm // sub, sub, WIDTH) * coef_ref[d]
    a = u * gate.reshape(tm, WIDTH)
    slabs = _group_norm_rows([a[:, s * LANES:(s + 1) * LANES] for s in range(N_SLABS)], g_out_a_ref)
    for s, a_s in enumerate(slabs):
        mixa_ref[:, s * LANES:(s + 1) * LANES] = a_s.astype(BF16)
    q_ref[...] = q
    k_ref[...] = k
    v_ref[...] = v
    vn_ref[...] = vn


def _prompt_attn_kernel(q_ref, k_ref, v_ref, band_ref, causal_ref, o_ref, m_sc, l_sc, s_sc, p_sc, grp_sc, sf_sc, pf_sc, *, n_blocks):
    (w_near, d_near), (w_mid, d_mid), (w_far, d_far) = DILATED_PATTERNS
    assert d_near == 1 and d_far == n_blocks and n_blocks % d_mid == 0
    assert w_near // d_near == w_mid // d_mid == w_far // d_far == CHUNK
    mid_blocks = n_blocks // d_mid
    sub = CHUNK // 2

    first_head = lax.broadcasted_iota(jnp.int32, (CHUNK, LANES), 1) < HEAD_DIM
    ones = jnp.ones((2 * CHUNK, LANES), BF16)

    both = lambda a: jnp.where(first_head, a[:CHUNK], a[CHUNK:])

    def keys_values(ref, s, cur, prev):
        a = ref[s, cur, :]
        return a if prev is None else jnp.concatenate([ref[s, prev, :], a], axis=0)

    def scores(s, cur, prev, src, s_tile):
        q_src, k_src, _ = src
        qb = q_src[s, cur, :].astype(BF16)
        zero = jnp.zeros_like(qb)
        lhs = jnp.concatenate([jnp.where(first_head, qb, zero), jnp.where(first_head, zero, qb)], axis=0)
        kk = keys_values(k_src, s, cur, prev).astype(BF16)
        s_tile[:, :kk.shape[0]] = _dot_nt(lhs, kk)

    def softmax(prev, has_prev, s_tile, p_tile):
        n_keys = CHUNK if prev is None else 2 * CHUNK
        m_parts = []
        for c in range(2 * CHUNK // sub):
            rows = slice(c * sub, (c + 1) * sub)
            bias = causal_ref[rows, :] if prev is None else band_ref[has_prev, rows, :]
            sc = s_tile[rows, :n_keys] + bias
            m = jnp.max(sc, axis=-1, keepdims=True)
            p_tile[rows, :n_keys] = jnp.exp2(sc - m).astype(BF16)
            m_parts.append(jnp.broadcast_to(m, (sub, LANES)))
        return both(jnp.concatenate(m_parts, axis=0))

    def values(s, cur, prev, src, p_tile):
        vv = keys_values(src[2], s, cur, prev).astype(BF16)
        n_keys = vv.shape[0]
        out = _dot(p_tile[:, :n_keys], jnp.concatenate([vv, ones[:n_keys]], axis=1))
        return both(out[:, LANES:]), both(out[:, :LANES])

    def attend(s, cur, prev, has_prev):
        src = (q_ref, k_ref, v_ref)
        scores(s, cur, prev, src, s_sc.at[s])
        m = softmax(prev, has_prev, s_sc.at[s], p_sc.at[s])
        return (m,) + values(s, cur, prev, src, p_sc.at[s])

    def merged(s, cur, stats):
        m_new, l_new, acc_new = stats
        m_old = m_sc[s, cur, :]
        m_tot = jnp.maximum(m_old, m_new)
        a_old = jnp.exp2(m_old - m_tot)
        a_new = jnp.exp2(m_new - m_tot)
        return (m_tot, a_old * l_sc[s, cur, :] + a_new * l_new, a_old * o_ref[s, cur, :] + a_new * acc_new)

    staged = [grp_sc.at[i] for i in range(6)]
    sub_classes = d_far // d_mid

    def far_group(r, carry):
        grp = pl.ds(r, o_ref.shape[1] // d_mid, stride=d_mid)
        for s in range(N_SLABS):
            for src, dst in zip((q_ref, k_ref, v_ref), staged[:3]):
                dst[s] = src[s, grp, :]
        blocks = [(c * N_SLABS + s, s, pl.ds(c, CHUNK, stride=sub_classes))
                  for c in range(sub_classes) for s in range(N_SLABS)]
        st_m, st_l, st_acc = staged[3:]
        for tile, s, cur in blocks:
            scores(s, cur, None, staged[:3], sf_sc.at[tile])
        for tile, s, cur in blocks:
            st_m[s, cur, :] = softmax(None, None, sf_sc.at[tile], pf_sc.at[tile])
        for tile, s, cur in blocks:
            st_l[s, cur, :], st_acc[s, cur, :] = values(s, cur, None, staged[:3], pf_sc.at[tile])
        for s in range(N_SLABS):
            for dst, src in zip((m_sc, l_sc, o_ref), staged[3:]):
                dst[s, grp, :] = src[s]
        return carry

    lax.fori_loop(0, d_mid, far_group, 0)

    def span(jb, carry):
        for r in range(d_mid):
            cur = pl.ds(r + (d_mid * CHUNK) * jb, CHUNK, stride=d_mid)
            prev = pl.ds(r + (d_mid * CHUNK) * jnp.maximum(jb - 1, 0), CHUNK, stride=d_mid)
            for s in range(N_SLABS):
                m_sc[s, cur, :], l_sc[s, cur, :], o_ref[s, cur, :] = merged(s, cur, attend(s, cur, prev, jnp.minimum(jb, 1)))
        for i in range(d_mid):
            j = jb * d_mid + i
            cur = pl.ds(pl.multiple_of(j * CHUNK, CHUNK), CHUNK)
            prev = pl.ds(pl.multiple_of(jnp.maximum(j - 1, 0) * CHUNK, CHUNK), CHUNK)
            for s in range(N_SLABS):
                _, l_tot, acc_tot = merged(s, cur, attend(s, cur, prev, jnp.minimum(j, 1)))
                o_ref[s, cur, :] = acc_tot / l_tot
        return carry

    lax.fori_loop(0, mid_blocks, span, 0)


def _sample_attn_stages(q_ref, kn_ref, vn_ref, kt_ref, vt_ref, mask_ref, o_ref, s_sc, p_sc, *, seq, new_valid):
    n_rows = mask_ref.shape[0]
    n_real = len(DILATED_PATTERNS) * seq
    assert q_ref.shape[1] == 2 * seq and n_rows == 4 * seq

    def branch_rows(a):
        return jnp.concatenate([a + pltpu.roll(a, seq, axis=0), a], axis=0)

    row = lax.broadcasted_iota(jnp.int32, (n_rows, 1), 0)
    real = row < n_real
    valid = mask_ref[...] > 0.5
    new_rows = []
    for tp in range(seq):
        sel = functools.reduce(jnp.logical_or, [row == r for r in range(n_rows) if new_valid[tp][r]])
        new_rows.append(sel)

    heads = range(q_ref.shape[0])
    for h in heads:
        sc = _dot(q_ref[h].astype(BF16), kt_ref[h].astype(BF16))
        s_sc[h] = jnp.where(valid, branch_rows(sc), -jnp.inf)
    yield

    weights = []
    for h in heads:
        q = branch_rows(q_ref[h])
        kn = kn_ref[h]
        s_new = [jnp.where(new_rows[tp], jnp.sum(q * kn[tp:tp + 1, :], axis=-1, keepdims=True), -jnp.inf)
                 for tp in range(seq)]
        sc = s_sc[h]
        m = jnp.max(sc, axis=-1, keepdims=True)
        for sn in s_new:
            m = jnp.maximum(m, sn)
        m = jnp.where(real, m, 0.0)
        p = jnp.exp2(sc - m)
        p_sc[h] = p.astype(BF16)
        p_new = [jnp.exp2(sn - m) for sn in s_new]
        l = jnp.sum(p, axis=-1, keepdims=True) + sum(p_new)
        m_eff = jnp.where(real, m, NEG_BIG)
        l_eff = jnp.where(real, l, 0.0)
        m_all = m_eff
        for i in range(1, n_rows // seq):
            m_all = jnp.maximum(m_all, pltpu.roll(m_eff, i * seq, axis=0))
        c = jnp.exp2(m_eff - m_all)
        cl = c * l_eff
        den = cl
        for i in range(1, n_rows // seq):
            den = den + pltpu.roll(cl, i * seq, axis=0)
        w = jnp.where(real, c / den, 0.0)
        weights.append((w, [pn * w for pn in p_new]))
    yield

    for h in heads:
        w, pw_new = weights[h]
        vn = vn_ref[h]
        out = _dot_nt(p_sc[h], vt_ref[h].astype(BF16)) * w
        for tp in range(seq):
            out = out + pw_new[tp] * vn[tp:tp + 1, :]
        tot = out
        for i in range(1, n_rows // seq):
            tot = tot + pltpu.roll(out, i * seq, axis=0)
        o_ref[h] = tot


def _finish_stages(x_ref, mixa_ref, b_ref, g_out_b_ref, wo_ref, g_ffn_ref, wg_ref, wu_ref, wd_ref, o_ref):
    slabs = _group_norm_rows([b_ref[s] for s in range(N_SLABS)], g_out_b_ref)
    mix_b = jnp.concatenate([a.astype(BF16) for a in slabs], axis=-1)
    x1 = x_ref[...] + _dot(mixa_ref[...], wo_ref[:WIDTH, :]) + _dot(mix_b, wo_ref[WIDTH:, :])
    ms = jnp.mean(x1 * x1, axis=-1, keepdims=True)
    h = (x1 * lax.rsqrt(ms + EPS) * g_ffn_ref[...]).astype(BF16)
    act = (jax.nn.silu(_dot(h, wg_ref[...])) * _dot(h, wu_ref[...])).astype(BF16)
    yield
    o_ref[...] = x1 + _dot(act, wd_ref[...])


def _params(sem):
    return pltpu.CompilerParams(dimension_semantics=sem, vmem_limit_bytes=VMEM_LIMIT)


_DONE = object()


def _staged_call(calls, order, name):
    steps = calls[0]["steps"]
    assert all(c["steps"] == steps for c in calls)
    counts = [(len(c["args"]), len(c["out_specs"]), len(c["scratch"])) for c in calls]

    def body(*refs):
        groups = []
        pos = 0
        for kind in range(3):
            per_call = []
            for cnt in counts:
                per_call.append(refs[pos:pos + cnt[kind]])
                pos += cnt[kind]
            groups.append(per_call)
        gens = [c["stages"](*groups[0][i], *groups[1][i], *groups[2][i]) for i, c in enumerate(calls)]
        for i in order:
            next(gens[i], None)
        assert all(next(g, _DONE) is _DONE for g in gens), "order leaves stages untraced"

    cat = lambda key: [item for c in calls for item in c[key]]
    outs = pl.pallas_call(
        body, grid=(steps,), in_specs=cat("in_specs"), out_specs=cat("out_specs"), out_shape=cat("out_shapes"),
        scratch_shapes=cat("scratch"), compiler_params=_params(("parallel",)), name=name,
    )(*cat("args"))
    split, pos = [], 0
    for _, n_out, _ in counts:
        split.append(outs[pos:pos + n_out])
        pos += n_out
    return split


def _shared_in_proj_args(g_attn, w_in, ln_v_g, ln_v_b, g_q, g_k):
    d_model, in_width = w_in.shape
    gq = (jnp.tile(g_q, N_HEADS) * (HEAD_DIM ** -0.5 * LOG2_E))[None]
    gk = jnp.tile(g_k, N_HEADS)[None]
    bd = jnp.asarray(np.kron(np.eye(N_HEADS), np.full((HEAD_DIM, HEAD_DIM), 1.0 / HEAD_DIM)), BF16)
    args = (g_attn[None], w_in.astype(BF16), ln_v_g[None], ln_v_b[None], gq, gk, bd)
    specs = [_const_spec((1, d_model)), _const_spec((d_model, in_width)), _const_spec((1, WIDTH)),
             _const_spec((1, WIDTH)), _const_spec((1, WIDTH)), _const_spec((1, WIDTH)),
             _const_spec((WIDTH, WIDTH))]
    return args, specs


def _in_proj_prompt_call(x, shared, w_s, b_s, g_out_a, *, tm):
    batch, seq, d_model = x.shape
    n = batch * seq
    nt = seq // tm
    shared_args, shared_specs = shared
    tri = np.tril(np.ones((CHUNK, CHUNK), bool))
    ws = jnp.where(tri, w_s, 0.0).astype(BF16)
    bs = jnp.repeat(b_s.T, HEAD_DIM, axis=1)
    slab = jax.ShapeDtypeStruct((batch, N_SLABS, seq, LANES), F32)
    flat = jax.ShapeDtypeStruct((batch, WIDTH, seq), F32)
    slab_spec = pl.BlockSpec((None, N_SLABS, tm, LANES), lambda i: (i // nt, 0, i % nt, 0))
    flat_spec = pl.BlockSpec((None, WIDTH, tm), lambda i: (i // nt, 0, i % nt))
    row_spec = pl.BlockSpec((tm, WIDTH), lambda i: (i, 0))
    return dict(
        stages=_in_proj_prompt_stages,
        args=(x.reshape(n, d_model), *shared_args, ws, bs, g_out_a[None]),
        in_specs=[pl.BlockSpec((tm, d_model), lambda i: (i, 0))] + shared_specs + [
            _const_spec((N_HEADS, CHUNK, CHUNK)), _const_spec((CHUNK, WIDTH)), _const_spec((1, WIDTH))],
        out_specs=[slab_spec, slab_spec, slab_spec, flat_spec, flat_spec, row_spec],
        out_shapes=[slab, slab, slab, flat, flat, jax.ShapeDtypeStruct((n, WIDTH), BF16)],
        scratch=[], steps=n // tm)


def _in_proj_sample(x, shared, w_s, b_s, g_out_a):
    n_seq, seq, d_model = x.shape
    n = n_seq * seq
    shared_args, shared_specs = shared
    sub = 8
    assert sub % seq == 0 and seq <= CHUNK
    step = np.arange(sub) % seq
    corner = w_s[:, :seq, :seq]
    taps = []
    for d in range(seq):
        src = np.maximum(step - d, 0)
        tap = jnp.where((step >= d)[:, None], corner[:, step, src].T, 0.0)
        taps.append(jnp.repeat(tap, HEAD_DIM, axis=1))
    coef = jnp.stack(taps)
    bias = jnp.repeat(b_s[:, step].T, HEAD_DIM, axis=1)
    flat = jax.ShapeDtypeStruct((n, WIDTH), F32)
    row_spec = pl.BlockSpec((n, WIDTH), lambda i: (0, 0))
    return pl.pallas_call(
        functools.partial(_in_proj_sample_kernel, seq=seq),
        grid=(1,),
        in_specs=[pl.BlockSpec((n, d_model), lambda i: (0, 0))] + shared_specs + [
            _const_spec((seq, sub, WIDTH)), _const_spec((sub, WIDTH)), _const_spec((1, WIDTH))],
        out_specs=[row_spec] * 5,
        out_shape=[flat, flat, flat, flat, jax.ShapeDtypeStruct((n, WIDTH), BF16)],
        compiler_params=_params(("arbitrary",)),
        name="in_proj_sample",
    )(x.reshape(n, d_model), *shared_args, coef, bias, g_out_a[None])


def _score_biases():
    row = (np.arange(2 * CHUNK) % CHUNK)[:, None]
    col = np.arange(2 * CHUNK)[None, :]
    in_band = (col >= row) & (col <= row + CHUNK)
    band = np.stack([in_band & (col >= CHUNK), in_band])
    causal = col[:, :CHUNK] <= row
    to_bias = lambda ok: jnp.asarray(np.where(ok, 0.0, -np.inf), F32)
    return to_bias(band), to_bias(causal)


def _prompt_attention(q, k, v):
    batch, _, seq, _ = q.shape
    n_blocks = seq // CHUNK
    assert all(n_blocks % dil == 0 for _, dil in DILATED_PATTERNS)
    spec = pl.BlockSpec((None, N_SLABS, seq, LANES), lambda b: (b, 0, 0, 0))
    band, causal = _score_biases()
    far_tiles = N_SLABS * DILATED_PATTERNS[2][1] // DILATED_PATTERNS[1][1]
    return pl.pallas_call(
        functools.partial(_prompt_attn_kernel, n_blocks=n_blocks),
        grid=(batch,),
        in_specs=[spec, spec, spec, _const_spec(band.shape), _const_spec(causal.shape)],
        out_specs=spec,
        out_shape=jax.ShapeDtypeStruct(q.shape, F32),
        scratch_shapes=[pltpu.VMEM((N_SLABS, seq, LANES), F32), pltpu.VMEM((N_SLABS, seq, LANES), F32),
                        pltpu.VMEM((N_SLABS, 2 * CHUNK, 2 * CHUNK), F32),
                        pltpu.VMEM((N_SLABS, 2 * CHUNK, 2 * CHUNK), BF16),
                        pltpu.VMEM((6, N_SLABS, seq // DILATED_PATTERNS[1][1], LANES), F32),
                        pltpu.VMEM((far_tiles, 2 * CHUNK, CHUNK), F32), pltpu.VMEM((far_tiles, 2 * CHUNK, CHUNK), BF16)],
        compiler_params=_params(("parallel",)),
        name="prompt_attention",
    )(q, k, v, band, causal)


def _sample_masks(seq, cache_len, past_len, n_rows):
    cache = np.zeros((n_rows, cache_len), np.float32)
    new = [[False] * n_rows for _ in range(seq)]
    for b, (window, dil) in enumerate(DILATED_PATTERNS):
        for t in range(seq):
            for jj in range(window // dil + 1):
                idx = cache_len + t - dil * jj
                if idx < 0 or idx + (past_len - cache_len) < 0:
                    continue
                if idx < cache_len:
                    cache[b * seq + t, idx] = 1.0
                else:
                    new[idx - cache_len][b * seq + t] = True
    return cache, new


UNIT_HEADS = 4
UNITS_PER_SEQ = N_HEADS // UNIT_HEADS


def _sample_attention_operands(q, k, v, cache_k, cache_v):
    n_seq, seq, _ = q.shape
    per_head = lambda a: jnp.pad(a.reshape(n_seq, seq, N_HEADS, HEAD_DIM).transpose(0, 2, 1, 3),
                                 ((0, 0), (0, 0), (0, 2 * seq - seq), (0, 0)))
    return (per_head(q), per_head(k), per_head(v),
            jnp.transpose(cache_k, (0, 2, 3, 1)), jnp.transpose(cache_v, (0, 2, 3, 1)))


def _sample_attention_call(operands, seq, past_len, first, count):
    q8, kn, vn, kt, vt = operands
    n_rows = 4 * seq
    assert len(DILATED_PATTERNS) * seq <= n_rows
    cache_len = kt.shape[3]
    mask, new_valid = _sample_masks(seq, cache_len, past_len, n_rows)
    at_unit = lambda i: ((first + i) // UNITS_PER_SEQ, (first + i) % UNITS_PER_SEQ, 0, 0)
    new_spec = pl.BlockSpec((None, UNIT_HEADS, 2 * seq, HEAD_DIM), at_unit)
    cache_spec = pl.BlockSpec((None, UNIT_HEADS, HEAD_DIM, cache_len), at_unit)
    return dict(
        stages=functools.partial(_sample_attn_stages, seq=seq, new_valid=new_valid),
        args=(q8, kn, vn, kt, vt, jnp.asarray(mask)),
        in_specs=[new_spec, new_spec, new_spec, cache_spec, cache_spec, _const_spec((n_rows, cache_len))],
        out_specs=[pl.BlockSpec((None, UNIT_HEADS, n_rows, HEAD_DIM), lambda i: (i, 0, 0, 0))],
        out_shapes=[jax.ShapeDtypeStruct((count, UNIT_HEADS, n_rows, HEAD_DIM), F32)],
        scratch=[pltpu.VMEM((UNIT_HEADS, n_rows, cache_len), F32), pltpu.VMEM((UNIT_HEADS, n_rows, cache_len), BF16)],
        steps=count)


def _sample_attention_slabs(unit_outs, seq):
    out = jnp.concatenate(unit_outs, axis=0)
    out = out.reshape(-1, N_HEADS, out.shape[2], HEAD_DIM)
    n_seq = out.shape[0]
    b_out = out[:, :, :seq, :].transpose(0, 2, 1, 3).reshape(n_seq * seq, N_SLABS, LANES)
    return b_out.transpose(1, 0, 2)[None]


def _finish_call(x2d, mix_a, b_slab, g_out_b, w_o, g_ffn, w_gate, w_up, w_down, *, tm):
    n, d_model = x2d.shape
    nt = b_slab.shape[2] // tm
    d_ff = w_gate.shape[1]
    return dict(
        stages=_finish_stages,
        args=(x2d, mix_a, b_slab, g_out_b[None], w_o.astype(BF16), g_ffn[None],
              w_gate.astype(BF16), w_up.astype(BF16), w_down.astype(BF16)),
        in_specs=[pl.BlockSpec((tm, d_model), lambda i: (i, 0)),
                  pl.BlockSpec((tm, WIDTH), lambda i: (i, 0)),
                  pl.BlockSpec((None, N_SLABS, tm, LANES), lambda i: (i // nt, 0, i % nt, 0)),
                  _const_spec((1, WIDTH)), _const_spec((2 * WIDTH, d_model)), _const_spec((1, d_model)),
                  _const_spec((d_model, d_ff)), _const_spec((d_model, d_ff)), _const_spec((d_ff, d_model))],
        out_specs=[pl.BlockSpec((tm, d_model), lambda i: (i, 0))],
        out_shapes=[jax.ShapeDtypeStruct((n, d_model), F32)],
        scratch=[], steps=n // tm)


def kernel(x_prompt, x_sample, cache_k, cache_v, g_attn, w_in, ln_v_g, ln_v_b, w_s, b_s, g_q, g_k,
           g_out_a, g_out_b, w_o, g_ffn, w_gate, w_up, w_down):
    depth = w_in.shape[0]
    assert depth == 1, "single-layer step"
    batch, seq, d_model = x_prompt.shape
    n_seq, dec_seq, _ = x_sample.shape
    l = 0
    shared = _shared_in_proj_args(g_attn[l], w_in[l], ln_v_g[l], ln_v_b[l], g_q[l], g_k[l])
    fin = (g_out_b[l], w_o[l], g_ffn[l], w_gate[l], w_up[l], w_down[l])
    tm = 256

    qs, ks, vs, vns, mix_as = _in_proj_sample(x_sample, shared, w_s[l], b_s[l], g_out_a[l])
    as3 = lambda a: a.reshape(n_seq, dec_seq, WIDTH)
    att_ops = _sample_attention_operands(as3(qs), as3(ks), as3(vs), cache_k[l], cache_v[l])
    steps = batch * seq // tm
    streams = n_seq * UNITS_PER_SEQ // steps
    assert streams * steps == n_seq * UNITS_PER_SEQ and streams >= 2
    unit_stream = lambda k: _sample_attention_call(att_ops, dec_seq, PAST_LEN, k * steps, steps)

    def hosted(host, hosted_streams, name):
        att = list(range(1, len(hosted_streams) + 1))
        outs = _staged_call([host] + hosted_streams, att + [0] + att + att + [0], name)
        return outs[0], [o[0] for o in outs[1:]]

    in_proj = _in_proj_prompt_call(x_prompt, shared, w_s[l], b_s[l], g_out_a[l], tm=tm)
    (q, k, v, k_out, v_out, mix_a), att_a = hosted(in_proj, [unit_stream(0)], "in_proj_prompt_and_sample_attention")
    b_slab = _prompt_attention(q, k, v)
    finish = _finish_call(x_prompt.reshape(batch * seq, d_model), mix_a, b_slab, *fin, tm=tm)
    (y_prompt,), att_b = hosted(finish, [unit_stream(k) for k in range(1, streams)], "finish_and_sample_attention")
    win = min(max(w for w, _ in DILATED_PATTERNS), seq)
    window = lambda a: a.reshape(batch, N_HEADS, HEAD_DIM, seq).transpose(0, 3, 1, 2)[:, -win:][None]

    bs_slab = _sample_attention_slabs(att_a + att_b, dec_seq)
    finish_s = _finish_call(x_sample.reshape(n_seq * dec_seq, d_model), mix_as, bs_slab, *fin, tm=n_seq * dec_seq)
    ((y_sample,),) = _staged_call([finish_s], [0, 0], "finish")

    head_shape = (1, n_seq, dec_seq, N_HEADS, HEAD_DIM)
    return (y_prompt.reshape(batch, seq, d_model), y_sample.reshape(n_seq, dec_seq, d_model),
            window(k_out), window(v_out), ks.reshape(head_shape), vs.reshape(head_shape),
            vns.reshape(1, n_seq, dec_seq, WIDTH))
```

```python
import functools

import numpy as np
import jax
import jax.numpy as jnp
from jax import lax
from jax.experimental import pallas as pl
from jax.experimental.pallas import tpu as pltpu

F32 = jnp.float32
BF16 = jnp.bfloat16

HEAD_DIM = 64
N_HEADS = 8
WIDTH = N_HEADS * HEAD_DIM
CHUNK = 128
DILATED_PATTERNS = ((128, 1), (512, 4), (2048, 16))
EPS = 1e-6
PAST_LEN = 8192
NEG_BIG = -1e30
LOG2_E = 1.4426950408889634

LANES = 128
N_SLABS = WIDTH // LANES
VMEM_LIMIT = 56 * 1024 * 1024


def _dot(a, b):
    return jnp.dot(a, b, preferred_element_type=F32)


def _dot_nt(a, b):
    return lax.dot_general(a, b, (((1,), (1,)), ((), ())), preferred_element_type=F32)


def _const_spec(shape):
    return pl.BlockSpec(shape, lambda *_: (0,) * len(shape), pipeline_mode=pl.Buffered(1))


def _head_rms(z, bd_ref, g_ref):
    msq = _dot((z * z).astype(BF16), bd_ref[...])
    return z * lax.rsqrt(msq + EPS) * g_ref[...]


def _features(x_ref, g_attn_ref, w_in_ref, ln_g_ref, ln_b_ref, gq_ref, gk_ref, bd_ref):
    x = x_ref[...]
    ms = jnp.mean(x * x, axis=-1, keepdims=True)
    xn = (x * lax.rsqrt(ms + EPS) * g_attn_ref[...]).astype(BF16)

    def proj(j):
        return _dot(xn, w_in_ref[:, j * WIDTH:(j + 1) * WIDTH])

    zu, zv, zq, zk, v = (proj(j) for j in range(5))
    u = jax.nn.gelu(zu)
    gv = jax.nn.gelu(zv)
    mu = jnp.mean(gv, axis=-1, keepdims=True)
    xc = gv - mu
    var = jnp.mean(xc * xc, axis=-1, keepdims=True)
    vn = xc * lax.rsqrt(var + EPS) * ln_g_ref[...] + ln_b_ref[...]
    q = _head_rms(zq, bd_ref, gq_ref)
    k = _head_rms(zk, bd_ref, gk_ref)
    return u, vn, q, k, v


def _group_norm_rows(slabs, g_ref):
    ssq = sum(jnp.sum(a * a, axis=-1, keepdims=True) for a in slabs)
    r = lax.rsqrt(ssq * (1.0 / WIDTH) + EPS)
    return [a * r * g_ref[:, s * LANES:(s + 1) * LANES] for s, a in enumerate(slabs)]


def _in_proj_prompt_stages(x_ref, g_attn_ref, w_in_ref, ln_g_ref, ln_b_ref, gq_ref, gk_ref, bd_ref,
                           ws_ref, bs_ref, g_out_a_ref,
                           q_ref, k_ref, v_ref, kout_ref, vout_ref, mixa_ref):
    u, vn, q, k, v = _features(x_ref, g_attn_ref, w_in_ref, ln_g_ref, ln_b_ref, gq_ref, gk_ref, bd_ref)
    yield
    tm = u.shape[0]
    first_head = lax.broadcasted_iota(jnp.int32, (CHUNK, LANES), 1) < HEAD_DIM
    for c in range(tm // CHUNK):
        rows = slice(c * CHUNK, (c + 1) * CHUNK)
        vn_c = vn[rows].astype(BF16)
        slabs = []
        for s in range(N_SLABS):
            cols = slice(s * LANES, (s + 1) * LANES)
            g0 = _dot(ws_ref[2 * s], vn_c[:, cols])
            g1 = _dot(ws_ref[2 * s + 1], vn_c[:, cols])
            gate = jnp.where(first_head, g0, g1) + bs_ref[:, cols]
            slabs.append(u[rows, cols] * gate)
        for s, a in enumerate(_group_norm_rows(slabs, g_out_a_ref)):
            mixa_ref[rows, s * LANES:(s + 1) * LANES] = a.astype(BF16)
    for s in range(N_SLABS):
        cols = slice(s * LANES, (s + 1) * LANES)
        q_ref[s] = q[:, cols]
        k_ref[s] = k[:, cols]
        v_ref[s] = v[:, cols]
    kout_ref[...] = k.T
    vout_ref[...] = v.T


def _in_proj_sample_kernel(x_ref, g_attn_ref, w_in_ref, ln_g_ref, ln_b_ref, gq_ref, gk_ref, bd_ref,
                           coef_ref, bias_ref, g_out_a_ref,
                           q_ref, k_ref, v_ref, vn_ref, mixa_ref, *, seq):
    u, vn, q, k, v = _features(x_ref, g_attn_ref, w_in_ref, ln_g_ref, ln_b_ref, gq_ref, gk_ref, bd_ref)
    tm = u.shape[0]
    sub = coef_ref.shape[1]
    gate = jnp.zeros((tm // sub, sub, WIDTH), F32) + bias_ref[...]
    for d in range(seq):
        shifted = vn if d == 0 else pltpu.roll(vn, d, axis=0)
        gate = gate + shifted.reshape(tm // sub, sub, WIDTH) * coef_ref[d]
    a = u * gate.reshape(tm, WIDTH)
    slabs = _group_norm_rows([a[:, s * LANES:(s + 1) * LANES] for s in range(N_SLABS)], g_out_a_ref)
    for s, a_s in enumerate(slabs):
        mixa_ref[:, s * LANES:(s + 1) * LANES] = a_s.astype(BF16)
    q_ref[...] = q
    k_ref[...] = k
    v_ref[...] = v
    vn_ref[...] = vn


def _prompt_attn_kernel(q_ref, k_ref, v_ref, band_ref, causal_ref, o_ref, m_sc, l_sc, s_sc, p_sc, grp_sc, sf_sc, pf_sc, *, n_blocks):
    (w_near, d_near), (w_mid, d_mid), (w_far, d_far) = DILATED_PATTERNS
    assert d_near == 1 and d_far == n_blocks and n_blocks % d_mid == 0
    assert w_near // d_near == w_mid // d_mid == w_far // d_far == CHUNK
    mid_blocks = n_blocks // d_mid
    sub = CHUNK // 2

    first_head = lax.broadcasted_iota(jnp.int32, (CHUNK, LANES), 1) < HEAD_DIM
    ones = jnp.ones((2 * CHUNK, LANES), BF16)

    both = lambda a: jnp.where(first_head, a[:CHUNK], a[CHUNK:])

    def keys_values(ref, s, cur, prev):
        a = ref[s, cur, :]
        return a if prev is None else jnp.concatenate([ref[s, prev, :], a], axis=0)

    def scores(s, cur, prev, src, s_tile):
        q_src, k_src, _ = src
        qb = q_src[s, cur, :].astype(BF16)
        zero = jnp.zeros_like(qb)
        lhs = jnp.concatenate([jnp.where(first_head, qb, zero), jnp.where(first_head, zero, qb)], axis=0)
        kk = keys_values(k_src, s, cur, prev).astype(BF16)
        s_tile[:, :kk.shape[0]] = _dot_nt(lhs, kk)

    def softmax(prev, has_prev, s_tile, p_tile):
        n_keys = CHUNK if prev is None else 2 * CHUNK
        m_parts = []
        for c in range(2 * CHUNK // sub):
            rows = slice(c * sub, (c + 1) * sub)
            bias = causal_ref[rows, :] if prev is None else band_ref[has_prev, rows, :]
            sc = s_tile[rows, :n_keys] + bias
            m = jnp.max(sc, axis=-1, keepdims=True)
            p_tile[rows, :n_keys] = jnp.exp2(sc - m).astype(BF16)
            m_parts.append(jnp.broadcast_to(m, (sub, LANES)))
        return both(jnp.concatenate(m_parts, axis=0))

    def values(s, cur, prev, src, p_tile):
        vv = keys_values(src[2], s, cur, prev).astype(BF16)
        n_keys = vv.shape[0]
        out = _dot(p_tile[:, :n_keys], jnp.concatenate([vv, ones[:n_keys]], axis=1))
        return both(out[:, LANES:]), both(out[:, :LANES])

    def attend(s, cur, prev, has_prev):
        src = (q_ref, k_ref, v_ref)
        scores(s, cur, prev, src, s_sc.at[s])
        m = softmax(prev, has_prev, s_sc.at[s], p_sc.at[s])
        return (m,) + values(s, cur, prev, src, p_sc.at[s])

    def merged(s, cur, stats):
        m_new, l_new, acc_new = stats
        m_old = m_sc[s, cur, :]
        m_tot = jnp.maximum(m_old, m_new)
        a_old = jnp.exp2(m_old - m_tot)
        a_new = jnp.exp2(m_new - m_tot)
        return (m_tot, a_old * l_sc[s, cur, :] + a_new * l_new, a_old * o_ref[s, cur, :] + a_new * acc_new)

    staged = [grp_sc.at[i] for i in range(6)]
    sub_classes = d_far // d_mid

    def far_group(r, carry):
        grp = pl.ds(r, o_ref.shape[1] // d_mid, stride=d_mid)
        for s in range(N_SLABS):
            for src, dst in zip((q_ref, k_ref, v_ref), staged[:3]):
                dst[s] = src[s, grp, :]
        blocks = [(c * N_SLABS + s, s, pl.ds(c, CHUNK, stride=sub_classes))
                  for c in range(sub_classes) for s in range(N_SLABS)]
        st_m, st_l, st_acc = staged[3:]
        for tile, s, cur in blocks:
            scores(s, cur, None, staged[:3], sf_sc.at[tile])
        for tile, s, cur in blocks:
            st_m[s, cur, :] = softmax(None, None, sf_sc.at[tile], pf_sc.at[tile])
        for tile, s, cur in blocks:
            st_l[s, cur, :], st_acc[s, cur, :] = values(s, cur, None, staged[:3], pf_sc.at[tile])
        for s in range(N_SLABS):
            for dst, src in zip((m_sc, l_sc, o_ref), staged[3:]):
                dst[s, grp, :] = src[s]
        return carry

    lax.fori_loop(0, d_mid, far_group, 0)

    def span(jb, carry):
        for r in range(d_mid):
            cur = pl.ds(r + (d_mid * CHUNK) * jb, CHUNK, stride=d_mid)
            prev = pl.ds(r + (d_mid * CHUNK) * jnp.maximum(jb - 1, 0), CHUNK, stride=d_mid)
            for s in range(N_SLABS):
                m_sc[s, cur, :], l_sc[s, cur, :], o_ref[s, cur, :] = merged(s, cur, attend(s, cur, prev, jnp.minimum(jb, 1)))
        for i in range(d_mid):
            j = jb * d_mid + i
            cur = pl.ds(pl.multiple_of(j * CHUNK, CHUNK), CHUNK)
            prev = pl.ds(pl.multiple_of(jnp.maximum(j - 1, 0) * CHUNK, CHUNK), CHUNK)
            for s in range(N_SLABS):
                _, l_tot, acc_tot = merged(s, cur, attend(s, cur, prev, jnp.minimum(j, 1)))
                o_ref[s, cur, :] = acc_tot / l_tot
        return carry

    lax.fori_loop(0, mid_blocks, span, 0)


def _sample_attn_stages(q_ref, kn_ref, vn_ref, kt_ref, vt_ref, mask_ref, o_ref, s_sc, p_sc, *, seq, new_valid):
    n_rows = mask_ref.shape[0]
    n_real = len(DILATED_PATTERNS) * seq
    assert q_ref.shape[1] == 2 * seq and n_rows == 4 * seq

    def branch_rows(a):
        return jnp.concatenate([a + pltpu.roll(a, seq, axis=0), a], axis=0)

    row = lax.broadcasted_iota(jnp.int32, (n_rows, 1), 0)
    real = row < n_real
    valid = mask_ref[...] > 0.5
    new_rows = []
    for tp in range(seq):
        sel = functools.reduce(jnp.logical_or, [row == r for r in range(n_rows) if new_valid[tp][r]])
        new_rows.append(sel)

    heads = range(q_ref.shape[0])
    for h in heads:
        sc = _dot(q_ref[h].astype(BF16), kt_ref[h].astype(BF16))
        s_sc[h] = jnp.where(valid, branch_rows(sc), -jnp.inf)
    yield

    weights = []
    for h in heads:
        q = branch_rows(q_ref[h])
        kn = kn_ref[h]
        s_new = [jnp.where(new_rows[tp], jnp.sum(q * kn[tp:tp + 1, :], axis=-1, keepdims=True), -jnp.inf)
                 for tp in range(seq)]
        sc = s_sc[h]
        m = jnp.max(sc, axis=-1, keepdims=True)
        for sn in s_new:
            m = jnp.maximum(m, sn)
        m = jnp.where(real, m, 0.0)
        p = jnp.exp2(sc - m)
        p_sc[h] = p.astype(BF16)
        p_new = [jnp.exp2(sn - m) for sn in s_new]
        l = jnp.sum(p, axis=-1, keepdims=True) + sum(p_new)
        m_eff = jnp.where(real, m, NEG_BIG)
        l_eff = jnp.where(real, l, 0.0)
        m_all = m_eff
        for i in range(1, n_rows // seq):
            m_all = jnp.maximum(m_all, pltpu.roll(m_eff, i * seq, axis=0))
        c = jnp.exp2(m_eff - m_all)
        cl = c * l_eff
        den = cl
        for i in range(1, n_rows // seq):
            den = den + pltpu.roll(cl, i * seq, axis=0)
        w = jnp.where(real, c / den, 0.0)
        weights.append((w, [pn * w for pn in p_new]))
    yield

    for h in heads:
        w, pw_new = weights[h]
        vn = vn_ref[h]
        out = _dot_nt(p_sc[h], vt_ref[h].astype(BF16)) * w
        for tp in range(seq):
            out = out + pw_new[tp] * vn[tp:tp + 1, :]
        tot = out
        for i in range(1, n_rows // seq):
            tot = tot + pltpu.roll(out, i * seq, axis=0)
        o_ref[h] = tot


def _finish_stages(x_ref, mixa_ref, b_ref, g_out_b_ref, wo_ref, g_ffn_ref, wg_ref, wu_ref, wd_ref, o_ref):
    slabs = _group_norm_rows([b_ref[s] for s in range(N_SLABS)], g_out_b_ref)
    mix_b = jnp.concatenate([a.astype(BF16) for a in slabs], axis=-1)
    x1 = x_ref[...] + _dot(mixa_ref[...], wo_ref[:WIDTH, :]) + _dot(mix_b, wo_ref[WIDTH:, :])
    ms = jnp.mean(x1 * x1, axis=-1, keepdims=True)
    h = (x1 * lax.rsqrt(ms + EPS) * g_ffn_ref[...]).astype(BF16)
    act = (jax.nn.silu(_dot(h, wg_ref[...])) * _dot(h, wu_ref[...])).astype(BF16)
    yield
    o_ref[...] = x1 + _dot(act, wd_ref[...])


def _params(sem):
    return pltpu.CompilerParams(dimension_semantics=sem, vmem_limit_bytes=VMEM_LIMIT)


_DONE = object()


def _staged_call(calls, order, name):
    steps = calls[0]["steps"]
    assert all(c["steps"] == steps for c in calls)
    counts = [(len(c["args"]), len(c["out_specs"]), len(c["scratch"])) for c in calls]

    def body(*refs):
        groups = []
        pos = 0
        for kind in range(3):
            per_call = []
            for cnt in counts:
                per_call.append(refs[pos:pos + cnt[kind]])
                pos += cnt[kind]
            groups.append(per_call)
        gens = [c["stages"](*groups[0][i], *groups[1][i], *groups[2][i]) for i, c in enumerate(calls)]
        for i in order:
            next(gens[i], None)
        assert all(next(g, _DONE) is _DONE for g in gens), "order leaves stages untraced"

    cat = lambda key: [item for c in calls for item in c[key]]
    outs = pl.pallas_call(
        body, grid=(steps,), in_specs=cat("in_specs"), out_specs=cat("out_specs"), out_shape=cat("out_shapes"),
        scratch_shapes=cat("scratch"), compiler_params=_params(("parallel",)), name=name,
    )(*cat("args"))
    split, pos = [], 0
    for _, n_out, _ in counts:
        split.append(outs[pos:pos + n_out])
        pos += n_out
    return split


def _shared_in_proj_args(g_attn, w_in, ln_v_g, ln_v_b, g_q, g_k):
    d_model, in_width = w_in.shape
    gq = (jnp.tile(g_q, N_HEADS) * (HEAD_DIM ** -0.5 * LOG2_E))[None]
    gk = jnp.tile(g_k, N_HEADS)[None]
    bd = jnp.asarray(np.kron(np.eye(N_HEADS), np.full((HEAD_DIM, HEAD_DIM), 1.0 / HEAD_DIM)), BF16)
    args = (g_attn[None], w_in.astype(BF16), ln_v_g[None], ln_v_b[None], gq, gk, bd)
    specs = [_const_spec((1, d_model)), _const_spec((d_model, in_width)), _const_spec((1, WIDTH)),
             _const_spec((1, WIDTH)), _const_spec((1, WIDTH)), _const_spec((1, WIDTH)),
             _const_spec((WIDTH, WIDTH))]
    return args, specs


def _in_proj_prompt_call(x, shared, w_s, b_s, g_out_a, *, tm):
    batch, seq, d_model = x.shape
    n = batch * seq
    nt = seq // tm
    shared_args, shared_specs = shared
    tri = np.tril(np.ones((CHUNK, CHUNK), bool))
    ws = jnp.where(tri, w_s, 0.0).astype(BF16)
    bs = jnp.repeat(b_s.T, HEAD_DIM, axis=1)
    slab = jax.ShapeDtypeStruct((batch, N_SLABS, seq, LANES), F32)
    flat = jax.ShapeDtypeStruct((batch, WIDTH, seq), F32)
    slab_spec = pl.BlockSpec((None, N_SLABS, tm, LANES), lambda i: (i // nt, 0, i % nt, 0))
    flat_spec = pl.BlockSpec((None, WIDTH, tm), lambda i: (i // nt, 0, i % nt))
    row_spec = pl.BlockSpec((tm, WIDTH), lambda i: (i, 0))
    return dict(
        stages=_in_proj_prompt_stages,
        args=(x.reshape(n, d_model), *shared_args, ws, bs, g_out_a[None]),
        in_specs=[pl.BlockSpec((tm, d_model), lambda i: (i, 0))] + shared_specs + [
            _const_spec((N_HEADS, CHUNK, CHUNK)), _const_spec((CHUNK, WIDTH)), _const_spec((1, WIDTH))],
        out_specs=[slab_spec, slab_spec, slab_spec, flat_spec, flat_spec, row_spec],
        out_shapes=[slab, slab, slab, flat, flat, jax.ShapeDtypeStruct((n, WIDTH), BF16)],
        scratch=[], steps=n // tm)


def _in_proj_sample(x, shared, w_s, b_s, g_out_a):
    n_seq, seq, d_model = x.shape
    n = n_seq * seq
    shared_args, shared_specs = shared
    sub = 8
    assert sub % seq == 0 and seq <= CHUNK
    step = np.arange(sub) % seq
    corner = w_s[:, :seq, :seq]
    taps = []
    for d in range(seq):
        src = np.maximum(step - d, 0)
        tap = jnp.where((step >= d)[:, None], corner[:, step, src].T, 0.0)
        taps.append(jnp.repeat(tap, HEAD_DIM, axis=1))
    coef = jnp.stack(taps)
    bias = jnp.repeat(b_s[:, step].T, HEAD_DIM, axis=1)
    flat = jax.ShapeDtypeStruct((n, WIDTH), F32)
    row_spec = pl.BlockSpec((n, WIDTH), lambda i: (0, 0))
    return pl.pallas_call(
        functools.partial(_in_proj_sample_kernel, seq=seq),
        grid=(1,),
        in_specs=[pl.BlockSpec((n, d_model), lambda i: (0, 0))] + shared_specs + [
            _const_spec((seq, sub, WIDTH)), _const_spec((sub, WIDTH)), _const_spec((1, WIDTH))],
        out_specs=[row_spec] * 5,
        out_shape=[flat, flat, flat, flat, jax.ShapeDtypeStruct((n, WIDTH), BF16)],
        compiler_params=_params(("arbitrary",)),
        name="in_proj_sample",
    )(x.reshape(n, d_model), *shared_args, coef, bias, g_out_a[None])


def _score_biases():
    row = (np.arange(2 * CHUNK) % CHUNK)[:, None]
    col = np.arange(2 * CHUNK)[None, :]
    in_band = (col >= row) & (col <= row + CHUNK)
    band = np.stack([in_band & (col >= CHUNK), in_band])
    causal = col[:, :CHUNK] <= row
    to_bias = lambda ok: jnp.asarray(np.where(ok, 0.0, -np.inf), F32)
    return to_bias(band), to_bias(causal)


def _prompt_attention(q, k, v):
    batch, _, seq, _ = q.shape
    n_blocks = seq // CHUNK
    assert all(n_blocks % dil == 0 for _, dil in DILATED_PATTERNS)
    spec = pl.BlockSpec((None, N_SLABS, seq, LANES), lambda b: (b, 0, 0, 0))
    band, causal = _score_biases()
    far_tiles = N_SLABS * DILATED_PATTERNS[2][1] // DILATED_PATTERNS[1][1]
    return pl.pallas_call(
        functools.partial(_prompt_attn_kernel, n_blocks=n_blocks),
        grid=(batch,),
        in_specs=[spec, spec, spec, _const_spec(band.shape), _const_spec(causal.shape)],
        out_specs=spec,
        out_shape=jax.ShapeDtypeStruct(q.shape, F32),
        scratch_shapes=[pltpu.VMEM((N_SLABS, seq, LANES), F32), pltpu.VMEM((N_SLABS, seq, LANES), F32),
                        pltpu.VMEM((N_SLABS, 2 * CHUNK, 2 * CHUNK), F32),
                        pltpu.VMEM((N_SLABS, 2 * CHUNK, 2 * CHUNK), BF16),
                        pltpu.VMEM((6, N_SLABS, seq // DILATED_PATTERNS[1][1], LANES), F32),
                        pltpu.VMEM((far_tiles, 2 * CHUNK, CHUNK), F32), pltpu.VMEM((far_tiles, 2 * CHUNK, CHUNK), BF16)],
        compiler_params=_params(("parallel",)),
        name="prompt_attention",
    )(q, k, v, band, causal)


def _sample_masks(seq, cache_len, past_len, n_rows):
    cache = np.zeros((n_rows, cache_len), np.float32)
    new = [[False] * n_rows for _ in range(seq)]
    for b, (window, dil) in enumerate(DILATED_PATTERNS):
        for t in range(seq):
            for jj in range(window // dil + 1):
                idx = cache_len + t - dil * jj
                if idx < 0 or idx + (past_len - cache_len) < 0:
                    continue
                if idx < cache_len:
                    cache[b * seq + t, idx] = 1.0
                else:
                    new[idx - cache_len][b * seq + t] = True
    return cache, new


UNIT_HEADS = 4
UNITS_PER_SEQ = N_HEADS // UNIT_HEADS


def _sample_attention_operands(q, k, v, cache_k, cache_v):
    n_seq, seq, _ = q.shape
    per_head = lambda a: jnp.pad(a.reshape(n_seq, seq, N_HEADS, HEAD_DIM).transpose(0, 2, 1, 3),
                                 ((0, 0), (0, 0), (0, 2 * seq - seq), (0, 0)))
    return (per_head(q), per_head(k), per_head(v),
            jnp.transpose(cache_k, (0, 2, 3, 1)), jnp.transpose(cache_v, (0, 2, 3, 1)))


def _sample_attention_call(operands, seq, past_len, first, count):
    q8, kn, vn, kt, vt = operands
    n_rows = 4 * seq
    assert len(DILATED_PATTERNS) * seq <= n_rows
    cache_len = kt.shape[3]
    mask, new_valid = _sample_masks(seq, cache_len, past_len, n_rows)
    at_unit = lambda i: ((first + i) // UNITS_PER_SEQ, (first + i) % UNITS_PER_SEQ, 0, 0)
    new_spec = pl.BlockSpec((None, UNIT_HEADS, 2 * seq, HEAD_DIM), at_unit)
    cache_spec = pl.BlockSpec((None, UNIT_HEADS, HEAD_DIM, cache_len), at_unit)
    return dict(
        stages=functools.partial(_sample_attn_stages, seq=seq, new_valid=new_valid),
        args=(q8, kn, vn, kt, vt, jnp.asarray(mask)),
        in_specs=[new_spec, new_spec, new_spec, cache_spec, cache_spec, _const_spec((n_rows, cache_len))],
        out_specs=[pl.BlockSpec((None, UNIT_HEADS, n_rows, HEAD_DIM), lambda i: (i, 0, 0, 0))],
        out_shapes=[jax.ShapeDtypeStruct((count, UNIT_HEADS, n_rows, HEAD_DIM), F32)],
        scratch=[pltpu.VMEM((UNIT_HEADS, n_rows, cache_len), F32), pltpu.VMEM((UNIT_HEADS, n_rows, cache_len), BF16)],
        steps=count)


def _sample_attention_slabs(unit_outs, seq):
    out = jnp.concatenate(unit_outs, axis=0)
    out = out.reshape(-1, N_HEADS, out.shape[2], HEAD_DIM)
    n_seq = out.shape[0]
    b_out = out[:, :, :seq, :].transpose(0, 2, 1, 3).reshape(n_seq * seq, N_SLABS, LANES)
    return b_out.transpose(1, 0, 2)[None]


def _finish_call(x2d, mix_a, b_slab, g_out_b, w_o, g_ffn, w_gate, w_up, w_down, *, tm):
    n, d_model = x2d.shape
    nt = b_slab.shape[2] // tm
    d_ff = w_gate.shape[1]
    return dict(
        stages=_finish_stages,
        args=(x2d, mix_a, b_slab, g_out_b[None], w_o.astype(BF16), g_ffn[None],
              w_gate.astype(BF16), w_up.astype(BF16), w_down.astype(BF16)),
        in_specs=[pl.BlockSpec((tm, d_model), lambda i: (i, 0)),
                  pl.BlockSpec((tm, WIDTH), lambda i: (i, 0)),
                  pl.BlockSpec((None, N_SLABS, tm, LANES), lambda i: (i // nt, 0, i % nt, 0)),
                  _const_spec((1, WIDTH)), _const_spec((2 * WIDTH, d_model)), _const_spec((1, d_model)),
                  _const_spec((d_model, d_ff)), _const_spec((d_model, d_ff)), _const_spec((d_ff, d_model))],
        out_specs=[pl.BlockSpec((tm, d_model), lambda i: (i, 0))],
        out_shapes=[jax.ShapeDtypeStruct((n, d_model), F32)],
        scratch=[], steps=n // tm)


def kernel(x_prompt, x_sample, cache_k, cache_v, g_attn, w_in, ln_v_g, ln_v_b, w_s, b_s, g_q, g_k,
           g_out_a, g_out_b, w_o, g_ffn, w_gate, w_up, w_down):
    depth = w_in.shape[0]
    assert depth == 1, "single-layer step"
    batch, seq, d_model = x_prompt.shape
    n_seq, dec_seq, _ = x_sample.shape
    l = 0
    shared = _shared_in_proj_args(g_attn[l], w_in[l], ln_v_g[l], ln_v_b[l], g_q[l], g_k[l])
    fin = (g_out_b[l], w_o[l], g_ffn[l], w_gate[l], w_up[l], w_down[l])
    n = batch * seq

    qs, ks, vs, vns, mix_as = _in_proj_sample(x_sample, shared, w_s[l], b_s[l], g_out_a[l])
    as3 = lambda a: a.reshape(n_seq, dec_seq, WIDTH)
    att_ops = _sample_attention_operands(as3(qs), as3(ks), as3(vs), cache_k[l], cache_v[l])
    (tm_in, units_in), (tm_fin, units_fin) = (512, 2), (256, 3)
    assert (n // tm_in) * units_in + (n // tm_fin) * units_fin == n_seq * UNITS_PER_SEQ
    next_unit = [0]

    def hosted(host, units_per_step, name):
        streams = []
        for _ in range(units_per_step):
            streams.append(_sample_attention_call(att_ops, dec_seq, PAST_LEN, next_unit[0], host["steps"]))
            next_unit[0] += host["steps"]
        att = list(range(1, units_per_step + 1))
        outs = _staged_call([host] + streams, att + [0] + att + att + [0], name)
        return outs[0], [o[0] for o in outs[1:]]

    in_proj = _in_proj_prompt_call(x_prompt, shared, w_s[l], b_s[l], g_out_a[l], tm=tm_in)
    (q, k, v, k_out, v_out, mix_a), att_a = hosted(in_proj, units_in, "in_proj_prompt_and_sample_attention")
    b_slab = _prompt_attention(q, k, v)
    finish = _finish_call(x_prompt.reshape(n, d_model), mix_a, b_slab, *fin, tm=tm_fin)
    (y_prompt,), att_b = hosted(finish, units_fin, "finish_and_sample_attention")
    win = min(max(w for w, _ in DILATED_PATTERNS), seq)
    window = lambda a: a.reshape(batch, N_HEADS, HEAD_DIM, seq).transpose(0, 3, 1, 2)[:, -win:][None]

    bs_slab = _sample_attention_slabs(att_a + att_b, dec_seq)
    finish_s = _finish_call(x_sample.reshape(n_seq * dec_seq, d_model), mix_as, bs_slab, *fin, tm=n_seq * dec_seq)
    ((y_sample,),) = _staged_call([finish_s], [0, 0], "finish")

    head_shape = (1, n_seq, dec_seq, N_HEADS, HEAD_DIM)
    return (y_prompt.reshape(batch, seq, d_model), y_sample.reshape(n_seq, dec_seq, d_model),
            window(k_out), window(v_out), ks.reshape(head_shape), vs.reshape(head_shape),
            vns.reshape(1, n_seq, dec_seq, WIDTH))
```

```python
import functools

import numpy as np
import jax
import jax.numpy as jnp
from jax import lax
from jax.experimental import pallas as pl
from jax.experimental.pallas import tpu as pltpu

F32 = jnp.float32
BF16 = jnp.bfloat16

HEAD_DIM = 64
N_HEADS = 8
WIDTH = N_HEADS * HEAD_DIM
CHUNK = 128
DILATED_PATTERNS = ((128, 1), (512, 4), (2048, 16))
EPS = 1e-6
PAST_LEN = 8192
NEG_BIG = -1e30
LOG2_E = 1.4426950408889634

LANES = 128
N_SLABS = WIDTH // LANES
VMEM_LIMIT = 56 * 1024 * 1024


def _dot(a, b):
    return jnp.dot(a, b, preferred_element_type=F32)


def _dot_nt(a, b):
    return lax.dot_general(a, b, (((1,), (1,)), ((), ())), preferred_element_type=F32)


def _const_spec(shape):
    return pl.BlockSpec(shape, lambda *_: (0,) * len(shape), pipeline_mode=pl.Buffered(1))


def _head_rms(z, bd_ref, g_ref):
    msq = _dot((z * z).astype(BF16), bd_ref[...])
    return z * lax.rsqrt(msq + EPS) * g_ref[...]


def _features(x_ref, g_attn_ref, w_in_ref, ln_g_ref, ln_b_ref, gq_ref, gk_ref, bd_ref):
    x = x_ref[...]
    ms = jnp.mean(x * x, axis=-1, keepdims=True)
    xn = (x * lax.rsqrt(ms + EPS) * g_attn_ref[...]).astype(BF16)

    def proj(j):
        return _dot(xn, w_in_ref[:, j * WIDTH:(j + 1) * WIDTH])

    zu, zv, zq, zk, v = (proj(j) for j in range(5))
    u = jax.nn.gelu(zu)
    gv = jax.nn.gelu(zv)
    mu = jnp.mean(gv, axis=-1, keepdims=True)
    xc = gv - mu
    var = jnp.mean(xc * xc, axis=-1, keepdims=True)
    vn = xc * lax.rsqrt(var + EPS) * ln_g_ref[...] + ln_b_ref[...]
    q = _head_rms(zq, bd_ref, gq_ref)
    k = _head_rms(zk, bd_ref, gk_ref)
    return u, vn, q, k, v


def _group_norm_rows(slabs, g_ref):
    ssq = sum(jnp.sum(a * a, axis=-1, keepdims=True) for a in slabs)
    r = lax.rsqrt(ssq * (1.0 / WIDTH) + EPS)
    return [a * r * g_ref[:, s * LANES:(s + 1) * LANES] for s, a in enumerate(slabs)]


def _in_proj_prompt_stages(x_ref, g_attn_ref, w_in_ref, ln_g_ref, ln_b_ref, gq_ref, gk_ref, bd_ref,
                           ws_ref, bs_ref, g_out_a_ref,
                           q_ref, k_ref, v_ref, kout_ref, vout_ref, mixa_ref):
    u, vn, q, k, v = _features(x_ref, g_attn_ref, w_in_ref, ln_g_ref, ln_b_ref, gq_ref, gk_ref, bd_ref)
    yield
    tm = u.shape[0]
    first_head = lax.broadcasted_iota(jnp.int32, (CHUNK, LANES), 1) < HEAD_DIM
    for c in range(tm // CHUNK):
        rows = slice(c * CHUNK, (c + 1) * CHUNK)
        vn_c = vn[rows].astype(BF16)
        slabs = []
        for s in range(N_SLABS):
            cols = slice(s * LANES, (s + 1) * LANES)
            g0 = _dot(ws_ref[2 * s], vn_c[:, cols])
            g1 = _dot(ws_ref[2 * s + 1], vn_c[:, cols])
            gate = jnp.where(first_head, g0, g1) + bs_ref[:, cols]
            slabs.append(u[rows, cols] * gate)
        for s, a in enumerate(_group_norm_rows(slabs, g_out_a_ref)):
            mixa_ref[rows, s * LANES:(s + 1) * LANES] = a.astype(BF16)
    for s in range(N_SLABS):
        cols = slice(s * LANES, (s + 1) * LANES)
        q_ref[s] = q[:, cols]
        k_ref[s] = k[:, cols]
        v_ref[s] = v[:, cols]
    kout_ref[...] = k.T
    vout_ref[...] = v.T


def _in_proj_sample_kernel(x_ref, g_attn_ref, w_in_ref, ln_g_ref, ln_b_ref, gq_ref, gk_ref, bd_ref,
                           coef_ref, bias_ref, g_out_a_ref,
                           q_ref, k_ref, v_ref, vn_ref, mixa_ref, *, seq):
    u, vn, q, k, v = _features(x_ref, g_attn_ref, w_in_ref, ln_g_ref, ln_b_ref, gq_ref, gk_ref, bd_ref)
    tm = u.shape[0]
    sub = coef_ref.shape[1]
    gate = jnp.zeros((tm // sub, sub, WIDTH), F32) + bias_ref[...]
    for d in range(seq):
        shifted = vn if d == 0 else pltpu.roll(vn, d, axis=0)
        gate = gate + shifted.reshape(tm // sub, sub, WIDTH) * coef_ref[d]
    a = u * gate.reshape(tm, WIDTH)
    slabs = _group_norm_rows([a[:, s * LANES:(s + 1) * LANES] for s in range(N_SLABS)], g_out_a_ref)
    for s, a_s in enumerate(slabs):
        mixa_ref[:, s * LANES:(s + 1) * LANES] = a_s.astype(BF16)
    q_ref[...] = q
    k_ref[...] = k
    v_ref[...] = v
    vn_ref[...] = vn


def _prompt_attn_kernel(q_ref, k_ref, v_ref, band_ref, causal_ref, o_ref, m_sc, l_sc, s_sc, p_sc, grp_sc, sf_sc, pf_sc, *, n_blocks):
    (w_near, d_near), (w_mid, d_mid), (w_far, d_far) = DILATED_PATTERNS
    assert d_near == 1 and d_far == n_blocks and n_blocks % d_mid == 0
    assert w_near // d_near == w_mid // d_mid == w_far // d_far == CHUNK
    mid_blocks = n_blocks // d_mid
    sub = CHUNK // 2

    first_head = lax.broadcasted_iota(jnp.int32, (CHUNK, LANES), 1) < HEAD_DIM
    ones = jnp.ones((2 * CHUNK, LANES), BF16)

    both = lambda a: jnp.where(first_head, a[:CHUNK], a[CHUNK:])

    def keys_values(ref, s, cur, prev):
        a = ref[s, cur, :]
        return a if prev is None else jnp.concatenate([ref[s, prev, :], a], axis=0)

    def scores(s, cur, prev, src, s_tile):
        q_src, k_src, _ = src
        qb = q_src[s, cur, :].astype(BF16)
        zero = jnp.zeros_like(qb)
        lhs = jnp.concatenate([jnp.where(first_head, qb, zero), jnp.where(first_head, zero, qb)], axis=0)
        kk = keys_values(k_src, s, cur, prev).astype(BF16)
        s_tile[:, :kk.shape[0]] = _dot_nt(lhs, kk)

    def softmax(prev, has_prev, s_tile, p_tile):
        n_keys = CHUNK if prev is None else 2 * CHUNK
        m_parts = []
        for c in range(2 * CHUNK // sub):
            rows = slice(c * sub, (c + 1) * sub)
            bias = causal_ref[rows, :] if prev is None else band_ref[has_prev, rows, :]
            sc = s_tile[rows, :n_keys] + bias
            m = jnp.max(sc, axis=-1, keepdims=True)
            p_tile[rows, :n_keys] = jnp.exp2(sc - m).astype(BF16)
            m_parts.append(jnp.broadcast_to(m, (sub, LANES)))
        return both(jnp.concatenate(m_parts, axis=0))

    def values(s, cur, prev, src, p_tile):
        vv = keys_values(src[2], s, cur, prev).astype(BF16)
        n_keys = vv.shape[0]
        out = _dot(p_tile[:, :n_keys], jnp.concatenate([vv, ones[:n_keys]], axis=1))
        return both(out[:, LANES:]), both(out[:, :LANES])

    def attend(s, cur, prev, has_prev):
        src = (q_ref, k_ref, v_ref)
        scores(s, cur, prev, src, s_sc.at[s])
        m = softmax(prev, has_prev, s_sc.at[s], p_sc.at[s])
        return (m,) + values(s, cur, prev, src, p_sc.at[s])

    def merged(s, cur, stats):
        m_new, l_new, acc_new = stats
        m_old = m_sc[s, cur, :]
        m_tot = jnp.maximum(m_old, m_new)
        a_old = jnp.exp2(m_old - m_tot)
        a_new = jnp.exp2(m_new - m_tot)
        return (m_tot, a_old * l_sc[s, cur, :] + a_new * l_new, a_old * o_ref[s, cur, :] + a_new * acc_new)

    staged = [grp_sc.at[i] for i in range(6)]
    sub_classes = d_far // d_mid

    def far_group(r, carry):
        grp = pl.ds(r, o_ref.shape[1] // d_mid, stride=d_mid)
        for s in range(N_SLABS):
            for src, dst in zip((q_ref, k_ref, v_ref), staged[:3]):
                dst[s] = src[s, grp, :]
        blocks = [(c * N_SLABS + s, s, pl.ds(c, CHUNK, stride=sub_classes))
                  for c in range(sub_classes) for s in range(N_SLABS)]
        st_m, st_l, st_acc = staged[3:]
        for tile, s, cur in blocks:
            scores(s, cur, None, staged[:3], sf_sc.at[tile])
        for tile, s, cur in blocks:
            st_m[s, cur, :] = softmax(None, None, sf_sc.at[tile], pf_sc.at[tile])
        for tile, s, cur in blocks:
            st_l[s, cur, :], st_acc[s, cur, :] = values(s, cur, None, staged[:3], pf_sc.at[tile])
        for s in range(N_SLABS):
            for dst, src in zip((m_sc, l_sc, o_ref), staged[3:]):
                dst[s, grp, :] = src[s]
        return carry

    lax.fori_loop(0, d_mid, far_group, 0)

    def span(jb, carry):
        for r in range(d_mid):
            cur = pl.ds(r + (d_mid * CHUNK) * jb, CHUNK, stride=d_mid)
            prev = pl.ds(r + (d_mid * CHUNK) * jnp.maximum(jb - 1, 0), CHUNK, stride=d_mid)
            for s in range(N_SLABS):
                m_sc[s, cur, :], l_sc[s, cur, :], o_ref[s, cur, :] = merged(s, cur, attend(s, cur, prev, jnp.minimum(jb, 1)))
        for i in range(d_mid):
            j = jb * d_mid + i
            cur = pl.ds(pl.multiple_of(j * CHUNK, CHUNK), CHUNK)
            prev = pl.ds(pl.multiple_of(jnp.maximum(j - 1, 0) * CHUNK, CHUNK), CHUNK)
            for s in range(N_SLABS):
                _, l_tot, acc_tot = merged(s, cur, attend(s, cur, prev, jnp.minimum(j, 1)))
                o_ref[s, cur, :] = acc_tot / l_tot
        return carry

    lax.fori_loop(0, mid_blocks, span, 0)


def _sample_attn_stages(new_ref, kt_ref, vt_ref, mask_ref, o_ref, s_sc, p_sc, *, seq, new_valid):
    n_rows = mask_ref.shape[0]
    n_real = len(DILATED_PATTERNS) * seq
    assert new_ref.shape[2] == 2 * seq and n_rows == 4 * seq
    q_ref, kn_ref, vn_ref = (new_ref.at[:, i] for i in range(3))

    def branch_rows(a):
        return jnp.concatenate([a + pltpu.roll(a, seq, axis=0), a], axis=0)

    row = lax.broadcasted_iota(jnp.int32, (n_rows, 1), 0)
    real = row < n_real
    valid = mask_ref[...] > 0.5
    new_rows = []
    for tp in range(seq):
        sel = functools.reduce(jnp.logical_or, [row == r for r in range(n_rows) if new_valid[tp][r]])
        new_rows.append(sel)

    heads = range(q_ref.shape[0])
    for h in heads:
        sc = _dot(q_ref[h].astype(BF16), kt_ref[h].astype(BF16))
        s_sc[h] = jnp.where(valid, branch_rows(sc), -jnp.inf)
    yield

    weights = []
    for h in heads:
        q = branch_rows(q_ref[h])
        kn = kn_ref[h]
        s_new = [jnp.where(new_rows[tp], jnp.sum(q * kn[tp:tp + 1, :], axis=-1, keepdims=True), -jnp.inf)
                 for tp in range(seq)]
        sc = s_sc[h]
        m = jnp.max(sc, axis=-1, keepdims=True)
        for sn in s_new:
            m = jnp.maximum(m, sn)
        m = jnp.where(real, m, 0.0)
        p = jnp.exp2(sc - m)
        p_sc[h] = p.astype(BF16)
        p_new = [jnp.exp2(sn - m) for sn in s_new]
        l = jnp.sum(p, axis=-1, keepdims=True) + sum(p_new)
        m_eff = jnp.where(real, m, NEG_BIG)
        l_eff = jnp.where(real, l, 0.0)
        m_all = m_eff
        for i in range(1, n_rows // seq):
            m_all = jnp.maximum(m_all, pltpu.roll(m_eff, i * seq, axis=0))
        c = jnp.exp2(m_eff - m_all)
        cl = c * l_eff
        den = cl
        for i in range(1, n_rows // seq):
            den = den + pltpu.roll(cl, i * seq, axis=0)
        w = jnp.where(real, c / den, 0.0)
        weights.append((w, [pn * w for pn in p_new]))
    yield

    for h in heads:
        w, pw_new = weights[h]
        vn = vn_ref[h]
        out = _dot_nt(p_sc[h], vt_ref[h].astype(BF16)) * w
        for tp in range(seq):
            out = out + pw_new[tp] * vn[tp:tp + 1, :]
        tot = out
        for i in range(1, n_rows // seq):
            tot = tot + pltpu.roll(out, i * seq, axis=0)
        o_ref[h] = tot


def _finish_stages(x_ref, mixa_ref, b_ref, g_out_b_ref, wo_ref, g_ffn_ref, wg_ref, wu_ref, wd_ref, o_ref):
    slabs = _group_norm_rows([b_ref[s] for s in range(N_SLABS)], g_out_b_ref)
    mix_b = jnp.concatenate([a.astype(BF16) for a in slabs], axis=-1)
    x1 = x_ref[...] + _dot(mixa_ref[...], wo_ref[:WIDTH, :]) + _dot(mix_b, wo_ref[WIDTH:, :])
    ms = jnp.mean(x1 * x1, axis=-1, keepdims=True)
    h = (x1 * lax.rsqrt(ms + EPS) * g_ffn_ref[...]).astype(BF16)
    act = (jax.nn.silu(_dot(h, wg_ref[...])) * _dot(h, wu_ref[...])).astype(BF16)
    yield
    o_ref[...] = x1 + _dot(act, wd_ref[...])


def _params(sem):
    return pltpu.CompilerParams(dimension_semantics=sem, vmem_limit_bytes=VMEM_LIMIT)


_DONE = object()


def _staged_call(calls, order, name):
    steps = calls[0]["steps"]
    assert all(c["steps"] == steps for c in calls)
    counts = [(len(c["args"]), len(c["out_specs"]), len(c["scratch"])) for c in calls]

    def body(*refs):
        groups = []
        pos = 0
        for kind in range(3):
            per_call = []
            for cnt in counts:
                per_call.append(refs[pos:pos + cnt[kind]])
                pos += cnt[kind]
            groups.append(per_call)
        gens = [c["stages"](*groups[0][i], *groups[1][i], *groups[2][i]) for i, c in enumerate(calls)]
        for i in order:
            next(gens[i], None)
        assert all(next(g, _DONE) is _DONE for g in gens), "order leaves stages untraced"

    cat = lambda key: [item for c in calls for item in c[key]]
    outs = pl.pallas_call(
        body, grid=(steps,), in_specs=cat("in_specs"), out_specs=cat("out_specs"), out_shape=cat("out_shapes"),
        scratch_shapes=cat("scratch"), compiler_params=_params(("parallel",)), name=name,
    )(*cat("args"))
    split, pos = [], 0
    for _, n_out, _ in counts:
        split.append(outs[pos:pos + n_out])
        pos += n_out
    return split


def _shared_in_proj_args(g_attn, w_in, ln_v_g, ln_v_b, g_q, g_k):
    d_model, in_width = w_in.shape
    gq = (jnp.tile(g_q, N_HEADS) * (HEAD_DIM ** -0.5 * LOG2_E))[None]
    gk = jnp.tile(g_k, N_HEADS)[None]
    bd = jnp.asarray(np.kron(np.eye(N_HEADS), np.full((HEAD_DIM, HEAD_DIM), 1.0 / HEAD_DIM)), BF16)
    args = (g_attn[None], w_in.astype(BF16), ln_v_g[None], ln_v_b[None], gq, gk, bd)
    specs = [_const_spec((1, d_model)), _const_spec((d_model, in_width)), _const_spec((1, WIDTH)),
             _const_spec((1, WIDTH)), _const_spec((1, WIDTH)), _const_spec((1, WIDTH)),
             _const_spec((WIDTH, WIDTH))]
    return args, specs


def _in_proj_prompt_call(x, shared, w_s, b_s, g_out_a, *, tm):
    batch, seq, d_model = x.shape
    n = batch * seq
    nt = seq // tm
    shared_args, shared_specs = shared
    tri = np.tril(np.ones((CHUNK, CHUNK), bool))
    ws = jnp.where(tri, w_s, 0.0).astype(BF16)
    bs = jnp.repeat(b_s.T, HEAD_DIM, axis=1)
    slab = jax.ShapeDtypeStruct((batch, N_SLABS, seq, LANES), F32)
    flat = jax.ShapeDtypeStruct((batch, WIDTH, seq), F32)
    slab_spec = pl.BlockSpec((None, N_SLABS, tm, LANES), lambda i: (i // nt, 0, i % nt, 0))
    flat_spec = pl.BlockSpec((None, WIDTH, tm), lambda i: (i // nt, 0, i % nt))
    row_spec = pl.BlockSpec((tm, WIDTH), lambda i: (i, 0))
    return dict(
        stages=_in_proj_prompt_stages,
        args=(x.reshape(n, d_model), *shared_args, ws, bs, g_out_a[None]),
        in_specs=[pl.BlockSpec((tm, d_model), lambda i: (i, 0))] + shared_specs + [
            _const_spec((N_HEADS, CHUNK, CHUNK)), _const_spec((CHUNK, WIDTH)), _const_spec((1, WIDTH))],
        out_specs=[slab_spec, slab_spec, slab_spec, flat_spec, flat_spec, row_spec],
        out_shapes=[slab, slab, slab, flat, flat, jax.ShapeDtypeStruct((n, WIDTH), BF16)],
        scratch=[], steps=n // tm)


def _in_proj_sample(x, shared, w_s, b_s, g_out_a):
    n_seq, seq, d_model = x.shape
    n = n_seq * seq
    shared_args, shared_specs = shared
    sub = 8
    assert sub % seq == 0 and seq <= CHUNK
    step = np.arange(sub) % seq
    corner = w_s[:, :seq, :seq]
    taps = []
    for d in range(seq):
        src = np.maximum(step - d, 0)
        tap = jnp.where((step >= d)[:, None], corner[:, step, src].T, 0.0)
        taps.append(jnp.repeat(tap, HEAD_DIM, axis=1))
    coef = jnp.stack(taps)
    bias = jnp.repeat(b_s[:, step].T, HEAD_DIM, axis=1)
    flat = jax.ShapeDtypeStruct((n, WIDTH), F32)
    row_spec = pl.BlockSpec((n, WIDTH), lambda i: (0, 0))
    return pl.pallas_call(
        functools.partial(_in_proj_sample_kernel, seq=seq),
        grid=(1,),
        in_specs=[pl.BlockSpec((n, d_model), lambda i: (0, 0))] + shared_specs + [
            _const_spec((seq, sub, WIDTH)), _const_spec((sub, WIDTH)), _const_spec((1, WIDTH))],
        out_specs=[row_spec] * 5,
        out_shape=[flat, flat, flat, flat, jax.ShapeDtypeStruct((n, WIDTH), BF16)],
        compiler_params=_params(("arbitrary",)),
        name="in_proj_sample",
    )(x.reshape(n, d_model), *shared_args, coef, bias, g_out_a[None])


def _score_biases():
    row = (np.arange(2 * CHUNK) % CHUNK)[:, None]
    col = np.arange(2 * CHUNK)[None, :]
    in_band = (col >= row) & (col <= row + CHUNK)
    band = np.stack([in_band & (col >= CHUNK), in_band])
    causal = col[:, :CHUNK] <= row
    to_bias = lambda ok: jnp.asarray(np.where(ok, 0.0, -np.inf), F32)
    return to_bias(band), to_bias(causal)


def _prompt_attention(q, k, v):
    batch, _, seq, _ = q.shape
    n_blocks = seq // CHUNK
    assert all(n_blocks % dil == 0 for _, dil in DILATED_PATTERNS)
    spec = pl.BlockSpec((None, N_SLABS, seq, LANES), lambda b: (b, 0, 0, 0))
    band, causal = _score_biases()
    far_tiles = N_SLABS * DILATED_PATTERNS[2][1] // DILATED_PATTERNS[1][1]
    return pl.pallas_call(
        functools.partial(_prompt_attn_kernel, n_blocks=n_blocks),
        grid=(batch,),
        in_specs=[spec, spec, spec, _const_spec(band.shape), _const_spec(causal.shape)],
        out_specs=spec,
        out_shape=jax.ShapeDtypeStruct(q.shape, F32),
        scratch_shapes=[pltpu.VMEM((N_SLABS, seq, LANES), F32), pltpu.VMEM((N_SLABS, seq, LANES), F32),
                        pltpu.VMEM((N_SLABS, 2 * CHUNK, 2 * CHUNK), F32),
                        pltpu.VMEM((N_SLABS, 2 * CHUNK, 2 * CHUNK), BF16),
                        pltpu.VMEM((6, N_SLABS, seq // DILATED_PATTERNS[1][1], LANES), F32),
                        pltpu.VMEM((far_tiles, 2 * CHUNK, CHUNK), F32), pltpu.VMEM((far_tiles, 2 * CHUNK, CHUNK), BF16)],
        compiler_params=_params(("parallel",)),
        name="prompt_attention",
    )(q, k, v, band, causal)


def _sample_masks(seq, cache_len, past_len, n_rows):
    cache = np.zeros((n_rows, cache_len), np.float32)
    new = [[False] * n_rows for _ in range(seq)]
    for b, (window, dil) in enumerate(DILATED_PATTERNS):
        for t in range(seq):
            for jj in range(window // dil + 1):
                idx = cache_len + t - dil * jj
                if idx < 0 or idx + (past_len - cache_len) < 0:
                    continue
                if idx < cache_len:
                    cache[b * seq + t, idx] = 1.0
                else:
                    new[idx - cache_len][b * seq + t] = True
    return cache, new


UNIT_HEADS = 4
UNITS_PER_SEQ = N_HEADS // UNIT_HEADS


def _sample_attention_operands(q, k, v, cache_k, cache_v):
    n_seq, seq, _ = q.shape
    new = jnp.stack([q, k, v], axis=2).reshape(n_seq, seq, 3, N_HEADS, HEAD_DIM).transpose(0, 3, 2, 1, 4)
    new = jnp.pad(new, ((0, 0), (0, 0), (0, 0), (0, seq), (0, 0)))
    return new, jnp.transpose(cache_k, (0, 2, 3, 1)), jnp.transpose(cache_v, (0, 2, 3, 1))


def _sample_attention_call(operands, seq, past_len, first, count):
    new, kt, vt = operands
    n_rows = 4 * seq
    assert len(DILATED_PATTERNS) * seq <= n_rows
    cache_len = kt.shape[3]
    mask, new_valid = _sample_masks(seq, cache_len, past_len, n_rows)
    unit = lambda i: ((first + i) // UNITS_PER_SEQ, (first + i) % UNITS_PER_SEQ)
    new_spec = pl.BlockSpec((None, UNIT_HEADS, 3, 2 * seq, HEAD_DIM), lambda i: unit(i) + (0, 0, 0))
    cache_spec = pl.BlockSpec((None, UNIT_HEADS, HEAD_DIM, cache_len), lambda i: unit(i) + (0, 0))
    return dict(
        stages=functools.partial(_sample_attn_stages, seq=seq, new_valid=new_valid),
        args=(new, kt, vt, jnp.asarray(mask)),
        in_specs=[new_spec, cache_spec, cache_spec, _const_spec((n_rows, cache_len))],
        out_specs=[pl.BlockSpec((None, UNIT_HEADS, n_rows, HEAD_DIM), lambda i: (i, 0, 0, 0))],
        out_shapes=[jax.ShapeDtypeStruct((count, UNIT_HEADS, n_rows, HEAD_DIM), F32)],
        scratch=[pltpu.VMEM((UNIT_HEADS, n_rows, cache_len), F32), pltpu.VMEM((UNIT_HEADS, n_rows, cache_len), BF16)],
        steps=count)


def _sample_attention_slabs(unit_outs, seq):
    out = jnp.concatenate(unit_outs, axis=0)
    out = out.reshape(-1, N_HEADS, out.shape[2], HEAD_DIM)
    n_seq = out.shape[0]
    b_out = out[:, :, :seq, :].transpose(0, 2, 1, 3).reshape(n_seq * seq, N_SLABS, LANES)
    return b_out.transpose(1, 0, 2)[None]


def _finish_call(x2d, mix_a, b_slab, g_out_b, w_o, g_ffn, w_gate, w_up, w_down, *, tm):
    n, d_model = x2d.shape
    nt = b_slab.shape[2] // tm
    d_ff = w_gate.shape[1]
    return dict(
        stages=_finish_stages,
        args=(x2d, mix_a, b_slab, g_out_b[None], w_o.astype(BF16), g_ffn[None],
              w_gate.astype(BF16), w_up.astype(BF16), w_down.astype(BF16)),
        in_specs=[pl.BlockSpec((tm, d_model), lambda i: (i, 0)),
                  pl.BlockSpec((tm, WIDTH), lambda i: (i, 0)),
                  pl.BlockSpec((None, N_SLABS, tm, LANES), lambda i: (i // nt, 0, i % nt, 0)),
                  _const_spec((1, WIDTH)), _const_spec((2 * WIDTH, d_model)), _const_spec((1, d_model)),
                  _const_spec((d_model, d_ff)), _const_spec((d_model, d_ff)), _const_spec((d_ff, d_model))],
        out_specs=[pl.BlockSpec((tm, d_model), lambda i: (i, 0))],
        out_shapes=[jax.ShapeDtypeStruct((n, d_model), F32)],
        scratch=[], steps=n // tm)


def kernel(x_prompt, x_sample, cache_k, cache_v, g_attn, w_in, ln_v_g, ln_v_b, w_s, b_s, g_q, g_k,
           g_out_a, g_out_b, w_o, g_ffn, w_gate, w_up, w_down):
    depth = w_in.shape[0]
    assert depth == 1, "single-layer step"
    batch, seq, d_model = x_prompt.shape
    n_seq, dec_seq, _ = x_sample.shape
    l = 0
    shared = _shared_in_proj_args(g_attn[l], w_in[l], ln_v_g[l], ln_v_b[l], g_q[l], g_k[l])
    fin = (g_out_b[l], w_o[l], g_ffn[l], w_gate[l], w_up[l], w_down[l])
    n = batch * seq

    qs, ks, vs, vns, mix_as = _in_proj_sample(x_sample, shared, w_s[l], b_s[l], g_out_a[l])
    as3 = lambda a: a.reshape(n_seq, dec_seq, WIDTH)
    att_ops = _sample_attention_operands(as3(qs), as3(ks), as3(vs), cache_k[l], cache_v[l])
    (tm_in, units_in), (tm_fin, units_fin) = (512, 2), (256, 3)
    assert (n // tm_in) * units_in + (n // tm_fin) * units_fin == n_seq * UNITS_PER_SEQ
    next_unit = [0]

    def hosted(host, units_per_step, name):
        streams = []
        for _ in range(units_per_step):
            streams.append(_sample_attention_call(att_ops, dec_seq, PAST_LEN, next_unit[0], host["steps"]))
            next_unit[0] += host["steps"]
        att = list(range(1, units_per_step + 1))
        outs = _staged_call([host] + streams, att + [0] + att + att + [0], name)
        return outs[0], [o[0] for o in outs[1:]]

    in_proj = _in_proj_prompt_call(x_prompt, shared, w_s[l], b_s[l], g_out_a[l], tm=tm_in)
    (q, k, v, k_out, v_out, mix_a), att_a = hosted(in_proj, units_in, "in_proj_prompt_and_sample_attention")
    b_slab = _prompt_attention(q, k, v)
    finish = _finish_call(x_prompt.reshape(n, d_model), mix_a, b_slab, *fin, tm=tm_fin)
    (y_prompt,), att_b = hosted(finish, units_fin, "finish_and_sample_attention")
    win = min(max(w for w, _ in DILATED_PATTERNS), seq)
    window = lambda a: a.reshape(batch, N_HEADS, HEAD_DIM, seq).transpose(0, 3, 1, 2)[:, -win:][None]

    bs_slab = _sample_attention_slabs(att_a + att_b, dec_seq)
    finish_s = _finish_call(x_sample.reshape(n_seq * dec_seq, d_model), mix_as, bs_slab, *fin, tm=n_seq * dec_seq)
    ((y_sample,),) = _staged_call([finish_s], [0, 0], "finish")

    head_shape = (1, n_seq, dec_seq, N_HEADS, HEAD_DIM)
    return (y_prompt.reshape(batch, seq, d_model), y_sample.reshape(n_seq, dec_seq, d_model),
            window(k_out), window(v_out), ks.reshape(head_shape), vs.reshape(head_shape),
            vns.reshape(1, n_seq, dec_seq, WIDTH))
```

```python
import functools

import numpy as np
import jax
import jax.numpy as jnp
from jax import lax
from jax.experimental import pallas as pl
from jax.experimental.pallas import tpu as pltpu

F32 = jnp.float32
BF16 = jnp.bfloat16

HEAD_DIM = 64
N_HEADS = 8
WIDTH = N_HEADS * HEAD_DIM
CHUNK = 128
DILATED_PATTERNS = ((128, 1), (512, 4), (2048, 16))
EPS = 1e-6
PAST_LEN = 8192
NEG_BIG = -1e30
LOG2_E = 1.4426950408889634

LANES = 128
N_SLABS = WIDTH // LANES
VMEM_LIMIT = 56 * 1024 * 1024


def _dot(a, b):
    return jnp.dot(a, b, preferred_element_type=F32)


def _dot_nt(a, b):
    return lax.dot_general(a, b, (((1,), (1,)), ((), ())), preferred_element_type=F32)


def _const_spec(shape):
    return pl.BlockSpec(shape, lambda *_: (0,) * len(shape), pipeline_mode=pl.Buffered(1))


def _head_rms(z, bd_ref, g_ref):
    msq = _dot((z * z).astype(BF16), bd_ref[...])
    return z * lax.rsqrt(msq + EPS) * g_ref[...]


def _features(x_ref, g_attn_ref, w_in_ref, ln_g_ref, ln_b_ref, gq_ref, gk_ref, bd_ref):
    x = x_ref[...]
    ms = jnp.mean(x * x, axis=-1, keepdims=True)
    xn = (x * lax.rsqrt(ms + EPS) * g_attn_ref[...]).astype(BF16)

    def proj(j):
        return _dot(xn, w_in_ref[:, j * WIDTH:(j + 1) * WIDTH])

    zu, zv, zq, zk, v = (proj(j) for j in range(5))
    u = jax.nn.gelu(zu)
    gv = jax.nn.gelu(zv)
    mu = jnp.mean(gv, axis=-1, keepdims=True)
    xc = gv - mu
    var = jnp.mean(xc * xc, axis=-1, keepdims=True)
    vn = xc * lax.rsqrt(var + EPS) * ln_g_ref[...] + ln_b_ref[...]
    q = _head_rms(zq, bd_ref, gq_ref)
    k = _head_rms(zk, bd_ref, gk_ref)
    return u, vn, q, k, v


def _group_norm_rows(slabs, g_ref):
    ssq = sum(jnp.sum(a * a, axis=-1, keepdims=True) for a in slabs)
    r = lax.rsqrt(ssq * (1.0 / WIDTH) + EPS)
    return [a * r * g_ref[:, s * LANES:(s + 1) * LANES] for s, a in enumerate(slabs)]


def _in_proj_prompt_stages(x_ref, g_attn_ref, w_in_ref, ln_g_ref, ln_b_ref, gq_ref, gk_ref, bd_ref,
                           ws_ref, bs_ref, g_out_a_ref,
                           q_ref, k_ref, v_ref, kout_ref, vout_ref, mixa_ref):
    u, vn, q, k, v = _features(x_ref, g_attn_ref, w_in_ref, ln_g_ref, ln_b_ref, gq_ref, gk_ref, bd_ref)
    yield
    tm = u.shape[0]
    first_head = lax.broadcasted_iota(jnp.int32, (CHUNK, LANES), 1) < HEAD_DIM
    for c in range(tm // CHUNK):
        rows = slice(c * CHUNK, (c + 1) * CHUNK)
        vn_c = vn[rows].astype(BF16)
        slabs = []
        for s in range(N_SLABS):
            cols = slice(s * LANES, (s + 1) * LANES)
            g0 = _dot(ws_ref[2 * s], vn_c[:, cols])
            g1 = _dot(ws_ref[2 * s + 1], vn_c[:, cols])
            gate = jnp.where(first_head, g0, g1) + bs_ref[:, cols]
            slabs.append(u[rows, cols] * gate)
        for s, a in enumerate(_group_norm_rows(slabs, g_out_a_ref)):
            mixa_ref[rows, s * LANES:(s + 1) * LANES] = a.astype(BF16)
    for s in range(N_SLABS):
        cols = slice(s * LANES, (s + 1) * LANES)
        q_ref[s] = q[:, cols]
        k_ref[s] = k[:, cols]
        v_ref[s] = v[:, cols]
    kout_ref[...] = k.T
    vout_ref[...] = v.T


def _in_proj_sample_kernel(x_ref, g_attn_ref, w_in_ref, ln_g_ref, ln_b_ref, gq_ref, gk_ref, bd_ref,
                           coef_ref, bias_ref, g_out_a_ref,
                           q_ref, k_ref, v_ref, vn_ref, mixa_ref, *, seq):
    u, vn, q, k, v = _features(x_ref, g_attn_ref, w_in_ref, ln_g_ref, ln_b_ref, gq_ref, gk_ref, bd_ref)
    tm = u.shape[0]
    sub = coef_ref.shape[1]
    gate = jnp.zeros((tm // sub, sub, WIDTH), F32) + bias_ref[...]
    for d in range(seq):
        shifted = vn if d == 0 else pltpu.roll(vn, d, axis=0)
        gate = gate + shifted.reshape(tm // sub, sub, WIDTH) * coef_ref[d]
    a = u * gate.reshape(tm, WIDTH)
    slabs = _group_norm_rows([a[:, s * LANES:(s + 1) * LANES] for s in range(N_SLABS)], g_out_a_ref)
    for s, a_s in enumerate(slabs):
        mixa_ref[:, s * LANES:(s + 1) * LANES] = a_s.astype(BF16)
    q_ref[...] = q
    k_ref[...] = k
    v_ref[...] = v
    vn_ref[...] = vn


def _prompt_attn_kernel(q_ref, k_ref, v_ref, band_ref, causal_ref, o_ref, m_sc, l_sc, s_sc, p_sc, grp_sc, sf_sc, pf_sc, *, n_blocks):
    (w_near, d_near), (w_mid, d_mid), (w_far, d_far) = DILATED_PATTERNS
    assert d_near == 1 and d_far == n_blocks and n_blocks % d_mid == 0
    assert w_near // d_near == w_mid // d_mid == w_far // d_far == CHUNK
    mid_blocks = n_blocks // d_mid
    sub = CHUNK // 2

    first_head = lax.broadcasted_iota(jnp.int32, (CHUNK, LANES), 1) < HEAD_DIM
    ones = jnp.ones((2 * CHUNK, LANES), BF16)

    both = lambda a: jnp.where(first_head, a[:CHUNK], a[CHUNK:])

    def keys_values(ref, s, cur, prev):
        a = ref[s, cur, :]
        return a if prev is None else jnp.concatenate([ref[s, prev, :], a], axis=0)

    def scores(s, cur, prev, src, s_tile):
        q_src, k_src, _ = src
        qb = q_src[s, cur, :].astype(BF16)
        zero = jnp.zeros_like(qb)
        lhs = jnp.concatenate([jnp.where(first_head, qb, zero), jnp.where(first_head, zero, qb)], axis=0)
        kk = keys_values(k_src, s, cur, prev).astype(BF16)
        s_tile[:, :kk.shape[0]] = _dot_nt(lhs, kk)

    def softmax(prev, has_prev, s_tile, p_tile):
        n_keys = CHUNK if prev is None else 2 * CHUNK
        m_parts = []
        for c in range(2 * CHUNK // sub):
            rows = slice(c * sub, (c + 1) * sub)
            bias = causal_ref[rows, :] if prev is None else band_ref[has_prev, rows, :]
            sc = s_tile[rows, :n_keys] + bias
            m = jnp.max(sc, axis=-1, keepdims=True)
            p_tile[rows, :n_keys] = jnp.exp2(sc - m).astype(BF16)
            m_parts.append(jnp.broadcast_to(m, (sub, LANES)))
        return both(jnp.concatenate(m_parts, axis=0))

    def values(s, cur, prev, src, p_tile):
        vv = keys_values(src[2], s, cur, prev).astype(BF16)
        n_keys = vv.shape[0]
        out = _dot(p_tile[:, :n_keys], jnp.concatenate([vv, ones[:n_keys]], axis=1))
        return both(out[:, LANES:]), both(out[:, :LANES])

    def attend(s, cur, prev, has_prev):
        src = (q_ref, k_ref, v_ref)
        scores(s, cur, prev, src, s_sc.at[s])
        m = softmax(prev, has_prev, s_sc.at[s], p_sc.at[s])
        return (m,) + values(s, cur, prev, src, p_sc.at[s])

    def merged(s, cur, stats):
        m_new, l_new, acc_new = stats
        m_old = m_sc[s, cur, :]
        m_tot = jnp.maximum(m_old, m_new)
        a_old = jnp.exp2(m_old - m_tot)
        a_new = jnp.exp2(m_new - m_tot)
        return (m_tot, a_old * l_sc[s, cur, :] + a_new * l_new, a_old * o_ref[s, cur, :] + a_new * acc_new)

    staged = [grp_sc.at[i] for i in range(6)]
    sub_classes = d_far // d_mid

    def far_group(r, carry):
        grp = pl.ds(r, o_ref.shape[1] // d_mid, stride=d_mid)
        for s in range(N_SLABS):
            for src, dst in zip((q_ref, k_ref, v_ref), staged[:3]):
                dst[s] = src[s, grp, :]
        blocks = [(c * N_SLABS + s, s, pl.ds(c, CHUNK, stride=sub_classes))
                  for c in range(sub_classes) for s in range(N_SLABS)]
        st_m, st_l, st_acc = staged[3:]
        for tile, s, cur in blocks:
            scores(s, cur, None, staged[:3], sf_sc.at[tile])
        for tile, s, cur in blocks:
            st_m[s, cur, :] = softmax(None, None, sf_sc.at[tile], pf_sc.at[tile])
        for tile, s, cur in blocks:
            st_l[s, cur, :], st_acc[s, cur, :] = values(s, cur, None, staged[:3], pf_sc.at[tile])
        for s in range(N_SLABS):
            for dst, src in zip((m_sc, l_sc, o_ref), staged[3:]):
                dst[s, grp, :] = src[s]
        return carry

    lax.fori_loop(0, d_mid, far_group, 0)

    def span(jb, carry):
        for r in range(d_mid):
            cur = pl.ds(r + (d_mid * CHUNK) * jb, CHUNK, stride=d_mid)
            prev = pl.ds(r + (d_mid * CHUNK) * jnp.maximum(jb - 1, 0), CHUNK, stride=d_mid)
            for s in range(N_SLABS):
                m_sc[s, cur, :], l_sc[s, cur, :], o_ref[s, cur, :] = merged(s, cur, attend(s, cur, prev, jnp.minimum(jb, 1)))
        for i in range(d_mid):
            j = jb * d_mid + i
            cur = pl.ds(pl.multiple_of(j * CHUNK, CHUNK), CHUNK)
            prev = pl.ds(pl.multiple_of(jnp.maximum(j - 1, 0) * CHUNK, CHUNK), CHUNK)
            for s in range(N_SLABS):
                _, l_tot, acc_tot = merged(s, cur, attend(s, cur, prev, jnp.minimum(j, 1)))
                o_ref[s, cur, :] = acc_tot / l_tot
        return carry

    lax.fori_loop(0, mid_blocks, span, 0)


def _sample_attn_stages(new_ref, kt_ref, vt_ref, mask_ref, o_ref, s_sc, p_sc, *, seq, new_valid):
    n_rows = mask_ref.shape[0]
    n_real = len(DILATED_PATTERNS) * seq
    assert new_ref.shape[2] == 2 * seq and n_rows == 4 * seq
    q_ref, kn_ref, vn_ref = (new_ref.at[:, i] for i in range(3))

    def branch_rows(a):
        return jnp.concatenate([a + pltpu.roll(a, seq, axis=0), a], axis=0)

    row = lax.broadcasted_iota(jnp.int32, (n_rows, 1), 0)
    real = row < n_real
    valid = mask_ref[...] > 0.5
    new_rows = []
    for tp in range(seq):
        sel = functools.reduce(jnp.logical_or, [row == r for r in range(n_rows) if new_valid[tp][r]])
        new_rows.append(sel)

    heads = range(q_ref.shape[0])
    for h in heads:
        sc = _dot(q_ref[h].astype(BF16), kt_ref[h].astype(BF16))
        s_sc[h] = jnp.where(valid, branch_rows(sc), -jnp.inf)
    yield

    weights = []
    for h in heads:
        q = branch_rows(q_ref[h])
        kn = kn_ref[h]
        s_new = [jnp.where(new_rows[tp], jnp.sum(q * kn[tp:tp + 1, :], axis=-1, keepdims=True), -jnp.inf)
                 for tp in range(seq)]
        sc = s_sc[h]
        m = jnp.max(sc, axis=-1, keepdims=True)
        for sn in s_new:
            m = jnp.maximum(m, sn)
        m = jnp.where(real, m, 0.0)
        p = jnp.exp2(sc - m)
        p_sc[h] = p.astype(BF16)
        p_new = [jnp.exp2(sn - m) for sn in s_new]
        l = jnp.sum(p, axis=-1, keepdims=True) + sum(p_new)
        m_eff = jnp.where(real, m, NEG_BIG)
        l_eff = jnp.where(real, l, 0.0)
        m_all = m_eff
        for i in range(1, n_rows // seq):
            m_all = jnp.maximum(m_all, pltpu.roll(m_eff, i * seq, axis=0))
        c = jnp.exp2(m_eff - m_all)
        cl = c * l_eff
        den = cl
        for i in range(1, n_rows // seq):
            den = den + pltpu.roll(cl, i * seq, axis=0)
        w = jnp.where(real, c / den, 0.0)
        weights.append((w, [pn * w for pn in p_new]))
    yield

    for h in heads:
        w, pw_new = weights[h]
        vn = vn_ref[h]
        out = _dot_nt(p_sc[h], vt_ref[h].astype(BF16)) * w
        for tp in range(seq):
            out = out + pw_new[tp] * vn[tp:tp + 1, :]
        tot = out
        for i in range(1, n_rows // seq):
            tot = tot + pltpu.roll(out, i * seq, axis=0)
        o_ref[h] = tot


def _finish_stages(x_ref, mixa_ref, b_ref, g_out_b_ref, wo_ref, g_ffn_ref, wg_ref, wu_ref, wd_ref, o_ref):
    slabs = _group_norm_rows([b_ref[s] for s in range(N_SLABS)], g_out_b_ref)
    mix_b = jnp.concatenate([a.astype(BF16) for a in slabs], axis=-1)
    x1 = x_ref[...] + _dot(mixa_ref[...], wo_ref[:WIDTH, :]) + _dot(mix_b, wo_ref[WIDTH:, :])
    ms = jnp.mean(x1 * x1, axis=-1, keepdims=True)
    h = (x1 * lax.rsqrt(ms + EPS) * g_ffn_ref[...]).astype(BF16)
    act = (jax.nn.silu(_dot(h, wg_ref[...])) * _dot(h, wu_ref[...])).astype(BF16)
    yield
    o_ref[...] = x1 + _dot(act, wd_ref[...])


def _params(sem):
    return pltpu.CompilerParams(dimension_semantics=sem, vmem_limit_bytes=VMEM_LIMIT)


_DONE = object()


def _staged_call(calls, order, name):
    steps = calls[0]["steps"]
    assert all(c["steps"] == steps for c in calls)
    counts = [(len(c["args"]), len(c["out_specs"]), len(c["scratch"])) for c in calls]

    def body(*refs):
        groups = []
        pos = 0
        for kind in range(3):
            per_call = []
            for cnt in counts:
                per_call.append(refs[pos:pos + cnt[kind]])
                pos += cnt[kind]
            groups.append(per_call)
        gens = [c["stages"](*groups[0][i], *groups[1][i], *groups[2][i]) for i, c in enumerate(calls)]
        for i in order:
            next(gens[i], None)
        assert all(next(g, _DONE) is _DONE for g in gens), "order leaves stages untraced"

    cat = lambda key: [item for c in calls for item in c[key]]
    outs = pl.pallas_call(
        body, grid=(steps,), in_specs=cat("in_specs"), out_specs=cat("out_specs"), out_shape=cat("out_shapes"),
        scratch_shapes=cat("scratch"), compiler_params=_params(("parallel",)), name=name,
    )(*cat("args"))
    split, pos = [], 0
    for _, n_out, _ in counts:
        split.append(outs[pos:pos + n_out])
        pos += n_out
    return split


def _shared_in_proj_args(g_attn, w_in, ln_v_g, ln_v_b, g_q, g_k):
    d_model, in_width = w_in.shape
    gq = (jnp.tile(g_q, N_HEADS) * (HEAD_DIM ** -0.5 * LOG2_E))[None]
    gk = jnp.tile(g_k, N_HEADS)[None]
    bd = jnp.asarray(np.kron(np.eye(N_HEADS), np.full((HEAD_DIM, HEAD_DIM), 1.0 / HEAD_DIM)), BF16)
    args = (g_attn[None], w_in.astype(BF16), ln_v_g[None], ln_v_b[None], gq, gk, bd)
    specs = [_const_spec((1, d_model)), _const_spec((d_model, in_width)), _const_spec((1, WIDTH)),
             _const_spec((1, WIDTH)), _const_spec((1, WIDTH)), _const_spec((1, WIDTH)),
             _const_spec((WIDTH, WIDTH))]
    return args, specs


def _in_proj_prompt_call(x, shared, w_s, b_s, g_out_a, *, tm):
    batch, seq, d_model = x.shape
    n = batch * seq
    nt = seq // tm
    shared_args, shared_specs = shared
    tri = np.tril(np.ones((CHUNK, CHUNK), bool))
    ws = jnp.where(tri, w_s, 0.0).astype(BF16)
    bs = jnp.repeat(b_s.T, HEAD_DIM, axis=1)
    slab = jax.ShapeDtypeStruct((batch, N_SLABS, seq, LANES), F32)
    flat = jax.ShapeDtypeStruct((batch, WIDTH, seq), F32)
    slab_spec = pl.BlockSpec((None, N_SLABS, tm, LANES), lambda i: (i // nt, 0, i % nt, 0))
    flat_spec = pl.BlockSpec((None, WIDTH, tm), lambda i: (i // nt, 0, i % nt))
    row_spec = pl.BlockSpec((tm, WIDTH), lambda i: (i, 0))
    return dict(
        stages=_in_proj_prompt_stages,
        args=(x.reshape(n, d_model), *shared_args, ws, bs, g_out_a[None]),
        in_specs=[pl.BlockSpec((tm, d_model), lambda i: (i, 0))] + shared_specs + [
            _const_spec((N_HEADS, CHUNK, CHUNK)), _const_spec((CHUNK, WIDTH)), _const_spec((1, WIDTH))],
        out_specs=[slab_spec, slab_spec, slab_spec, flat_spec, flat_spec, row_spec],
        out_shapes=[slab, slab, slab, flat, flat, jax.ShapeDtypeStruct((n, WIDTH), BF16)],
        scratch=[], steps=n // tm)


def _in_proj_sample(x, shared, w_s, b_s, g_out_a):
    n_seq, seq, d_model = x.shape
    n = n_seq * seq
    shared_args, shared_specs = shared
    sub = 8
    assert sub % seq == 0 and seq <= CHUNK
    step = np.arange(sub) % seq
    corner = w_s[:, :seq, :seq]
    taps = []
    for d in range(seq):
        src = np.maximum(step - d, 0)
        tap = jnp.where((step >= d)[:, None], corner[:, step, src].T, 0.0)
        taps.append(jnp.repeat(tap, HEAD_DIM, axis=1))
    coef = jnp.stack(taps)
    bias = jnp.repeat(b_s[:, step].T, HEAD_DIM, axis=1)
    flat = jax.ShapeDtypeStruct((n, WIDTH), F32)
    row_spec = pl.BlockSpec((n, WIDTH), lambda i: (0, 0))
    return pl.pallas_call(
        functools.partial(_in_proj_sample_kernel, seq=seq),
        grid=(1,),
        in_specs=[pl.BlockSpec((n, d_model), lambda i: (0, 0))] + shared_specs + [
            _const_spec((seq, sub, WIDTH)), _const_spec((sub, WIDTH)), _const_spec((1, WIDTH))],
        out_specs=[row_spec] * 5,
        out_shape=[flat, flat, flat, flat, jax.ShapeDtypeStruct((n, WIDTH), BF16)],
        compiler_params=_params(("arbitrary",)),
        name="in_proj_sample",
    )(x.reshape(n, d_model), *shared_args, coef, bias, g_out_a[None])


def _score_biases():
    row = (np.arange(2 * CHUNK) % CHUNK)[:, None]
    col = np.arange(2 * CHUNK)[None, :]
    in_band = (col >= row) & (col <= row + CHUNK)
    band = np.stack([in_band & (col >= CHUNK), in_band])
    causal = col[:, :CHUNK] <= row
    to_bias = lambda ok: jnp.asarray(np.where(ok, 0.0, -np.inf), F32)
    return to_bias(band), to_bias(causal)


def _prompt_attention(q, k, v):
    batch, _, seq, _ = q.shape
    n_blocks = seq // CHUNK
    assert all(n_blocks % dil == 0 for _, dil in DILATED_PATTERNS)
    spec = pl.BlockSpec((None, N_SLABS, seq, LANES), lambda b: (b, 0, 0, 0))
    band, causal = _score_biases()
    far_tiles = N_SLABS * DILATED_PATTERNS[2][1] // DILATED_PATTERNS[1][1]
    return pl.pallas_call(
        functools.partial(_prompt_attn_kernel, n_blocks=n_blocks),
        grid=(batch,),
        in_specs=[spec, spec, spec, _const_spec(band.shape), _const_spec(causal.shape)],
        out_specs=spec,
        out_shape=jax.ShapeDtypeStruct(q.shape, F32),
        scratch_shapes=[pltpu.VMEM((N_SLABS, seq, LANES), F32), pltpu.VMEM((N_SLABS, seq, LANES), F32),
                        pltpu.VMEM((N_SLABS, 2 * CHUNK, 2 * CHUNK), F32),
                        pltpu.VMEM((N_SLABS, 2 * CHUNK, 2 * CHUNK), BF16),
                        pltpu.VMEM((6, N_SLABS, seq // DILATED_PATTERNS[1][1], LANES), F32),
                        pltpu.VMEM((far_tiles, 2 * CHUNK, CHUNK), F32), pltpu.VMEM((far_tiles, 2 * CHUNK, CHUNK), BF16)],
        compiler_params=_params(("parallel",)),
        name="prompt_attention",
    )(q, k, v, band, causal)


def _sample_masks(seq, cache_len, past_len, n_rows):
    cache = np.zeros((n_rows, cache_len), np.float32)
    new = [[False] * n_rows for _ in range(seq)]
    for b, (window, dil) in enumerate(DILATED_PATTERNS):
        for t in range(seq):
            for jj in range(window // dil + 1):
                idx = cache_len + t - dil * jj
                if idx < 0 or idx + (past_len - cache_len) < 0:
                    continue
                if idx < cache_len:
                    cache[b * seq + t, idx] = 1.0
                else:
                    new[idx - cache_len][b * seq + t] = True
    return cache, new


def _sample_attention_operands(q, k, v, cache_k, cache_v):
    n_seq, seq, _ = q.shape
    cache_len = cache_k.shape[1]
    new = jnp.stack([q, k, v], axis=2).reshape(n_seq, seq, 3, N_HEADS, HEAD_DIM).transpose(0, 3, 2, 1, 4)
    new = jnp.pad(new, ((0, 0), (0, 0), (0, 0), (0, seq), (0, 0)))
    per_pair = lambda c: jnp.transpose(c, (0, 2, 3, 1)).reshape(n_seq * N_HEADS, HEAD_DIM, cache_len)
    return new.reshape(n_seq * N_HEADS, 3, 2 * seq, HEAD_DIM), per_pair(cache_k), per_pair(cache_v)


def _sample_attention_call(operands, seq, past_len, first, steps, per_step):
    new, kt, vt = operands
    n_rows = 4 * seq
    assert len(DILATED_PATTERNS) * seq <= n_rows and first % per_step == 0
    cache_len = kt.shape[2]
    mask, new_valid = _sample_masks(seq, cache_len, past_len, n_rows)
    block0 = first // per_step
    new_spec = pl.BlockSpec((per_step, 3, 2 * seq, HEAD_DIM), lambda i: (block0 + i, 0, 0, 0))
    cache_spec = pl.BlockSpec((per_step, HEAD_DIM, cache_len), lambda i: (block0 + i, 0, 0))
    return dict(
        stages=functools.partial(_sample_attn_stages, seq=seq, new_valid=new_valid),
        args=(new, kt, vt, jnp.asarray(mask)),
        in_specs=[new_spec, cache_spec, cache_spec, _const_spec((n_rows, cache_len))],
        out_specs=[pl.BlockSpec((per_step, n_rows, HEAD_DIM), lambda i: (i, 0, 0))],
        out_shapes=[jax.ShapeDtypeStruct((steps * per_step, n_rows, HEAD_DIM), F32)],
        scratch=[pltpu.VMEM((per_step, n_rows, cache_len), F32), pltpu.VMEM((per_step, n_rows, cache_len), BF16)],
        steps=steps)


def _sample_attention_slabs(outs, seq):
    out = jnp.concatenate(outs, axis=0)
    out = out.reshape(-1, N_HEADS, out.shape[1], HEAD_DIM)
    n_seq = out.shape[0]
    b_out = out[:, :, :seq, :].transpose(0, 2, 1, 3).reshape(n_seq * seq, N_SLABS, LANES)
    return b_out.transpose(1, 0, 2)[None]


def _finish_call(x2d, mix_a, b_slab, g_out_b, w_o, g_ffn, w_gate, w_up, w_down, *, tm):
    n, d_model = x2d.shape
    nt = b_slab.shape[2] // tm
    d_ff = w_gate.shape[1]
    return dict(
        stages=_finish_stages,
        args=(x2d, mix_a, b_slab, g_out_b[None], w_o.astype(BF16), g_ffn[None],
              w_gate.astype(BF16), w_up.astype(BF16), w_down.astype(BF16)),
        in_specs=[pl.BlockSpec((tm, d_model), lambda i: (i, 0)),
                  pl.BlockSpec((tm, WIDTH), lambda i: (i, 0)),
                  pl.BlockSpec((None, N_SLABS, tm, LANES), lambda i: (i // nt, 0, i % nt, 0)),
                  _const_spec((1, WIDTH)), _const_spec((2 * WIDTH, d_model)), _const_spec((1, d_model)),
                  _const_spec((d_model, d_ff)), _const_spec((d_model, d_ff)), _const_spec((d_ff, d_model))],
        out_specs=[pl.BlockSpec((tm, d_model), lambda i: (i, 0))],
        out_shapes=[jax.ShapeDtypeStruct((n, d_model), F32)],
        scratch=[], steps=n // tm)


def kernel(x_prompt, x_sample, cache_k, cache_v, g_attn, w_in, ln_v_g, ln_v_b, w_s, b_s, g_q, g_k,
           g_out_a, g_out_b, w_o, g_ffn, w_gate, w_up, w_down):
    depth = w_in.shape[0]
    assert depth == 1, "single-layer step"
    batch, seq, d_model = x_prompt.shape
    n_seq, dec_seq, _ = x_sample.shape
    l = 0
    shared = _shared_in_proj_args(g_attn[l], w_in[l], ln_v_g[l], ln_v_b[l], g_q[l], g_k[l])
    fin = (g_out_b[l], w_o[l], g_ffn[l], w_gate[l], w_up[l], w_down[l])
    n = batch * seq

    qs, ks, vs, vns, mix_as = _in_proj_sample(x_sample, shared, w_s[l], b_s[l], g_out_a[l])
    as3 = lambda a: a.reshape(n_seq, dec_seq, WIDTH)
    att_ops = _sample_attention_operands(as3(qs), as3(ks), as3(vs), cache_k[l], cache_v[l])
    (tm_in, pairs_in), (tm_fin, pairs_fin) = (512, 8), (256, 12)
    fin_pairs = (n // tm_fin) * pairs_fin
    assert fin_pairs + (n // tm_in) * pairs_in == n_seq * N_HEADS

    def hosted(host, first, per_step, name):
        att = _sample_attention_call(att_ops, dec_seq, PAST_LEN, first, host["steps"], per_step)
        host_outs, (att_out,) = _staged_call([host, att], [1, 0, 1, 1, 0], name)
        return host_outs, att_out

    in_proj = _in_proj_prompt_call(x_prompt, shared, w_s[l], b_s[l], g_out_a[l], tm=tm_in)
    (q, k, v, k_out, v_out, mix_a), att_in = hosted(in_proj, fin_pairs, pairs_in, "in_proj_prompt_and_sample_attention")
    b_slab = _prompt_attention(q, k, v)
    finish = _finish_call(x_prompt.reshape(n, d_model), mix_a, b_slab, *fin, tm=tm_fin)
    (y_prompt,), att_fin = hosted(finish, 0, pairs_fin, "finish_and_sample_attention")
    win = min(max(w for w, _ in DILATED_PATTERNS), seq)
    window = lambda a: a.reshape(batch, N_HEADS, HEAD_DIM, seq).transpose(0, 3, 1, 2)[:, -win:][None]

    bs_slab = _sample_attention_slabs([att_fin, att_in], dec_seq)
    finish_s = _finish_call(x_sample.reshape(n_seq * dec_seq, d_model), mix_as, bs_slab, *fin, tm=n_seq * dec_seq)
    ((y_sample,),) = _staged_call([finish_s], [0, 0], "finish")

    head_shape = (1, n_seq, dec_seq, N_HEADS, HEAD_DIM)
    return (y_prompt.reshape(batch, seq, d_model), y_sample.reshape(n_seq, dec_seq, d_model),
            window(k_out), window(v_out), ks.reshape(head_shape), vs.reshape(head_shape),
            vns.reshape(1, n_seq, dec_seq, WIDTH))
```

```python
import functools

import numpy as np
import jax
import jax.numpy as jnp
from jax import lax
from jax.experimental import pallas as pl
from jax.experimental.pallas import tpu as pltpu

F32 = jnp.float32
BF16 = jnp.bfloat16

HEAD_DIM = 64
N_HEADS = 8
WIDTH = N_HEADS * HEAD_DIM
CHUNK = 128
DILATED_PATTERNS = ((128, 1), (512, 4), (2048, 16))
EPS = 1e-6
PAST_LEN = 8192
NEG_BIG = -1e30
LOG2_E = 1.4426950408889634

LANES = 128
N_SLABS = WIDTH // LANES
VMEM_LIMIT = 56 * 1024 * 1024
IN_PROJ_HOSTING = (512, 8)
FINISH_HOSTING = (256, 12)


def _dot(a, b):
    return jnp.dot(a, b, preferred_element_type=F32)


def _dot_nt(a, b):
    return lax.dot_general(a, b, (((1,), (1,)), ((), ())), preferred_element_type=F32)


def _const_spec(shape):
    return pl.BlockSpec(shape, lambda *_: (0,) * len(shape), pipeline_mode=pl.Buffered(1))


def _head_rms(z, bd_ref, g_ref):
    msq = _dot((z * z).astype(BF16), bd_ref[...])
    return z * lax.rsqrt(msq + EPS) * g_ref[...]


def _features(x_ref, g_attn_ref, w_in_ref, ln_g_ref, ln_b_ref, gq_ref, gk_ref, bd_ref):
    x = x_ref[...]
    ms = jnp.mean(x * x, axis=-1, keepdims=True)
    xn = (x * lax.rsqrt(ms + EPS) * g_attn_ref[...]).astype(BF16)

    def proj(j):
        return _dot(xn, w_in_ref[:, j * WIDTH:(j + 1) * WIDTH])

    zu, zv, zq, zk, v = (proj(j) for j in range(5))
    u = jax.nn.gelu(zu)
    gv = jax.nn.gelu(zv)
    mu = jnp.mean(gv, axis=-1, keepdims=True)
    xc = gv - mu
    var = jnp.mean(xc * xc, axis=-1, keepdims=True)
    vn = xc * lax.rsqrt(var + EPS) * ln_g_ref[...] + ln_b_ref[...]
    q = _head_rms(zq, bd_ref, gq_ref)
    k = _head_rms(zk, bd_ref, gk_ref)
    return u, vn, q, k, v


def _group_norm_rows(slabs, g_ref):
    ssq = sum(jnp.sum(a * a, axis=-1, keepdims=True) for a in slabs)
    r = lax.rsqrt(ssq * (1.0 / WIDTH) + EPS)
    return [a * r * g_ref[:, s * LANES:(s + 1) * LANES] for s, a in enumerate(slabs)]


def _in_proj_prompt_stages(x_ref, g_attn_ref, w_in_ref, ln_g_ref, ln_b_ref, gq_ref, gk_ref, bd_ref,
                           ws_ref, bs_ref, g_out_a_ref,
                           q_ref, k_ref, v_ref, kout_ref, vout_ref, mixa_ref):
    u, vn, q, k, v = _features(x_ref, g_attn_ref, w_in_ref, ln_g_ref, ln_b_ref, gq_ref, gk_ref, bd_ref)
    yield
    tm = u.shape[0]
    first_head = lax.broadcasted_iota(jnp.int32, (CHUNK, LANES), 1) < HEAD_DIM
    for c in range(tm // CHUNK):
        rows = slice(c * CHUNK, (c + 1) * CHUNK)
        vn_c = vn[rows].astype(BF16)
        slabs = []
        for s in range(N_SLABS):
            cols = slice(s * LANES, (s + 1) * LANES)
            g0 = _dot(ws_ref[2 * s], vn_c[:, cols])
            g1 = _dot(ws_ref[2 * s + 1], vn_c[:, cols])
            gate = jnp.where(first_head, g0, g1) + bs_ref[:, cols]
            slabs.append(u[rows, cols] * gate)
        for s, a in enumerate(_group_norm_rows(slabs, g_out_a_ref)):
            mixa_ref[rows, s * LANES:(s + 1) * LANES] = a.astype(BF16)
    for s in range(N_SLABS):
        cols = slice(s * LANES, (s + 1) * LANES)
        q_ref[s] = q[:, cols]
        k_ref[s] = k[:, cols]
        v_ref[s] = v[:, cols]
    kout_ref[...] = k.T
    vout_ref[...] = v.T


def _in_proj_sample_kernel(x_ref, g_attn_ref, w_in_ref, ln_g_ref, ln_b_ref, gq_ref, gk_ref, bd_ref,
                           coef_ref, bias_ref, g_out_a_ref,
                           q_ref, k_ref, v_ref, vn_ref, mixa_ref, *, seq):
    u, vn, q, k, v = _features(x_ref, g_attn_ref, w_in_ref, ln_g_ref, ln_b_ref, gq_ref, gk_ref, bd_ref)
    tm = u.shape[0]
    sub = coef_ref.shape[1]
    gate = jnp.zeros((tm // sub, sub, WIDTH), F32) + bias_ref[...]
    for d in range(seq):
        shifted = vn if d == 0 else pltpu.roll(vn, d, axis=0)
        gate = gate + shifted.reshape(tm // sub, sub, WIDTH) * coef_ref[d]
    a = u * gate.reshape(tm, WIDTH)
    slabs = _group_norm_rows([a[:, s * LANES:(s + 1) * LANES] for s in range(N_SLABS)], g_out_a_ref)
    for s, a_s in enumerate(slabs):
        mixa_ref[:, s * LANES:(s + 1) * LANES] = a_s.astype(BF16)
    q_ref[...] = q
    k_ref[...] = k
    v_ref[...] = v
    vn_ref[...] = vn


def _prompt_attn_kernel(q_ref, k_ref, v_ref, band_ref, causal_ref, o_ref, m_sc, l_sc, s_sc, p_sc, grp_sc, sf_sc, pf_sc, *, n_blocks):
    (w_near, d_near), (w_mid, d_mid), (w_far, d_far) = DILATED_PATTERNS
    assert d_near == 1 and d_far == n_blocks and n_blocks % d_mid == 0
    assert w_near // d_near == w_mid // d_mid == w_far // d_far == CHUNK
    mid_blocks = n_blocks // d_mid
    sub = CHUNK // 2

    first_head = lax.broadcasted_iota(jnp.int32, (CHUNK, LANES), 1) < HEAD_DIM
    ones = jnp.ones((2 * CHUNK, LANES), BF16)

    both = lambda a: jnp.where(first_head, a[:CHUNK], a[CHUNK:])

    def keys_values(ref, s, cur, prev):
        a = ref[s, cur, :]
        return a if prev is None else jnp.concatenate([ref[s, prev, :], a], axis=0)

    def scores(s, cur, prev, src, s_tile):
        q_src, k_src, _ = src
        qb = q_src[s, cur, :].astype(BF16)
        zero = jnp.zeros_like(qb)
        lhs = jnp.concatenate([jnp.where(first_head, qb, zero), jnp.where(first_head, zero, qb)], axis=0)
        kk = keys_values(k_src, s, cur, prev).astype(BF16)
        s_tile[:, :kk.shape[0]] = _dot_nt(lhs, kk)

    def softmax(prev, has_prev, s_tile, p_tile):
        n_keys = CHUNK if prev is None else 2 * CHUNK
        m_parts = []
        for c in range(2 * CHUNK // sub):
            rows = slice(c * sub, (c + 1) * sub)
            bias = causal_ref[rows, :] if prev is None else band_ref[has_prev, rows, :]
            sc = s_tile[rows, :n_keys] + bias
            m = jnp.max(sc, axis=-1, keepdims=True)
            p_tile[rows, :n_keys] = jnp.exp2(sc - m).astype(BF16)
            m_parts.append(jnp.broadcast_to(m, (sub, LANES)))
        return both(jnp.concatenate(m_parts, axis=0))

    def values(s, cur, prev, src, p_tile):
        vv = keys_values(src[2], s, cur, prev).astype(BF16)
        n_keys = vv.shape[0]
        out = _dot(p_tile[:, :n_keys], jnp.concatenate([vv, ones[:n_keys]], axis=1))
        return both(out[:, LANES:]), both(out[:, :LANES])

    def attend(s, cur, prev, has_prev):
        src = (q_ref, k_ref, v_ref)
        scores(s, cur, prev, src, s_sc.at[s])
        m = softmax(prev, has_prev, s_sc.at[s], p_sc.at[s])
        return (m,) + values(s, cur, prev, src, p_sc.at[s])

    def merged(s, cur, stats):
        m_new, l_new, acc_new = stats
        m_old = m_sc[s, cur, :]
        m_tot = jnp.maximum(m_old, m_new)
        a_old = jnp.exp2(m_old - m_tot)
        a_new = jnp.exp2(m_new - m_tot)
        return (m_tot, a_old * l_sc[s, cur, :] + a_new * l_new, a_old * o_ref[s, cur, :] + a_new * acc_new)

    staged = [grp_sc.at[i] for i in range(6)]
    sub_classes = d_far // d_mid

    def far_group(r, carry):
        grp = pl.ds(r, o_ref.shape[1] // d_mid, stride=d_mid)
        for s in range(N_SLABS):
            for src, dst in zip((q_ref, k_ref, v_ref), staged[:3]):
                dst[s] = src[s, grp, :]
        blocks = [(c * N_SLABS + s, s, pl.ds(c, CHUNK, stride=sub_classes))
                  for c in range(sub_classes) for s in range(N_SLABS)]
        st_m, st_l, st_acc = staged[3:]
        for tile, s, cur in blocks:
            scores(s, cur, None, staged[:3], sf_sc.at[tile])
        for tile, s, cur in blocks:
            st_m[s, cur, :] = softmax(None, None, sf_sc.at[tile], pf_sc.at[tile])
        for tile, s, cur in blocks:
            st_l[s, cur, :], st_acc[s, cur, :] = values(s, cur, None, staged[:3], pf_sc.at[tile])
        for s in range(N_SLABS):
            for dst, src in zip((m_sc, l_sc, o_ref), staged[3:]):
                dst[s, grp, :] = src[s]
        return carry

    lax.fori_loop(0, d_mid, far_group, 0)

    def span(jb, carry):
        for r in range(d_mid):
            cur = pl.ds(r + (d_mid * CHUNK) * jb, CHUNK, stride=d_mid)
            prev = pl.ds(r + (d_mid * CHUNK) * jnp.maximum(jb - 1, 0), CHUNK, stride=d_mid)
            for s in range(N_SLABS):
                m_sc[s, cur, :], l_sc[s, cur, :], o_ref[s, cur, :] = merged(s, cur, attend(s, cur, prev, jnp.minimum(jb, 1)))
        for i in range(d_mid):
            j = jb * d_mid + i
            cur = pl.ds(pl.multiple_of(j * CHUNK, CHUNK), CHUNK)
            prev = pl.ds(pl.multiple_of(jnp.maximum(j - 1, 0) * CHUNK, CHUNK), CHUNK)
            for s in range(N_SLABS):
                _, l_tot, acc_tot = merged(s, cur, attend(s, cur, prev, jnp.minimum(j, 1)))
                o_ref[s, cur, :] = acc_tot / l_tot
        return carry

    lax.fori_loop(0, mid_blocks, span, 0)


def _sample_attn_stages(new_ref, kt_ref, vt_ref, mask_ref, o_ref, s_sc, p_sc, *, seq, new_valid):
    n_rows = mask_ref.shape[0]
    n_real = len(DILATED_PATTERNS) * seq
    assert new_ref.shape[2] == 2 * seq and n_rows == 4 * seq
    q_ref, kn_ref, vn_ref = (new_ref.at[:, i] for i in range(3))

    def branch_rows(a):
        return jnp.concatenate([a + pltpu.roll(a, seq, axis=0), a], axis=0)

    row = lax.broadcasted_iota(jnp.int32, (n_rows, 1), 0)
    real = row < n_real
    valid = mask_ref[...] > 0.5
    new_rows = []
    for tp in range(seq):
        sel = functools.reduce(jnp.logical_or, [row == r for r in range(n_rows) if new_valid[tp][r]])
        new_rows.append(sel)

    heads = range(q_ref.shape[0])
    for h in heads:
        sc = _dot(q_ref[h].astype(BF16), kt_ref[h].astype(BF16))
        s_sc[h] = jnp.where(valid, branch_rows(sc), -jnp.inf)
    yield

    weights = []
    for h in heads:
        q = branch_rows(q_ref[h])
        kn = kn_ref[h]
        s_new = [jnp.where(new_rows[tp], jnp.sum(q * kn[tp:tp + 1, :], axis=-1, keepdims=True), -jnp.inf)
                 for tp in range(seq)]
        sc = s_sc[h]
        m = jnp.max(sc, axis=-1, keepdims=True)
        for sn in s_new:
            m = jnp.maximum(m, sn)
        m = jnp.where(real, m, 0.0)
        p = jnp.exp2(sc - m)
        p_sc[h] = p.astype(BF16)
        p_new = [jnp.exp2(sn - m) for sn in s_new]
        l = jnp.sum(p, axis=-1, keepdims=True) + sum(p_new)
        m_eff = jnp.where(real, m, NEG_BIG)
        l_eff = jnp.where(real, l, 0.0)
        m_all = m_eff
        for i in range(1, n_rows // seq):
            m_all = jnp.maximum(m_all, pltpu.roll(m_eff, i * seq, axis=0))
        c = jnp.exp2(m_eff - m_all)
        cl = c * l_eff
        den = cl
        for i in range(1, n_rows // seq):
            den = den + pltpu.roll(cl, i * seq, axis=0)
        w = jnp.where(real, c / den, 0.0)
        weights.append((w, [pn * w for pn in p_new]))
    yield

    for h in heads:
        w, pw_new = weights[h]
        vn = vn_ref[h]
        out = _dot_nt(p_sc[h], vt_ref[h].astype(BF16)) * w
        for tp in range(seq):
            out = out + pw_new[tp] * vn[tp:tp + 1, :]
        tot = out
        for i in range(1, n_rows // seq):
            tot = tot + pltpu.roll(out, i * seq, axis=0)
        o_ref[h] = tot


def _finish_stages(x_ref, mixa_ref, b_ref, g_out_b_ref, wo_ref, g_ffn_ref, wg_ref, wu_ref, wd_ref, o_ref):
    slabs = _group_norm_rows([b_ref[s] for s in range(N_SLABS)], g_out_b_ref)
    mix_b = jnp.concatenate([a.astype(BF16) for a in slabs], axis=-1)
    x1 = x_ref[...] + _dot(mixa_ref[...], wo_ref[:WIDTH, :]) + _dot(mix_b, wo_ref[WIDTH:, :])
    ms = jnp.mean(x1 * x1, axis=-1, keepdims=True)
    h = (x1 * lax.rsqrt(ms + EPS) * g_ffn_ref[...]).astype(BF16)
    act = (jax.nn.silu(_dot(h, wg_ref[...])) * _dot(h, wu_ref[...])).astype(BF16)
    yield
    o_ref[...] = x1 + _dot(act, wd_ref[...])


def _params(sem):
    return pltpu.CompilerParams(dimension_semantics=sem, vmem_limit_bytes=VMEM_LIMIT)


_DONE = object()


def _staged_call(calls, order, name):
    steps = calls[0]["steps"]
    assert all(c["steps"] == steps for c in calls)
    counts = [(len(c["args"]), len(c["out_specs"]), len(c["scratch"])) for c in calls]

    def body(*refs):
        groups = []
        pos = 0
        for kind in range(3):
            per_call = []
            for cnt in counts:
                per_call.append(refs[pos:pos + cnt[kind]])
                pos += cnt[kind]
            groups.append(per_call)
        gens = [c["stages"](*groups[0][i], *groups[1][i], *groups[2][i]) for i, c in enumerate(calls)]
        for i in order:
            next(gens[i], None)
        assert all(next(g, _DONE) is _DONE for g in gens), "order leaves stages untraced"

    cat = lambda key: [item for c in calls for item in c[key]]
    outs = pl.pallas_call(
        body, grid=(steps,), in_specs=cat("in_specs"), out_specs=cat("out_specs"), out_shape=cat("out_shapes"),
        scratch_shapes=cat("scratch"), compiler_params=_params(("parallel",)), name=name,
    )(*cat("args"))
    split, pos = [], 0
    for _, n_out, _ in counts:
        split.append(outs[pos:pos + n_out])
        pos += n_out
    return split


def _shared_in_proj_args(g_attn, w_in, ln_v_g, ln_v_b, g_q, g_k):
    d_model, in_width = w_in.shape
    gq = (jnp.tile(g_q, N_HEADS) * (HEAD_DIM ** -0.5 * LOG2_E))[None]
    gk = jnp.tile(g_k, N_HEADS)[None]
    bd = jnp.asarray(np.kron(np.eye(N_HEADS), np.full((HEAD_DIM, HEAD_DIM), 1.0 / HEAD_DIM)), BF16)
    args = (g_attn[None], w_in.astype(BF16), ln_v_g[None], ln_v_b[None], gq, gk, bd)
    specs = [_const_spec((1, d_model)), _const_spec((d_model, in_width)), _const_spec((1, WIDTH)),
             _const_spec((1, WIDTH)), _const_spec((1, WIDTH)), _const_spec((1, WIDTH)),
             _const_spec((WIDTH, WIDTH))]
    return args, specs


def _in_proj_prompt_call(x, shared, w_s, b_s, g_out_a, *, tm):
    batch, seq, d_model = x.shape
    n = batch * seq
    nt = seq // tm
    shared_args, shared_specs = shared
    tri = np.tril(np.ones((CHUNK, CHUNK), bool))
    ws = jnp.where(tri, w_s, 0.0).astype(BF16)
    bs = jnp.repeat(b_s.T, HEAD_DIM, axis=1)
    slab = jax.ShapeDtypeStruct((batch, N_SLABS, seq, LANES), F32)
    flat = jax.ShapeDtypeStruct((batch, WIDTH, seq), F32)
    slab_spec = pl.BlockSpec((None, N_SLABS, tm, LANES), lambda i: (i // nt, 0, i % nt, 0))
    flat_spec = pl.BlockSpec((None, WIDTH, tm), lambda i: (i // nt, 0, i % nt))
    row_spec = pl.BlockSpec((tm, WIDTH), lambda i: (i, 0))
    return dict(
        stages=_in_proj_prompt_stages,
        args=(x.reshape(n, d_model), *shared_args, ws, bs, g_out_a[None]),
        in_specs=[pl.BlockSpec((tm, d_model), lambda i: (i, 0))] + shared_specs + [
            _const_spec((N_HEADS, CHUNK, CHUNK)), _const_spec((CHUNK, WIDTH)), _const_spec((1, WIDTH))],
        out_specs=[slab_spec, slab_spec, slab_spec, flat_spec, flat_spec, row_spec],
        out_shapes=[slab, slab, slab, flat, flat, jax.ShapeDtypeStruct((n, WIDTH), BF16)],
        scratch=[], steps=n // tm)


def _in_proj_sample(x, shared, w_s, b_s, g_out_a):
    n_seq, seq, d_model = x.shape
    n = n_seq * seq
    shared_args, shared_specs = shared
    sub = 8
    assert sub % seq == 0 and seq <= CHUNK
    step = np.arange(sub) % seq
    corner = w_s[:, :seq, :seq]
    taps = []
    for d in range(seq):
        src = np.maximum(step - d, 0)
        tap = jnp.where((step >= d)[:, None], corner[:, step, src].T, 0.0)
        taps.append(jnp.repeat(tap, HEAD_DIM, axis=1))
    coef = jnp.stack(taps)
    bias = jnp.repeat(b_s[:, step].T, HEAD_DIM, axis=1)
    flat = jax.ShapeDtypeStruct((n, WIDTH), F32)
    row_spec = pl.BlockSpec((n, WIDTH), lambda i: (0, 0))
    return pl.pallas_call(
        functools.partial(_in_proj_sample_kernel, seq=seq),
        grid=(1,),
        in_specs=[pl.BlockSpec((n, d_model), lambda i: (0, 0))] + shared_specs + [
            _const_spec((seq, sub, WIDTH)), _const_spec((sub, WIDTH)), _const_spec((1, WIDTH))],
        out_specs=[row_spec] * 5,
        out_shape=[flat, flat, flat, flat, jax.ShapeDtypeStruct((n, WIDTH), BF16)],
        compiler_params=_params(("arbitrary",)),
        name="in_proj_sample",
    )(x.reshape(n, d_model), *shared_args, coef, bias, g_out_a[None])


def _score_biases():
    row = (np.arange(2 * CHUNK) % CHUNK)[:, None]
    col = np.arange(2 * CHUNK)[None, :]
    in_band = (col >= row) & (col <= row + CHUNK)
    band = np.stack([in_band & (col >= CHUNK), in_band])
    causal = col[:, :CHUNK] <= row
    to_bias = lambda ok: jnp.asarray(np.where(ok, 0.0, -np.inf), F32)
    return to_bias(band), to_bias(causal)


def _prompt_attention(q, k, v):
    batch, _, seq, _ = q.shape
    n_blocks = seq // CHUNK
    assert all(n_blocks % dil == 0 for _, dil in DILATED_PATTERNS)
    spec = pl.BlockSpec((None, N_SLABS, seq, LANES), lambda b: (b, 0, 0, 0))
    band, causal = _score_biases()
    far_tiles = N_SLABS * DILATED_PATTERNS[2][1] // DILATED_PATTERNS[1][1]
    return pl.pallas_call(
        functools.partial(_prompt_attn_kernel, n_blocks=n_blocks),
        grid=(batch,),
        in_specs=[spec, spec, spec, _const_spec(band.shape), _const_spec(causal.shape)],
        out_specs=spec,
        out_shape=jax.ShapeDtypeStruct(q.shape, F32),
        scratch_shapes=[pltpu.VMEM((N_SLABS, seq, LANES), F32), pltpu.VMEM((N_SLABS, seq, LANES), F32),
                        pltpu.VMEM((N_SLABS, 2 * CHUNK, 2 * CHUNK), F32),
                        pltpu.VMEM((N_SLABS, 2 * CHUNK, 2 * CHUNK), BF16),
                        pltpu.VMEM((6, N_SLABS, seq // DILATED_PATTERNS[1][1], LANES), F32),
                        pltpu.VMEM((far_tiles, 2 * CHUNK, CHUNK), F32), pltpu.VMEM((far_tiles, 2 * CHUNK, CHUNK), BF16)],
        compiler_params=_params(("parallel",)),
        name="prompt_attention",
    )(q, k, v, band, causal)


def _sample_masks(seq, cache_len, past_len, n_rows):
    cache = np.zeros((n_rows, cache_len), np.float32)
    new = [[False] * n_rows for _ in range(seq)]
    for b, (window, dil) in enumerate(DILATED_PATTERNS):
        for t in range(seq):
            for jj in range(window // dil + 1):
                idx = cache_len + t - dil * jj
                if idx < 0 or idx + (past_len - cache_len) < 0:
                    continue
                if idx < cache_len:
                    cache[b * seq + t, idx] = 1.0
                else:
                    new[idx - cache_len][b * seq + t] = True
    return cache, new


def _sample_attention_operands(q, k, v, cache_k, cache_v):
    n_seq, seq, _ = q.shape
    cache_len = cache_k.shape[1]
    new = jnp.stack([q, k, v], axis=2).reshape(n_seq, seq, 3, N_HEADS, HEAD_DIM).transpose(0, 3, 2, 1, 4)
    new = jnp.pad(new, ((0, 0), (0, 0), (0, 0), (0, seq), (0, 0)))
    per_pair = lambda c: jnp.transpose(c, (0, 2, 3, 1)).reshape(n_seq * N_HEADS, HEAD_DIM, cache_len)
    return new.reshape(n_seq * N_HEADS, 3, 2 * seq, HEAD_DIM), per_pair(cache_k), per_pair(cache_v)


def _sample_attention_call(operands, seq, past_len, first, steps, per_step):
    new, kt, vt = operands
    n_rows = 4 * seq
    assert len(DILATED_PATTERNS) * seq <= n_rows and first % per_step == 0
    cache_len = kt.shape[2]
    mask, new_valid = _sample_masks(seq, cache_len, past_len, n_rows)
    block0 = first // per_step
    new_spec = pl.BlockSpec((per_step, 3, 2 * seq, HEAD_DIM), lambda i: (block0 + i, 0, 0, 0))
    cache_spec = pl.BlockSpec((per_step, HEAD_DIM, cache_len), lambda i: (block0 + i, 0, 0))
    return dict(
        stages=functools.partial(_sample_attn_stages, seq=seq, new_valid=new_valid),
        args=(new, kt, vt, jnp.asarray(mask)),
        in_specs=[new_spec, cache_spec, cache_spec, _const_spec((n_rows, cache_len))],
        out_specs=[pl.BlockSpec((per_step, n_rows, HEAD_DIM), lambda i: (i, 0, 0))],
        out_shapes=[jax.ShapeDtypeStruct((steps * per_step, n_rows, HEAD_DIM), F32)],
        scratch=[pltpu.VMEM((per_step, n_rows, cache_len), F32), pltpu.VMEM((per_step, n_rows, cache_len), BF16)],
        steps=steps)


def _sample_attention_slabs(outs, seq):
    out = jnp.concatenate(outs, axis=0)
    out = out.reshape(-1, N_HEADS, out.shape[1], HEAD_DIM)
    n_seq = out.shape[0]
    b_out = out[:, :, :seq, :].transpose(0, 2, 1, 3).reshape(n_seq * seq, N_SLABS, LANES)
    return b_out.transpose(1, 0, 2)[None]


def _finish_call(x2d, mix_a, b_slab, g_out_b, w_o, g_ffn, w_gate, w_up, w_down, *, tm):
    n, d_model = x2d.shape
    nt = b_slab.shape[2] // tm
    d_ff = w_gate.shape[1]
    return dict(
        stages=_finish_stages,
        args=(x2d, mix_a, b_slab, g_out_b[None], w_o.astype(BF16), g_ffn[None],
              w_gate.astype(BF16), w_up.astype(BF16), w_down.astype(BF16)),
        in_specs=[pl.BlockSpec((tm, d_model), lambda i: (i, 0)),
                  pl.BlockSpec((tm, WIDTH), lambda i: (i, 0)),
                  pl.BlockSpec((None, N_SLABS, tm, LANES), lambda i: (i // nt, 0, i % nt, 0)),
                  _const_spec((1, WIDTH)), _const_spec((2 * WIDTH, d_model)), _const_spec((1, d_model)),
                  _const_spec((d_model, d_ff)), _const_spec((d_model, d_ff)), _const_spec((d_ff, d_model))],
        out_specs=[pl.BlockSpec((tm, d_model), lambda i: (i, 0))],
        out_shapes=[jax.ShapeDtypeStruct((n, d_model), F32)],
        scratch=[], steps=n // tm)


def kernel(x_prompt, x_sample, cache_k, cache_v, g_attn, w_in, ln_v_g, ln_v_b, w_s, b_s, g_q, g_k,
           g_out_a, g_out_b, w_o, g_ffn, w_gate, w_up, w_down):
    depth = w_in.shape[0]
    assert depth == 1, "single-layer step"
    batch, seq, d_model = x_prompt.shape
    n_seq, dec_seq, _ = x_sample.shape
    l = 0
    shared = _shared_in_proj_args(g_attn[l], w_in[l], ln_v_g[l], ln_v_b[l], g_q[l], g_k[l])
    fin = (g_out_b[l], w_o[l], g_ffn[l], w_gate[l], w_up[l], w_down[l])
    n = batch * seq

    qs, ks, vs, vns, mix_as = _in_proj_sample(x_sample, shared, w_s[l], b_s[l], g_out_a[l])
    as3 = lambda a: a.reshape(n_seq, dec_seq, WIDTH)
    att_ops = _sample_attention_operands(as3(qs), as3(ks), as3(vs), cache_k[l], cache_v[l])
    (tm_in, pairs_in), (tm_fin, pairs_fin) = IN_PROJ_HOSTING, FINISH_HOSTING
    fin_pairs = (n // tm_fin) * pairs_fin
    assert fin_pairs + (n // tm_in) * pairs_in == n_seq * N_HEADS

    def hosted(host, first, per_step, name):
        att = _sample_attention_call(att_ops, dec_seq, PAST_LEN, first, host["steps"], per_step)
        host_outs, (att_out,) = _staged_call([host, att], [1, 0, 1, 0, 1], name)
        return host_outs, att_out

    in_proj = _in_proj_prompt_call(x_prompt, shared, w_s[l], b_s[l], g_out_a[l], tm=tm_in)
    (q, k, v, k_out, v_out, mix_a), att_in = hosted(in_proj, fin_pairs, pairs_in, "in_proj_prompt_and_sample_attention")
    b_slab = _prompt_attention(q, k, v)
    finish = _finish_call(x_prompt.reshape(n, d_model), mix_a, b_slab, *fin, tm=tm_fin)
    (y_prompt,), att_fin = hosted(finish, 0, pairs_fin, "finish_and_sample_attention")
    win = min(max(w for w, _ in DILATED_PATTERNS), seq)
    window = lambda a: a.reshape(batch, N_HEADS, HEAD_DIM, seq).transpose(0, 3, 1, 2)[:, -win:][None]

    bs_slab = _sample_attention_slabs([att_fin, att_in], dec_seq)
    finish_s = _finish_call(x_sample.reshape(n_seq * dec_seq, d_model), mix_as, bs_slab, *fin, tm=n_seq * dec_seq)
    ((y_sample,),) = _staged_call([finish_s], [0, 0], "finish")

    head_shape = (1, n_seq, dec_seq, N_HEADS, HEAD_DIM)
    return (y_prompt.reshape(batch, seq, d_model), y_sample.reshape(n_seq, dec_seq, d_model),
            window(k_out), window(v_out), ks.reshape(head_shape), vs.reshape(head_shape),
            vns.reshape(1, n_seq, dec_seq, WIDTH))
```

```python
import functools

import numpy as np
import jax
import jax.numpy as jnp
from jax import lax
from jax.experimental import pallas as pl
from jax.experimental.pallas import tpu as pltpu

F32 = jnp.float32
BF16 = jnp.bfloat16

HEAD_DIM = 64
N_HEADS = 8
WIDTH = N_HEADS * HEAD_DIM
CHUNK = 128
DILATED_PATTERNS = ((128, 1), (512, 4), (2048, 16))
EPS = 1e-6
PAST_LEN = 8192
NEG_BIG = -1e30
LOG2_E = 1.4426950408889634

LANES = 128
N_SLABS = WIDTH // LANES
VMEM_LIMIT = 60 * 1024 * 1024
IN_PROJ_HOSTING = (512, 4)
FINISH_HOSTING = (256, 14)


def _dot(a, b):
    return jnp.dot(a, b, preferred_element_type=F32)


def _dot_nt(a, b):
    return lax.dot_general(a, b, (((1,), (1,)), ((), ())), preferred_element_type=F32)


def _const_spec(shape):
    return pl.BlockSpec(shape, lambda *_: (0,) * len(shape), pipeline_mode=pl.Buffered(1))


def _head_rms(z, bd_ref, g_ref):
    msq = _dot((z * z).astype(BF16), bd_ref[...])
    return z * lax.rsqrt(msq + EPS) * g_ref[...]


def _features(x_ref, g_attn_ref, w_in_ref, ln_g_ref, ln_b_ref, gq_ref, gk_ref, bd_ref):
    x = x_ref[...]
    ms = jnp.mean(x * x, axis=-1, keepdims=True)
    xn = (x * lax.rsqrt(ms + EPS) * g_attn_ref[...]).astype(BF16)

    def proj(j):
        return _dot(xn, w_in_ref[:, j * WIDTH:(j + 1) * WIDTH])

    zu, zv, zq, zk, v = (proj(j) for j in range(5))
    u = jax.nn.gelu(zu)
    gv = jax.nn.gelu(zv)
    mu = jnp.mean(gv, axis=-1, keepdims=True)
    xc = gv - mu
    var = jnp.mean(xc * xc, axis=-1, keepdims=True)
    vn = xc * lax.rsqrt(var + EPS) * ln_g_ref[...] + ln_b_ref[...]
    q = _head_rms(zq, bd_ref, gq_ref)
    k = _head_rms(zk, bd_ref, gk_ref)
    return u, vn, q, k, v


def _group_norm_rows(slabs, g_ref):
    ssq = sum(jnp.sum(a * a, axis=-1, keepdims=True) for a in slabs)
    r = lax.rsqrt(ssq * (1.0 / WIDTH) + EPS)
    return [a * r * g_ref[:, s * LANES:(s + 1) * LANES] for s, a in enumerate(slabs)]


def _in_proj_prompt_stages(x_ref, g_attn_ref, w_in_ref, ln_g_ref, ln_b_ref, gq_ref, gk_ref, bd_ref,
                           ws_ref, bs_ref, g_out_a_ref,
                           q_ref, k_ref, v_ref, kout_ref, vout_ref, mixa_ref):
    u, vn, q, k, v = _features(x_ref, g_attn_ref, w_in_ref, ln_g_ref, ln_b_ref, gq_ref, gk_ref, bd_ref)
    yield
    tm = u.shape[0]
    first_head = lax.broadcasted_iota(jnp.int32, (CHUNK, LANES), 1) < HEAD_DIM
    for c in range(tm // CHUNK):
        rows = slice(c * CHUNK, (c + 1) * CHUNK)
        vn_c = vn[rows].astype(BF16)
        slabs = []
        for s in range(N_SLABS):
            cols = slice(s * LANES, (s + 1) * LANES)
            g0 = _dot(ws_ref[2 * s], vn_c[:, cols])
            g1 = _dot(ws_ref[2 * s + 1], vn_c[:, cols])
            gate = jnp.where(first_head, g0, g1) + bs_ref[:, cols]
            slabs.append(u[rows, cols] * gate)
        for s, a in enumerate(_group_norm_rows(slabs, g_out_a_ref)):
            mixa_ref[rows, s * LANES:(s + 1) * LANES] = a.astype(BF16)
    for s in range(N_SLABS):
        cols = slice(s * LANES, (s + 1) * LANES)
        q_ref[s] = q[:, cols]
        k_ref[s] = k[:, cols]
        v_ref[s] = v[:, cols]
    kout_ref[...] = k.T
    vout_ref[...] = v.T


def _in_proj_sample_kernel(x_ref, g_attn_ref, w_in_ref, ln_g_ref, ln_b_ref, gq_ref, gk_ref, bd_ref,
                           coef_ref, bias_ref, g_out_a_ref,
                           q_ref, k_ref, v_ref, vn_ref, mixa_ref, *, seq):
    u, vn, q, k, v = _features(x_ref, g_attn_ref, w_in_ref, ln_g_ref, ln_b_ref, gq_ref, gk_ref, bd_ref)
    tm = u.shape[0]
    sub = coef_ref.shape[1]
    gate = jnp.zeros((tm // sub, sub, WIDTH), F32) + bias_ref[...]
    for d in range(seq):
        shifted = vn if d == 0 else pltpu.roll(vn, d, axis=0)
        gate = gate + shifted.reshape(tm // sub, sub, WIDTH) * coef_ref[d]
    a = u * gate.reshape(tm, WIDTH)
    slabs = _group_norm_rows([a[:, s * LANES:(s + 1) * LANES] for s in range(N_SLABS)], g_out_a_ref)
    for s, a_s in enumerate(slabs):
        mixa_ref[:, s * LANES:(s + 1) * LANES] = a_s.astype(BF16)
    q_ref[...] = q
    k_ref[...] = k
    v_ref[...] = v
    vn_ref[...] = vn


def _prompt_attn_kernel(q_ref, k_ref, v_ref, band_ref, causal_ref, o_ref, m_sc, l_sc, s_sc, p_sc, grp_sc, sf_sc, pf_sc, *, n_blocks):
    (w_near, d_near), (w_mid, d_mid), (w_far, d_far) = DILATED_PATTERNS
    assert d_near == 1 and d_far == n_blocks and n_blocks % d_mid == 0
    assert w_near // d_near == w_mid // d_mid == w_far // d_far == CHUNK
    mid_blocks = n_blocks // d_mid
    sub = CHUNK // 2

    first_head = lax.broadcasted_iota(jnp.int32, (CHUNK, LANES), 1) < HEAD_DIM
    ones = jnp.ones((2 * CHUNK, LANES), BF16)

    both = lambda a: jnp.where(first_head, a[:CHUNK], a[CHUNK:])

    def keys_values(ref, s, cur, prev):
        a = ref[s, cur, :]
        return a if prev is None else jnp.concatenate([ref[s, prev, :], a], axis=0)

    def scores(s, cur, prev, src, s_tile):
        q_src, k_src, _ = src
        qb = q_src[s, cur, :].astype(BF16)
        zero = jnp.zeros_like(qb)
        lhs = jnp.concatenate([jnp.where(first_head, qb, zero), jnp.where(first_head, zero, qb)], axis=0)
        kk = keys_values(k_src, s, cur, prev).astype(BF16)
        s_tile[:, :kk.shape[0]] = _dot_nt(lhs, kk)

    def softmax(prev, has_prev, s_tile, p_tile):
        n_keys = CHUNK if prev is None else 2 * CHUNK
        m_parts = []
        for c in range(2 * CHUNK // sub):
            rows = slice(c * sub, (c + 1) * sub)
            bias = causal_ref[rows, :] if prev is None else band_ref[has_prev, rows, :]
            sc = s_tile[rows, :n_keys] + bias
            m = jnp.max(sc, axis=-1, keepdims=True)
            p_tile[rows, :n_keys] = jnp.exp2(sc - m).astype(BF16)
            m_parts.append(jnp.broadcast_to(m, (sub, LANES)))
        return both(jnp.concatenate(m_parts, axis=0))

    def values(s, cur, prev, src, p_tile):
        vv = keys_values(src[2], s, cur, prev).astype(BF16)
        n_keys = vv.shape[0]
        out = _dot(p_tile[:, :n_keys], jnp.concatenate([vv, ones[:n_keys]], axis=1))
        return both(out[:, LANES:]), both(out[:, :LANES])

    def attend(s, cur, prev, has_prev):
        src = (q_ref, k_ref, v_ref)
        scores(s, cur, prev, src, s_sc.at[s])
        m = softmax(prev, has_prev, s_sc.at[s], p_sc.at[s])
        return (m,) + values(s, cur, prev, src, p_sc.at[s])

    def merged(s, cur, stats):
        m_new, l_new, acc_new = stats
        m_old = m_sc[s, cur, :]
        m_tot = jnp.maximum(m_old, m_new)
        a_old = jnp.exp2(m_old - m_tot)
        a_new = jnp.exp2(m_new - m_tot)
        return (m_tot, a_old * l_sc[s, cur, :] + a_new * l_new, a_old * o_ref[s, cur, :] + a_new * acc_new)

    staged = [grp_sc.at[i] for i in range(6)]
    sub_classes = d_far // d_mid

    def far_group(r, carry):
        grp = pl.ds(r, o_ref.shape[1] // d_mid, stride=d_mid)
        for s in range(N_SLABS):
            for src, dst in zip((q_ref, k_ref, v_ref), staged[:3]):
                dst[s] = src[s, grp, :]
        blocks = [(c * N_SLABS + s, s, pl.ds(c, CHUNK, stride=sub_classes))
                  for c in range(sub_classes) for s in range(N_SLABS)]
        st_m, st_l, st_acc = staged[3:]
        for tile, s, cur in blocks:
            scores(s, cur, None, staged[:3], sf_sc.at[tile])
        for tile, s, cur in blocks:
            st_m[s, cur, :] = softmax(None, None, sf_sc.at[tile], pf_sc.at[tile])
        for tile, s, cur in blocks:
            st_l[s, cur, :], st_acc[s, cur, :] = values(s, cur, None, staged[:3], pf_sc.at[tile])
        for s in range(N_SLABS):
            for dst, src in zip((m_sc, l_sc, o_ref), staged[3:]):
                dst[s, grp, :] = src[s]
        return carry

    lax.fori_loop(0, d_mid, far_group, 0)

    def span(jb, carry):
        for r in range(d_mid):
            cur = pl.ds(r + (d_mid * CHUNK) * jb, CHUNK, stride=d_mid)
            prev = pl.ds(r + (d_mid * CHUNK) * jnp.maximum(jb - 1, 0), CHUNK, stride=d_mid)
            for s in range(N_SLABS):
                m_sc[s, cur, :], l_sc[s, cur, :], o_ref[s, cur, :] = merged(s, cur, attend(s, cur, prev, jnp.minimum(jb, 1)))
        for i in range(d_mid):
            j = jb * d_mid + i
            cur = pl.ds(pl.multiple_of(j * CHUNK, CHUNK), CHUNK)
            prev = pl.ds(pl.multiple_of(jnp.maximum(j - 1, 0) * CHUNK, CHUNK), CHUNK)
            for s in range(N_SLABS):
                _, l_tot, acc_tot = merged(s, cur, attend(s, cur, prev, jnp.minimum(j, 1)))
                o_ref[s, cur, :] = acc_tot / l_tot
        return carry

    lax.fori_loop(0, mid_blocks, span, 0)


def _sample_attn_stages(new_ref, kt_ref, vt_ref, mask_ref, o_ref, s_sc, p_sc, *, seq, new_valid):
    n_rows = mask_ref.shape[0]
    n_real = len(DILATED_PATTERNS) * seq
    assert new_ref.shape[2] == 2 * seq and n_rows == 4 * seq
    q_ref, kn_ref, vn_ref = (new_ref.at[:, i] for i in range(3))

    def branch_rows(a):
        return jnp.concatenate([a + pltpu.roll(a, seq, axis=0), a], axis=0)

    row = lax.broadcasted_iota(jnp.int32, (n_rows, 1), 0)
    real = row < n_real
    valid = mask_ref[...] > 0.5
    new_rows = []
    for tp in range(seq):
        sel = functools.reduce(jnp.logical_or, [row == r for r in range(n_rows) if new_valid[tp][r]])
        new_rows.append(sel)

    heads = range(q_ref.shape[0])
    for h in heads:
        sc = _dot(q_ref[h].astype(BF16), kt_ref[h].astype(BF16))
        s_sc[h] = jnp.where(valid, branch_rows(sc), -jnp.inf)
    yield

    weights = []
    for h in heads:
        q = branch_rows(q_ref[h])
        kn = kn_ref[h]
        s_new = [jnp.where(new_rows[tp], jnp.sum(q * kn[tp:tp + 1, :], axis=-1, keepdims=True), -jnp.inf)
                 for tp in range(seq)]
        sc = s_sc[h]
        m = jnp.max(sc, axis=-1, keepdims=True)
        for sn in s_new:
            m = jnp.maximum(m, sn)
        m = jnp.where(real, m, 0.0)
        p = jnp.exp2(sc - m)
        p_sc[h] = p.astype(BF16)
        p_new = [jnp.exp2(sn - m) for sn in s_new]
        l = jnp.sum(p, axis=-1, keepdims=True) + sum(p_new)
        m_eff = jnp.where(real, m, NEG_BIG)
        l_eff = jnp.where(real, l, 0.0)
        m_all = m_eff
        for i in range(1, n_rows // seq):
            m_all = jnp.maximum(m_all, pltpu.roll(m_eff, i * seq, axis=0))
        c = jnp.exp2(m_eff - m_all)
        cl = c * l_eff
        den = cl
        for i in range(1, n_rows // seq):
            den = den + pltpu.roll(cl, i * seq, axis=0)
        w = jnp.where(real, c / den, 0.0)
        weights.append((w, [pn * w for pn in p_new]))
    yield

    for h in heads:
        w, pw_new = weights[h]
        vn = vn_ref[h]
        out = _dot_nt(p_sc[h], vt_ref[h].astype(BF16)) * w
        for tp in range(seq):
            out = out + pw_new[tp] * vn[tp:tp + 1, :]
        tot = out
        for i in range(1, n_rows // seq):
            tot = tot + pltpu.roll(out, i * seq, axis=0)
        o_ref[h] = tot


def _finish_stages(x_ref, mixa_ref, b_ref, g_out_b_ref, wo_ref, g_ffn_ref, wg_ref, wu_ref, wd_ref, o_ref):
    slabs = _group_norm_rows([b_ref[s] for s in range(N_SLABS)], g_out_b_ref)
    mix_b = jnp.concatenate([a.astype(BF16) for a in slabs], axis=-1)
    x1 = x_ref[...] + _dot(mixa_ref[...], wo_ref[:WIDTH, :]) + _dot(mix_b, wo_ref[WIDTH:, :])
    ms = jnp.mean(x1 * x1, axis=-1, keepdims=True)
    h = (x1 * lax.rsqrt(ms + EPS) * g_ffn_ref[...]).astype(BF16)
    act = (jax.nn.silu(_dot(h, wg_ref[...])) * _dot(h, wu_ref[...])).astype(BF16)
    yield
    o_ref[...] = x1 + _dot(act, wd_ref[...])


def _params(sem):
    return pltpu.CompilerParams(dimension_semantics=sem, vmem_limit_bytes=VMEM_LIMIT)


_DONE = object()


def _staged_call(calls, order, name):
    steps = calls[0]["steps"]
    assert all(c["steps"] == steps for c in calls)
    counts = [(len(c["args"]), len(c["out_specs"]), len(c["scratch"])) for c in calls]

    def body(*refs):
        groups = []
        pos = 0
        for kind in range(3):
            per_call = []
            for cnt in counts:
                per_call.append(refs[pos:pos + cnt[kind]])
                pos += cnt[kind]
            groups.append(per_call)
        gens = [c["stages"](*groups[0][i], *groups[1][i], *groups[2][i]) for i, c in enumerate(calls)]
        for i in order:
            next(gens[i], None)
        assert all(next(g, _DONE) is _DONE for g in gens), "order leaves stages untraced"

    cat = lambda key: [item for c in calls for item in c[key]]
    outs = pl.pallas_call(
        body, grid=(steps,), in_specs=cat("in_specs"), out_specs=cat("out_specs"), out_shape=cat("out_shapes"),
        scratch_shapes=cat("scratch"), compiler_params=_params(("parallel",)), name=name,
    )(*cat("args"))
    split, pos = [], 0
    for _, n_out, _ in counts:
        split.append(outs[pos:pos + n_out])
        pos += n_out
    return split


def _shared_in_proj_args(g_attn, w_in, ln_v_g, ln_v_b, g_q, g_k):
    d_model, in_width = w_in.shape
    gq = (jnp.tile(g_q, N_HEADS) * (HEAD_DIM ** -0.5 * LOG2_E))[None]
    gk = jnp.tile(g_k, N_HEADS)[None]
    bd = jnp.asarray(np.kron(np.eye(N_HEADS), np.full((HEAD_DIM, HEAD_DIM), 1.0 / HEAD_DIM)), BF16)
    args = (g_attn[None], w_in.astype(BF16), ln_v_g[None], ln_v_b[None], gq, gk, bd)
    specs = [_const_spec((1, d_model)), _const_spec((d_model, in_width)), _const_spec((1, WIDTH)),
             _const_spec((1, WIDTH)), _const_spec((1, WIDTH)), _const_spec((1, WIDTH)),
             _const_spec((WIDTH, WIDTH))]
    return args, specs


def _in_proj_prompt_call(x, shared, w_s, b_s, g_out_a, *, tm):
    batch, seq, d_model = x.shape
    n = batch * seq
    nt = seq // tm
    shared_args, shared_specs = shared
    tri = np.tril(np.ones((CHUNK, CHUNK), bool))
    ws = jnp.where(tri, w_s, 0.0).astype(BF16)
    bs = jnp.repeat(b_s.T, HEAD_DIM, axis=1)
    slab = jax.ShapeDtypeStruct((batch, N_SLABS, seq, LANES), F32)
    flat = jax.ShapeDtypeStruct((batch, WIDTH, seq), F32)
    slab_spec = pl.BlockSpec((None, N_SLABS, tm, LANES), lambda i: (i // nt, 0, i % nt, 0))
    flat_spec = pl.BlockSpec((None, WIDTH, tm), lambda i: (i // nt, 0, i % nt))
    row_spec = pl.BlockSpec((tm, WIDTH), lambda i: (i, 0))
    return dict(
        stages=_in_proj_prompt_stages,
        args=(x.reshape(n, d_model), *shared_args, ws, bs, g_out_a[None]),
        in_specs=[pl.BlockSpec((tm, d_model), lambda i: (i, 0))] + shared_specs + [
            _const_spec((N_HEADS, CHUNK, CHUNK)), _const_spec((CHUNK, WIDTH)), _const_spec((1, WIDTH))],
        out_specs=[slab_spec, slab_spec, slab_spec, flat_spec, flat_spec, row_spec],
        out_shapes=[slab, slab, slab, flat, flat, jax.ShapeDtypeStruct((n, WIDTH), BF16)],
        scratch=[], steps=n // tm)


def _in_proj_sample(x, shared, w_s, b_s, g_out_a):
    n_seq, seq, d_model = x.shape
    n = n_seq * seq
    shared_args, shared_specs = shared
    sub = 8
    assert sub % seq == 0 and seq <= CHUNK
    step = np.arange(sub) % seq
    corner = w_s[:, :seq, :seq]
    taps = []
    for d in range(seq):
        src = np.maximum(step - d, 0)
        tap = jnp.where((step >= d)[:, None], corner[:, step, src].T, 0.0)
        taps.append(jnp.repeat(tap, HEAD_DIM, axis=1))
    coef = jnp.stack(taps)
    bias = jnp.repeat(b_s[:, step].T, HEAD_DIM, axis=1)
    flat = jax.ShapeDtypeStruct((n, WIDTH), F32)
    row_spec = pl.BlockSpec((n, WIDTH), lambda i: (0, 0))
    return pl.pallas_call(
        functools.partial(_in_proj_sample_kernel, seq=seq),
        grid=(1,),
        in_specs=[pl.BlockSpec((n, d_model), lambda i: (0, 0))] + shared_specs + [
            _const_spec((seq, sub, WIDTH)), _const_spec((sub, WIDTH)), _const_spec((1, WIDTH))],
        out_specs=[row_spec] * 5,
        out_shape=[flat, flat, flat, flat, jax.ShapeDtypeStruct((n, WIDTH), BF16)],
        compiler_params=_params(("arbitrary",)),
        name="in_proj_sample",
    )(x.reshape(n, d_model), *shared_args, coef, bias, g_out_a[None])


def _score_biases():
    row = (np.arange(2 * CHUNK) % CHUNK)[:, None]
    col = np.arange(2 * CHUNK)[None, :]
    in_band = (col >= row) & (col <= row + CHUNK)
    band = np.stack([in_band & (col >= CHUNK), in_band])
    causal = col[:, :CHUNK] <= row
    to_bias = lambda ok: jnp.asarray(np.where(ok, 0.0, -np.inf), F32)
    return to_bias(band), to_bias(causal)


def _prompt_attention(q, k, v):
    batch, _, seq, _ = q.shape
    n_blocks = seq // CHUNK
    assert all(n_blocks % dil == 0 for _, dil in DILATED_PATTERNS)
    spec = pl.BlockSpec((None, N_SLABS, seq, LANES), lambda b: (b, 0, 0, 0))
    band, causal = _score_biases()
    far_tiles = N_SLABS * DILATED_PATTERNS[2][1] // DILATED_PATTERNS[1][1]
    return pl.pallas_call(
        functools.partial(_prompt_attn_kernel, n_blocks=n_blocks),
        grid=(batch,),
        in_specs=[spec, spec, spec, _const_spec(band.shape), _const_spec(causal.shape)],
        out_specs=spec,
        out_shape=jax.ShapeDtypeStruct(q.shape, F32),
        scratch_shapes=[pltpu.VMEM((N_SLABS, seq, LANES), F32), pltpu.VMEM((N_SLABS, seq, LANES), F32),
                        pltpu.VMEM((N_SLABS, 2 * CHUNK, 2 * CHUNK), F32),
                        pltpu.VMEM((N_SLABS, 2 * CHUNK, 2 * CHUNK), BF16),
                        pltpu.VMEM((6, N_SLABS, seq // DILATED_PATTERNS[1][1], LANES), F32),
                        pltpu.VMEM((far_tiles, 2 * CHUNK, CHUNK), F32), pltpu.VMEM((far_tiles, 2 * CHUNK, CHUNK), BF16)],
        compiler_params=_params(("parallel",)),
        name="prompt_attention",
    )(q, k, v, band, causal)


def _sample_masks(seq, cache_len, past_len, n_rows):
    cache = np.zeros((n_rows, cache_len), np.float32)
    new = [[False] * n_rows for _ in range(seq)]
    for b, (window, dil) in enumerate(DILATED_PATTERNS):
        for t in range(seq):
            for jj in range(window // dil + 1):
                idx = cache_len + t - dil * jj
                if idx < 0 or idx + (past_len - cache_len) < 0:
                    continue
                if idx < cache_len:
                    cache[b * seq + t, idx] = 1.0
                else:
                    new[idx - cache_len][b * seq + t] = True
    return cache, new


def _sample_attention_operands(q, k, v, cache_k, cache_v):
    n_seq, seq, _ = q.shape
    cache_len = cache_k.shape[1]
    new = jnp.stack([q, k, v], axis=2).reshape(n_seq, seq, 3, N_HEADS, HEAD_DIM).transpose(0, 3, 2, 1, 4)
    new = jnp.pad(new, ((0, 0), (0, 0), (0, 0), (0, seq), (0, 0)))
    per_pair = lambda c: jnp.transpose(c, (0, 2, 3, 1)).reshape(n_seq * N_HEADS, HEAD_DIM, cache_len)
    return new.reshape(n_seq * N_HEADS, 3, 2 * seq, HEAD_DIM), per_pair(cache_k), per_pair(cache_v)


def _sample_attention_call(operands, seq, past_len, first, steps, per_step):
    new, kt, vt = operands
    n_rows = 4 * seq
    assert len(DILATED_PATTERNS) * seq <= n_rows and first % per_step == 0
    cache_len = kt.shape[2]
    mask, new_valid = _sample_masks(seq, cache_len, past_len, n_rows)
    block0 = first // per_step
    new_spec = pl.BlockSpec((per_step, 3, 2 * seq, HEAD_DIM), lambda i: (block0 + i, 0, 0, 0))
    cache_spec = pl.BlockSpec((per_step, HEAD_DIM, cache_len), lambda i: (block0 + i, 0, 0))
    return dict(
        stages=functools.partial(_sample_attn_stages, seq=seq, new_valid=new_valid),
        args=(new, kt, vt, jnp.asarray(mask)),
        in_specs=[new_spec, cache_spec, cache_spec, _const_spec((n_rows, cache_len))],
        out_specs=[pl.BlockSpec((per_step, n_rows, HEAD_DIM), lambda i: (i, 0, 0))],
        out_shapes=[jax.ShapeDtypeStruct((steps * per_step, n_rows, HEAD_DIM), F32)],
        scratch=[pltpu.VMEM((per_step, n_rows, cache_len), F32), pltpu.VMEM((per_step, n_rows, cache_len), BF16)],
        steps=steps)


def _sample_attention_slabs(outs, seq):
    out = jnp.concatenate(outs, axis=0)
    out = out.reshape(-1, N_HEADS, out.shape[1], HEAD_DIM)
    n_seq = out.shape[0]
    b_out = out[:, :, :seq, :].transpose(0, 2, 1, 3).reshape(n_seq * seq, N_SLABS, LANES)
    return b_out.transpose(1, 0, 2)[None]


def _finish_call(x2d, mix_a, b_slab, g_out_b, w_o, g_ffn, w_gate, w_up, w_down, *, tm):
    n, d_model = x2d.shape
    nt = b_slab.shape[2] // tm
    d_ff = w_gate.shape[1]
    return dict(
        stages=_finish_stages,
        args=(x2d, mix_a, b_slab, g_out_b[None], w_o.astype(BF16), g_ffn[None],
              w_gate.astype(BF16), w_up.astype(BF16), w_down.astype(BF16)),
        in_specs=[pl.BlockSpec((tm, d_model), lambda i: (i, 0)),
                  pl.BlockSpec((tm, WIDTH), lambda i: (i, 0)),
                  pl.BlockSpec((None, N_SLABS, tm, LANES), lambda i: (i // nt, 0, i % nt, 0)),
                  _const_spec((1, WIDTH)), _const_spec((2 * WIDTH, d_model)), _const_spec((1, d_model)),
                  _const_spec((d_model, d_ff)), _const_spec((d_model, d_ff)), _const_spec((d_ff, d_model))],
        out_specs=[pl.BlockSpec((tm, d_model), lambda i: (i, 0))],
        out_shapes=[jax.ShapeDtypeStruct((n, d_model), F32)],
        scratch=[], steps=n // tm)


def kernel(x_prompt, x_sample, cache_k, cache_v, g_attn, w_in, ln_v_g, ln_v_b, w_s, b_s, g_q, g_k,
           g_out_a, g_out_b, w_o, g_ffn, w_gate, w_up, w_down):
    depth = w_in.shape[0]
    assert depth == 1, "single-layer step"
    batch, seq, d_model = x_prompt.shape
    n_seq, dec_seq, _ = x_sample.shape
    l = 0
    shared = _shared_in_proj_args(g_attn[l], w_in[l], ln_v_g[l], ln_v_b[l], g_q[l], g_k[l])
    fin = (g_out_b[l], w_o[l], g_ffn[l], w_gate[l], w_up[l], w_down[l])
    n = batch * seq

    qs, ks, vs, vns, mix_as = _in_proj_sample(x_sample, shared, w_s[l], b_s[l], g_out_a[l])
    as3 = lambda a: a.reshape(n_seq, dec_seq, WIDTH)
    att_ops = _sample_attention_operands(as3(qs), as3(ks), as3(vs), cache_k[l], cache_v[l])
    (tm_in, pairs_in), (tm_fin, pairs_fin) = IN_PROJ_HOSTING, FINISH_HOSTING
    fin_pairs = (n // tm_fin) * pairs_fin
    assert fin_pairs + (n // tm_in) * pairs_in == n_seq * N_HEADS

    def hosted(host, first, per_step, name):
        att = _sample_attention_call(att_ops, dec_seq, PAST_LEN, first, host["steps"], per_step)
        host_outs, (att_out,) = _staged_call([host, att], [1, 0, 1, 0, 1], name)
        return host_outs, att_out

    in_proj = _in_proj_prompt_call(x_prompt, shared, w_s[l], b_s[l], g_out_a[l], tm=tm_in)
    (q, k, v, k_out, v_out, mix_a), att_in = hosted(in_proj, fin_pairs, pairs_in, "in_proj_prompt_and_sample_attention")
    b_slab = _prompt_attention(q, k, v)
    finish = _finish_call(x_prompt.reshape(n, d_model), mix_a, b_slab, *fin, tm=tm_fin)
    (y_prompt,), att_fin = hosted(finish, 0, pairs_fin, "finish_and_sample_attention")
    win = min(max(w for w, _ in DILATED_PATTERNS), seq)
    window = lambda a: a.reshape(batch, N_HEADS, HEAD_DIM, seq).transpose(0, 3, 1, 2)[:, -win:][None]

    bs_slab = _sample_attention_slabs([att_fin, att_in], dec_seq)
    finish_s = _finish_call(x_sample.reshape(n_seq * dec_seq, d_model), mix_as, bs_slab, *fin, tm=n_seq * dec_seq)
    ((y_sample,),) = _staged_call([finish_s], [0, 0], "finish")

    head_shape = (1, n_seq, dec_seq, N_HEADS, HEAD_DIM)
    return (y_prompt.reshape(batch, seq, d_model), y_sample.reshape(n_seq, dec_seq, d_model),
            window(k_out), window(v_out), ks.reshape(head_shape), vs.reshape(head_shape),
            vns.reshape(1, n_seq, dec_seq, WIDTH))
```

```python
import functools

import numpy as np
import jax
import jax.numpy as jnp
from jax import lax
from jax.experimental import pallas as pl
from jax.experimental.pallas import tpu as pltpu

F32 = jnp.float32
BF16 = jnp.bfloat16

HEAD_DIM = 64
N_HEADS = 8
WIDTH = N_HEADS * HEAD_DIM
CHUNK = 128
DILATED_PATTERNS = ((128, 1), (512, 4), (2048, 16))
EPS = 1e-6
PAST_LEN = 8192
NEG_BIG = -1e30
LOG2_E = 1.4426950408889634

LANES = 128
N_SLABS = WIDTH // LANES
VMEM_LIMIT = 60 * 1024 * 1024
IN_PROJ_HOSTING = (512, 4)
FINISH_HOSTING = (256, 14)


def _dot(a, b):
    return jnp.dot(a, b, preferred_element_type=F32)


def _dot_nt(a, b):
    return lax.dot_general(a, b, (((1,), (1,)), ((), ())), preferred_element_type=F32)


def _const_spec(shape):
    return pl.BlockSpec(shape, lambda *_: (0,) * len(shape), pipeline_mode=pl.Buffered(1))


def _head_rms(z, bd_ref, g_ref):
    msq = _dot((z * z).astype(BF16), bd_ref[...])
    return z * lax.rsqrt(msq + EPS) * g_ref[...]


def _features(x_ref, g_attn_ref, w_in_ref, ln_g_ref, ln_b_ref, gq_ref, gk_ref, bd_ref):
    x = x_ref[...]
    ms = jnp.mean(x * x, axis=-1, keepdims=True)
    xn = (x * lax.rsqrt(ms + EPS) * g_attn_ref[...]).astype(BF16)

    def proj(j):
        return _dot(xn, w_in_ref[:, j * WIDTH:(j + 1) * WIDTH])

    zu, zv, zq, zk, v = (proj(j) for j in range(5))
    u = jax.nn.gelu(zu)
    gv = jax.nn.gelu(zv)
    mu = jnp.mean(gv, axis=-1, keepdims=True)
    xc = gv - mu
    var = jnp.mean(xc * xc, axis=-1, keepdims=True)
    vn = xc * lax.rsqrt(var + EPS) * ln_g_ref[...] + ln_b_ref[...]
    q = _head_rms(zq, bd_ref, gq_ref)
    k = _head_rms(zk, bd_ref, gk_ref)
    return u, vn, q, k, v


def _group_norm_rows(slabs, g_ref):
    ssq = sum(jnp.sum(a * a, axis=-1, keepdims=True) for a in slabs)
    r = lax.rsqrt(ssq * (1.0 / WIDTH) + EPS)
    return [a * r * g_ref[:, s * LANES:(s + 1) * LANES] for s, a in enumerate(slabs)]


def _in_proj_prompt_stages(x_ref, g_attn_ref, w_in_ref, ln_g_ref, ln_b_ref, gq_ref, gk_ref, bd_ref,
                           ws_ref, bs_ref, g_out_a_ref,
                           q_ref, k_ref, v_ref, kout_ref, vout_ref, mixa_ref):
    u, vn, q, k, v = _features(x_ref, g_attn_ref, w_in_ref, ln_g_ref, ln_b_ref, gq_ref, gk_ref, bd_ref)
    yield
    tm = u.shape[0]
    first_head = lax.broadcasted_iota(jnp.int32, (CHUNK, LANES), 1) < HEAD_DIM
    for c in range(tm // CHUNK):
        rows = slice(c * CHUNK, (c + 1) * CHUNK)
        vn_c = vn[rows].astype(BF16)
        slabs = []
        for s in range(N_SLABS):
            cols = slice(s * LANES, (s + 1) * LANES)
            g0 = _dot(ws_ref[2 * s], vn_c[:, cols])
            g1 = _dot(ws_ref[2 * s + 1], vn_c[:, cols])
            gate = jnp.where(first_head, g0, g1) + bs_ref[:, cols]
            slabs.append(u[rows, cols] * gate)
        for s, a in enumerate(_group_norm_rows(slabs, g_out_a_ref)):
            mixa_ref[rows, s * LANES:(s + 1) * LANES] = a.astype(BF16)
    for s in range(N_SLABS):
        cols = slice(s * LANES, (s + 1) * LANES)
        q_ref[s] = q[:, cols]
        k_ref[s] = k[:, cols]
        v_ref[s] = v[:, cols]
    kout_ref[...] = k.T
    vout_ref[...] = v.T


def _in_proj_sample_kernel(x_ref, g_attn_ref, w_in_ref, ln_g_ref, ln_b_ref, gq_ref, gk_ref, bd_ref,
                           coef_ref, bias_ref, g_out_a_ref,
                           q_ref, k_ref, v_ref, vn_ref, mixa_ref, *, seq):
    u, vn, q, k, v = _features(x_ref, g_attn_ref, w_in_ref, ln_g_ref, ln_b_ref, gq_ref, gk_ref, bd_ref)
    tm = u.shape[0]
    sub = coef_ref.shape[1]
    gate = jnp.zeros((tm // sub, sub, WIDTH), F32) + bias_ref[...]
    for d in range(seq):
        shifted = vn if d == 0 else pltpu.roll(vn, d, axis=0)
        gate = gate + shifted.reshape(tm // sub, sub, WIDTH) * coef_ref[d]
    a = u * gate.reshape(tm, WIDTH)
    slabs = _group_norm_rows([a[:, s * LANES:(s + 1) * LANES] for s in range(N_SLABS)], g_out_a_ref)
    for s, a_s in enumerate(slabs):
        mixa_ref[:, s * LANES:(s + 1) * LANES] = a_s.astype(BF16)
    q_ref[...] = q
    k_ref[...] = k
    v_ref[...] = v
    vn_ref[...] = vn


def _prompt_attn_kernel(q_ref, k_ref, v_ref, band_ref, causal_ref, o_ref, m_sc, l_sc, s_sc, p_sc, grp_sc, sf_sc, pf_sc, *, n_blocks):
    (w_near, d_near), (w_mid, d_mid), (w_far, d_far) = DILATED_PATTERNS
    assert d_near == 1 and d_far == n_blocks and n_blocks % d_mid == 0
    assert w_near // d_near == w_mid // d_mid == w_far // d_far == CHUNK
    mid_blocks = n_blocks // d_mid
    sub = CHUNK // 2

    first_head = lax.broadcasted_iota(jnp.int32, (CHUNK, LANES), 1) < HEAD_DIM
    ones = jnp.ones((2 * CHUNK, LANES), BF16)

    both = lambda a: jnp.where(first_head, a[:CHUNK], a[CHUNK:])

    def keys_values(ref, s, cur, prev):
        a = ref[s, cur, :]
        return a if prev is None else jnp.concatenate([ref[s, prev, :], a], axis=0)

    def scores(s, cur, prev, src, s_tile):
        q_src, k_src, _ = src
        qb = q_src[s, cur, :].astype(BF16)
        zero = jnp.zeros_like(qb)
        lhs = jnp.concatenate([jnp.where(first_head, qb, zero), jnp.where(first_head, zero, qb)], axis=0)
        kk = keys_values(k_src, s, cur, prev).astype(BF16)
        s_tile[:, :kk.shape[0]] = _dot_nt(lhs, kk)

    def softmax(prev, has_prev, s_tile, p_tile):
        n_keys = CHUNK if prev is None else 2 * CHUNK
        m_parts = []
        for c in range(2 * CHUNK // sub):
            rows = slice(c * sub, (c + 1) * sub)
            bias = causal_ref[rows, :] if prev is None else band_ref[has_prev, rows, :]
            sc = s_tile[rows, :n_keys] + bias
            m = jnp.max(sc, axis=-1, keepdims=True)
            p_tile[rows, :n_keys] = jnp.exp2(sc - m).astype(BF16)
            m_parts.append(jnp.broadcast_to(m, (sub, LANES)))
        return both(jnp.concatenate(m_parts, axis=0))

    def values(s, cur, prev, src, p_tile):
        vv = keys_values(src[2], s, cur, prev).astype(BF16)
        n_keys = vv.shape[0]
        out = _dot(p_tile[:, :n_keys], jnp.concatenate([vv, ones[:n_keys]], axis=1))
        return both(out[:, LANES:]), both(out[:, :LANES])

    def attend(s, cur, prev, has_prev):
        src = (q_ref, k_ref, v_ref)
        scores(s, cur, prev, src, s_sc.at[s])
        m = softmax(prev, has_prev, s_sc.at[s], p_sc.at[s])
        return (m,) + values(s, cur, prev, src, p_sc.at[s])

    def merged(s, cur, stats):
        m_new, l_new, acc_new = stats
        m_old = m_sc[s, cur, :]
        m_tot = jnp.maximum(m_old, m_new)
        a_old = jnp.exp2(m_old - m_tot)
        a_new = jnp.exp2(m_new - m_tot)
        return (m_tot, a_old * l_sc[s, cur, :] + a_new * l_new, a_old * o_ref[s, cur, :] + a_new * acc_new)

    staged = [grp_sc.at[i] for i in range(6)]
    sub_classes = d_far // d_mid

    def far_group(r, carry):
        grp = pl.ds(r, o_ref.shape[1] // d_mid, stride=d_mid)
        for s in range(N_SLABS):
            for src, dst in zip((q_ref, k_ref, v_ref), staged[:3]):
                dst[s] = src[s, grp, :]
        blocks = [(c * N_SLABS + s, s, pl.ds(c, CHUNK, stride=sub_classes))
                  for c in range(sub_classes) for s in range(N_SLABS)]
        st_m, st_l, st_acc = staged[3:]
        for tile, s, cur in blocks:
            scores(s, cur, None, staged[:3], sf_sc.at[tile])
        for tile, s, cur in blocks:
            st_m[s, cur, :] = softmax(None, None, sf_sc.at[tile], pf_sc.at[tile])
        for tile, s, cur in blocks:
            st_l[s, cur, :], st_acc[s, cur, :] = values(s, cur, None, staged[:3], pf_sc.at[tile])
        for s in range(N_SLABS):
            for dst, src in zip((m_sc, l_sc, o_ref), staged[3:]):
                dst[s, grp, :] = src[s]
        return carry

    lax.fori_loop(0, d_mid, far_group, 0, unroll=2)

    def span(jb, carry):
        for r in range(d_mid):
            cur = pl.ds(r + (d_mid * CHUNK) * jb, CHUNK, stride=d_mid)
            prev = pl.ds(r + (d_mid * CHUNK) * jnp.maximum(jb - 1, 0), CHUNK, stride=d_mid)
            for s in range(N_SLABS):
                m_sc[s, cur, :], l_sc[s, cur, :], o_ref[s, cur, :] = merged(s, cur, attend(s, cur, prev, jnp.minimum(jb, 1)))
        for i in range(d_mid):
            j = jb * d_mid + i
            cur = pl.ds(pl.multiple_of(j * CHUNK, CHUNK), CHUNK)
            prev = pl.ds(pl.multiple_of(jnp.maximum(j - 1, 0) * CHUNK, CHUNK), CHUNK)
            for s in range(N_SLABS):
                _, l_tot, acc_tot = merged(s, cur, attend(s, cur, prev, jnp.minimum(j, 1)))
                o_ref[s, cur, :] = acc_tot / l_tot
        return carry

    lax.fori_loop(0, mid_blocks, span, 0)


def _sample_attn_stages(new_ref, kt_ref, vt_ref, mask_ref, o_ref, s_sc, p_sc, *, seq, new_valid):
    n_rows = mask_ref.shape[0]
    n_real = len(DILATED_PATTERNS) * seq
    assert new_ref.shape[2] == 2 * seq and n_rows == 4 * seq
    q_ref, kn_ref, vn_ref = (new_ref.at[:, i] for i in range(3))

    def branch_rows(a):
        return jnp.concatenate([a + pltpu.roll(a, seq, axis=0), a], axis=0)

    row = lax.broadcasted_iota(jnp.int32, (n_rows, 1), 0)
    real = row < n_real
    valid = mask_ref[...] > 0.5
    new_rows = []
    for tp in range(seq):
        sel = functools.reduce(jnp.logical_or, [row == r for r in range(n_rows) if new_valid[tp][r]])
        new_rows.append(sel)

    heads = range(q_ref.shape[0])
    for h in heads:
        sc = _dot(q_ref[h].astype(BF16), kt_ref[h].astype(BF16))
        s_sc[h] = jnp.where(valid, branch_rows(sc), -jnp.inf)
    yield

    weights = []
    for h in heads:
        q = branch_rows(q_ref[h])
        kn = kn_ref[h]
        s_new = [jnp.where(new_rows[tp], jnp.sum(q * kn[tp:tp + 1, :], axis=-1, keepdims=True), -jnp.inf)
                 for tp in range(seq)]
        sc = s_sc[h]
        m = jnp.max(sc, axis=-1, keepdims=True)
        for sn in s_new:
            m = jnp.maximum(m, sn)
        m = jnp.where(real, m, 0.0)
        p = jnp.exp2(sc - m)
        p_sc[h] = p.astype(BF16)
        p_new = [jnp.exp2(sn - m) for sn in s_new]
        l = jnp.sum(p, axis=-1, keepdims=True) + sum(p_new)
        m_eff = jnp.where(real, m, NEG_BIG)
        l_eff = jnp.where(real, l, 0.0)
        m_all = m_eff
        for i in range(1, n_rows // seq):
            m_all = jnp.maximum(m_all, pltpu.roll(m_eff, i * seq, axis=0))
        c = jnp.exp2(m_eff - m_all)
        cl = c * l_eff
        den = cl
        for i in range(1, n_rows // seq):
            den = den + pltpu.roll(cl, i * seq, axis=0)
        w = jnp.where(real, c / den, 0.0)
        weights.append((w, [pn * w for pn in p_new]))
    yield

    for h in heads:
        w, pw_new = weights[h]
        vn = vn_ref[h]
        out = _dot_nt(p_sc[h], vt_ref[h].astype(BF16)) * w
        for tp in range(seq):
            out = out + pw_new[tp] * vn[tp:tp + 1, :]
        tot = out
        for i in range(1, n_rows // seq):
            tot = tot + pltpu.roll(out, i * seq, axis=0)
        o_ref[h] = tot


def _finish_stages(x_ref, mixa_ref, b_ref, g_out_b_ref, wo_ref, g_ffn_ref, wg_ref, wu_ref, wd_ref, o_ref):
    slabs = _group_norm_rows([b_ref[s] for s in range(N_SLABS)], g_out_b_ref)
    mix_b = jnp.concatenate([a.astype(BF16) for a in slabs], axis=-1)
    x1 = x_ref[...] + _dot(mixa_ref[...], wo_ref[:WIDTH, :]) + _dot(mix_b, wo_ref[WIDTH:, :])
    ms = jnp.mean(x1 * x1, axis=-1, keepdims=True)
    h = (x1 * lax.rsqrt(ms + EPS) * g_ffn_ref[...]).astype(BF16)
    act = (jax.nn.silu(_dot(h, wg_ref[...])) * _dot(h, wu_ref[...])).astype(BF16)
    yield
    o_ref[...] = x1 + _dot(act, wd_ref[...])


def _params(sem):
    return pltpu.CompilerParams(dimension_semantics=sem, vmem_limit_bytes=VMEM_LIMIT)


_DONE = object()


def _staged_call(calls, order, name):
    steps = calls[0]["steps"]
    assert all(c["steps"] == steps for c in calls)
    counts = [(len(c["args"]), len(c["out_specs"]), len(c["scratch"])) for c in calls]

    def body(*refs):
        groups = []
        pos = 0
        for kind in range(3):
            per_call = []
            for cnt in counts:
                per_call.append(refs[pos:pos + cnt[kind]])
                pos += cnt[kind]
            groups.append(per_call)
        gens = [c["stages"](*groups[0][i], *groups[1][i], *groups[2][i]) for i, c in enumerate(calls)]
        for i in order:
            next(gens[i], None)
        assert all(next(g, _DONE) is _DONE for g in gens), "order leaves stages untraced"

    cat = lambda key: [item for c in calls for item in c[key]]
    outs = pl.pallas_call(
        body, grid=(steps,), in_specs=cat("in_specs"), out_specs=cat("out_specs"), out_shape=cat("out_shapes"),
        scratch_shapes=cat("scratch"), compiler_params=_params(("parallel",)), name=name,
    )(*cat("args"))
    split, pos = [], 0
    for _, n_out, _ in counts:
        split.append(outs[pos:pos + n_out])
        pos += n_out
    return split


def _shared_in_proj_args(g_attn, w_in, ln_v_g, ln_v_b, g_q, g_k):
    d_model, in_width = w_in.shape
    gq = (jnp.tile(g_q, N_HEADS) * (HEAD_DIM ** -0.5 * LOG2_E))[None]
    gk = jnp.tile(g_k, N_HEADS)[None]
    bd = jnp.asarray(np.kron(np.eye(N_HEADS), np.full((HEAD_DIM, HEAD_DIM), 1.0 / HEAD_DIM)), BF16)
    args = (g_attn[None], w_in.astype(BF16), ln_v_g[None], ln_v_b[None], gq, gk, bd)
    specs = [_const_spec((1, d_model)), _const_spec((d_model, in_width)), _const_spec((1, WIDTH)),
             _const_spec((1, WIDTH)), _const_spec((1, WIDTH)), _const_spec((1, WIDTH)),
             _const_spec((WIDTH, WIDTH))]
    return args, specs


def _in_proj_prompt_call(x, shared, w_s, b_s, g_out_a, *, tm):
    batch, seq, d_model = x.shape
    n = batch * seq
    nt = seq // tm
    shared_args, shared_specs = shared
    tri = np.tril(np.ones((CHUNK, CHUNK), bool))
    ws = jnp.where(tri, w_s, 0.0).astype(BF16)
    bs = jnp.repeat(b_s.T, HEAD_DIM, axis=1)
    slab = jax.ShapeDtypeStruct((batch, N_SLABS, seq, LANES), F32)
    flat = jax.ShapeDtypeStruct((batch, WIDTH, seq), F32)
    slab_spec = pl.BlockSpec((None, N_SLABS, tm, LANES), lambda i: (i // nt, 0, i % nt, 0))
    flat_spec = pl.BlockSpec((None, WIDTH, tm), lambda i: (i // nt, 0, i % nt))
    row_spec = pl.BlockSpec((tm, WIDTH), lambda i: (i, 0))
    return dict(
        stages=_in_proj_prompt_stages,
        args=(x.reshape(n, d_model), *shared_args, ws, bs, g_out_a[None]),
        in_specs=[pl.BlockSpec((tm, d_model), lambda i: (i, 0))] + shared_specs + [
            _const_spec((N_HEADS, CHUNK, CHUNK)), _const_spec((CHUNK, WIDTH)), _const_spec((1, WIDTH))],
        out_specs=[slab_spec, slab_spec, slab_spec, flat_spec, flat_spec, row_spec],
        out_shapes=[slab, slab, slab, flat, flat, jax.ShapeDtypeStruct((n, WIDTH), BF16)],
        scratch=[], steps=n // tm)


def _in_proj_sample(x, shared, w_s, b_s, g_out_a):
    n_seq, seq, d_model = x.shape
    n = n_seq * seq
    shared_args, shared_specs = shared
    sub = 8
    assert sub % seq == 0 and seq <= CHUNK
    step = np.arange(sub) % seq
    corner = w_s[:, :seq, :seq]
    taps = []
    for d in range(seq):
        src = np.maximum(step - d, 0)
        tap = jnp.where((step >= d)[:, None], corner[:, step, src].T, 0.0)
        taps.append(jnp.repeat(tap, HEAD_DIM, axis=1))
    coef = jnp.stack(taps)
    bias = jnp.repeat(b_s[:, step].T, HEAD_DIM, axis=1)
    flat = jax.ShapeDtypeStruct((n, WIDTH), F32)
    row_spec = pl.BlockSpec((n, WIDTH), lambda i: (0, 0))
    return pl.pallas_call(
        functools.partial(_in_proj_sample_kernel, seq=seq),
        grid=(1,),
        in_specs=[pl.BlockSpec((n, d_model), lambda i: (0, 0))] + shared_specs + [
            _const_spec((seq, sub, WIDTH)), _const_spec((sub, WIDTH)), _const_spec((1, WIDTH))],
        out_specs=[row_spec] * 5,
        out_shape=[flat, flat, flat, flat, jax.ShapeDtypeStruct((n, WIDTH), BF16)],
        compiler_params=_params(("arbitrary",)),
        name="in_proj_sample",
    )(x.reshape(n, d_model), *shared_args, coef, bias, g_out_a[None])


def _score_biases():
    row = (np.arange(2 * CHUNK) % CHUNK)[:, None]
    col = np.arange(2 * CHUNK)[None, :]
    in_band = (col >= row) & (col <= row + CHUNK)
    band = np.stack([in_band & (col >= CHUNK), in_band])
    causal = col[:, :CHUNK] <= row
    to_bias = lambda ok: jnp.asarray(np.where(ok, 0.0, -np.inf), F32)
    return to_bias(band), to_bias(causal)


def _prompt_attention(q, k, v):
    batch, _, seq, _ = q.shape
    n_blocks = seq // CHUNK
    assert all(n_blocks % dil == 0 for _, dil in DILATED_PATTERNS)
    spec = pl.BlockSpec((None, N_SLABS, seq, LANES), lambda b: (b, 0, 0, 0))
    band, causal = _score_biases()
    far_tiles = N_SLABS * DILATED_PATTERNS[2][1] // DILATED_PATTERNS[1][1]
    return pl.pallas_call(
        functools.partial(_prompt_attn_kernel, n_blocks=n_blocks),
        grid=(batch,),
        in_specs=[spec, spec, spec, _const_spec(band.shape), _const_spec(causal.shape)],
        out_specs=spec,
        out_shape=jax.ShapeDtypeStruct(q.shape, F32),
        scratch_shapes=[pltpu.VMEM((N_SLABS, seq, LANES), F32), pltpu.VMEM((N_SLABS, seq, LANES), F32),
                        pltpu.VMEM((N_SLABS, 2 * CHUNK, 2 * CHUNK), F32),
                        pltpu.VMEM((N_SLABS, 2 * CHUNK, 2 * CHUNK), BF16),
                        pltpu.VMEM((6, N_SLABS, seq // DILATED_PATTERNS[1][1], LANES), F32),
                        pltpu.VMEM((far_tiles, 2 * CHUNK, CHUNK), F32), pltpu.VMEM((far_tiles, 2 * CHUNK, CHUNK), BF16)],
        compiler_params=_params(("parallel",)),
        name="prompt_attention",
    )(q, k, v, band, causal)


def _sample_masks(seq, cache_len, past_len, n_rows):
    cache = np.zeros((n_rows, cache_len), np.float32)
    new = [[False] * n_rows for _ in range(seq)]
    for b, (window, dil) in enumerate(DILATED_PATTERNS):
        for t in range(seq):
            for jj in range(window // dil + 1):
                idx = cache_len + t - dil * jj
                if idx < 0 or idx + (past_len - cache_len) < 0:
                    continue
                if idx < cache_len:
                    cache[b * seq + t, idx] = 1.0
                else:
                    new[idx - cache_len][b * seq + t] = True
    return cache, new


def _sample_attention_operands(q, k, v, cache_k, cache_v):
    n_seq, seq, _ = q.shape
    cache_len = cache_k.shape[1]
    new = jnp.stack([q, k, v], axis=2).reshape(n_seq, seq, 3, N_HEADS, HEAD_DIM).transpose(0, 3, 2, 1, 4)
    new = jnp.pad(new, ((0, 0), (0, 0), (0, 0), (0, seq), (0, 0)))
    per_pair = lambda c: jnp.transpose(c, (0, 2, 3, 1)).reshape(n_seq * N_HEADS, HEAD_DIM, cache_len)
    return new.reshape(n_seq * N_HEADS, 3, 2 * seq, HEAD_DIM), per_pair(cache_k), per_pair(cache_v)


def _sample_attention_call(operands, seq, past_len, first, steps, per_step):
    new, kt, vt = operands
    n_rows = 4 * seq
    assert len(DILATED_PATTERNS) * seq <= n_rows and first % per_step == 0
    cache_len = kt.shape[2]
    mask, new_valid = _sample_masks(seq, cache_len, past_len, n_rows)
    block0 = first // per_step
    new_spec = pl.BlockSpec((per_step, 3, 2 * seq, HEAD_DIM), lambda i: (block0 + i, 0, 0, 0))
    cache_spec = pl.BlockSpec((per_step, HEAD_DIM, cache_len), lambda i: (block0 + i, 0, 0))
    return dict(
        stages=functools.partial(_sample_attn_stages, seq=seq, new_valid=new_valid),
        args=(new, kt, vt, jnp.asarray(mask)),
        in_specs=[new_spec, cache_spec, cache_spec, _const_spec((n_rows, cache_len))],
        out_specs=[pl.BlockSpec((per_step, n_rows, HEAD_DIM), lambda i: (i, 0, 0))],
        out_shapes=[jax.ShapeDtypeStruct((steps * per_step, n_rows, HEAD_DIM), F32)],
        scratch=[pltpu.VMEM((per_step, n_rows, cache_len), F32), pltpu.VMEM((per_step, n_rows, cache_len), BF16)],
        steps=steps)


def _sample_attention_slabs(outs, seq):
    out = jnp.concatenate(outs, axis=0)
    out = out.reshape(-1, N_HEADS, out.shape[1], HEAD_DIM)
    n_seq = out.shape[0]
    b_out = out[:, :, :seq, :].transpose(0, 2, 1, 3).reshape(n_seq * seq, N_SLABS, LANES)
    return b_out.transpose(1, 0, 2)[None]


def _finish_call(x2d, mix_a, b_slab, g_out_b, w_o, g_ffn, w_gate, w_up, w_down, *, tm):
    n, d_model = x2d.shape
    nt = b_slab.shape[2] // tm
    d_ff = w_gate.shape[1]
    return dict(
        stages=_finish_stages,
        args=(x2d, mix_a, b_slab, g_out_b[None], w_o.astype(BF16), g_ffn[None],
              w_gate.astype(BF16), w_up.astype(BF16), w_down.astype(BF16)),
        in_specs=[pl.BlockSpec((tm, d_model), lambda i: (i, 0)),
                  pl.BlockSpec((tm, WIDTH), lambda i: (i, 0)),
                  pl.BlockSpec((None, N_SLABS, tm, LANES), lambda i: (i // nt, 0, i % nt, 0)),
                  _const_spec((1, WIDTH)), _const_spec((2 * WIDTH, d_model)), _const_spec((1, d_model)),
                  _const_spec((d_model, d_ff)), _const_spec((d_model, d_ff)), _const_spec((d_ff, d_model))],
        out_specs=[pl.BlockSpec((tm, d_model), lambda i: (i, 0))],
        out_shapes=[jax.ShapeDtypeStruct((n, d_model), F32)],
        scratch=[], steps=n // tm)


def kernel(x_prompt, x_sample, cache_k, cache_v, g_attn, w_in, ln_v_g, ln_v_b, w_s, b_s, g_q, g_k,
           g_out_a, g_out_b, w_o, g_ffn, w_gate, w_up, w_down):
    depth = w_in.shape[0]
    assert depth == 1, "single-layer step"
    batch, seq, d_model = x_prompt.shape
    n_seq, dec_seq, _ = x_sample.shape
    l = 0
    shared = _shared_in_proj_args(g_attn[l], w_in[l], ln_v_g[l], ln_v_b[l], g_q[l], g_k[l])
    fin = (g_out_b[l], w_o[l], g_ffn[l], w_gate[l], w_up[l], w_down[l])
    n = batch * seq

    qs, ks, vs, vns, mix_as = _in_proj_sample(x_sample, shared, w_s[l], b_s[l], g_out_a[l])
    as3 = lambda a: a.reshape(n_seq, dec_seq, WIDTH)
    att_ops = _sample_attention_operands(as3(qs), as3(ks), as3(vs), cache_k[l], cache_v[l])
    (tm_in, pairs_in), (tm_fin, pairs_fin) = IN_PROJ_HOSTING, FINISH_HOSTING
    fin_pairs = (n // tm_fin) * pairs_fin
    assert fin_pairs + (n // tm_in) * pairs_in == n_seq * N_HEADS

    def hosted(host, first, per_step, name):
        att = _sample_attention_call(att_ops, dec_seq, PAST_LEN, first, host["steps"], per_step)
        host_outs, (att_out,) = _staged_call([host, att], [1, 0, 1, 0, 1], name)
        return host_outs, att_out

    in_proj = _in_proj_prompt_call(x_prompt, shared, w_s[l], b_s[l], g_out_a[l], tm=tm_in)
    (q, k, v, k_out, v_out, mix_a), att_in = hosted(in_proj, fin_pairs, pairs_in, "in_proj_prompt_and_sample_attention")
    b_slab = _prompt_attention(q, k, v)
    finish = _finish_call(x_prompt.reshape(n, d_model), mix_a, b_slab, *fin, tm=tm_fin)
    (y_prompt,), att_fin = hosted(finish, 0, pairs_fin, "finish_and_sample_attention")
    win = min(max(w for w, _ in DILATED_PATTERNS), seq)
    window = lambda a: a.reshape(batch, N_HEADS, HEAD_DIM, seq).transpose(0, 3, 1, 2)[:, -win:][None]

    bs_slab = _sample_attention_slabs([att_fin, att_in], dec_seq)
    finish_s = _finish_call(x_sample.reshape(n_seq * dec_seq, d_model), mix_as, bs_slab, *fin, tm=n_seq * dec_seq)
    ((y_sample,),) = _staged_call([finish_s], [0, 0], "finish")

    head_shape = (1, n_seq, dec_seq, N_HEADS, HEAD_DIM)
    return (y_prompt.reshape(batch, seq, d_model), y_sample.reshape(n_seq, dec_seq, d_model),
            window(k_out), window(v_out), ks.reshape(head_shape), vs.reshape(head_shape),
            vns.reshape(1, n_seq, dec_seq, WIDTH))
```

```python
import functools

import numpy as np
import jax
import jax.numpy as jnp
from jax import lax
from jax.experimental import pallas as pl
from jax.experimental.pallas import tpu as pltpu

F32 = jnp.float32
BF16 = jnp.bfloat16

HEAD_DIM = 64
N_HEADS = 8
WIDTH = N_HEADS * HEAD_DIM
CHUNK = 128
DILATED_PATTERNS = ((128, 1), (512, 4), (2048, 16))
EPS = 1e-6
PAST_LEN = 8192
NEG_BIG = -1e30
LOG2_E = 1.4426950408889634

LANES = 128
N_SLABS = WIDTH // LANES
VMEM_LIMIT = 62 * 1024 * 1024
IN_PROJ_HOSTING = (512, 2)
FINISH_HOSTING = (256, 15)


def _dot(a, b):
    return jnp.dot(a, b, preferred_element_type=F32)


def _dot_nt(a, b):
    return lax.dot_general(a, b, (((1,), (1,)), ((), ())), preferred_element_type=F32)


def _const_spec(shape):
    return pl.BlockSpec(shape, lambda *_: (0,) * len(shape), pipeline_mode=pl.Buffered(1))


def _head_rms(z, bd_ref, g_ref):
    msq = _dot((z * z).astype(BF16), bd_ref[...])
    return z * lax.rsqrt(msq + EPS) * g_ref[...]


def _features(x_ref, g_attn_ref, w_in_ref, ln_g_ref, ln_b_ref, gq_ref, gk_ref, bd_ref):
    x = x_ref[...]
    ms = jnp.mean(x * x, axis=-1, keepdims=True)
    xn = (x * lax.rsqrt(ms + EPS) * g_attn_ref[...]).astype(BF16)

    def proj(j):
        return _dot(xn, w_in_ref[:, j * WIDTH:(j + 1) * WIDTH])

    zu, zv, zq, zk, v = (proj(j) for j in range(5))
    u = jax.nn.gelu(zu)
    gv = jax.nn.gelu(zv)
    mu = jnp.mean(gv, axis=-1, keepdims=True)
    xc = gv - mu
    var = jnp.mean(xc * xc, axis=-1, keepdims=True)
    vn = xc * lax.rsqrt(var + EPS) * ln_g_ref[...] + ln_b_ref[...]
    q = _head_rms(zq, bd_ref, gq_ref)
    k = _head_rms(zk, bd_ref, gk_ref)
    return u, vn, q, k, v


def _group_norm_rows(slabs, g_ref):
    ssq = sum(jnp.sum(a * a, axis=-1, keepdims=True) for a in slabs)
    r = lax.rsqrt(ssq * (1.0 / WIDTH) + EPS)
    return [a * r * g_ref[:, s * LANES:(s + 1) * LANES] for s, a in enumerate(slabs)]


def _in_proj_prompt_stages(x_ref, g_attn_ref, w_in_ref, ln_g_ref, ln_b_ref, gq_ref, gk_ref, bd_ref,
                           ws_ref, bs_ref, g_out_a_ref,
                           q_ref, k_ref, v_ref, kout_ref, vout_ref, mixa_ref):
    u, vn, q, k, v = _features(x_ref, g_attn_ref, w_in_ref, ln_g_ref, ln_b_ref, gq_ref, gk_ref, bd_ref)
    yield
    tm = u.shape[0]
    first_head = lax.broadcasted_iota(jnp.int32, (CHUNK, LANES), 1) < HEAD_DIM
    for c in range(tm // CHUNK):
        rows = slice(c * CHUNK, (c + 1) * CHUNK)
        vn_c = vn[rows].astype(BF16)
        slabs = []
        for s in range(N_SLABS):
            cols = slice(s * LANES, (s + 1) * LANES)
            g0 = _dot(ws_ref[2 * s], vn_c[:, cols])
            g1 = _dot(ws_ref[2 * s + 1], vn_c[:, cols])
            gate = jnp.where(first_head, g0, g1) + bs_ref[:, cols]
            slabs.append(u[rows, cols] * gate)
        for s, a in enumerate(_group_norm_rows(slabs, g_out_a_ref)):
            mixa_ref[rows, s * LANES:(s + 1) * LANES] = a.astype(BF16)
    for s in range(N_SLABS):
        cols = slice(s * LANES, (s + 1) * LANES)
        q_ref[s] = q[:, cols]
        k_ref[s] = k[:, cols]
        v_ref[s] = v[:, cols]
    kout_ref[...] = k.T
    vout_ref[...] = v.T


def _in_proj_sample_kernel(x_ref, g_attn_ref, w_in_ref, ln_g_ref, ln_b_ref, gq_ref, gk_ref, bd_ref,
                           coef_ref, bias_ref, g_out_a_ref,
                           q_ref, k_ref, v_ref, vn_ref, mixa_ref, *, seq):
    u, vn, q, k, v = _features(x_ref, g_attn_ref, w_in_ref, ln_g_ref, ln_b_ref, gq_ref, gk_ref, bd_ref)
    tm = u.shape[0]
    sub = coef_ref.shape[1]
    gate = jnp.zeros((tm // sub, sub, WIDTH), F32) + bias_ref[...]
    for d in range(seq):
        shifted = vn if d == 0 else pltpu.roll(vn, d, axis=0)
        gate = gate + shifted.reshape(tm // sub, sub, WIDTH) * coef_ref[d]
    a = u * gate.reshape(tm, WIDTH)
    slabs = _group_norm_rows([a[:, s * LANES:(s + 1) * LANES] for s in range(N_SLABS)], g_out_a_ref)
    for s, a_s in enumerate(slabs):
        mixa_ref[:, s * LANES:(s + 1) * LANES] = a_s.astype(BF16)
    q_ref[...] = q
    k_ref[...] = k
    v_ref[...] = v
    vn_ref[...] = vn


def _prompt_attn_kernel(q_ref, k_ref, v_ref, band_ref, causal_ref, o_ref, m_sc, l_sc, s_sc, p_sc, grp_sc, sf_sc, pf_sc, *, n_blocks):
    (w_near, d_near), (w_mid, d_mid), (w_far, d_far) = DILATED_PATTERNS
    assert d_near == 1 and d_far == n_blocks and n_blocks % d_mid == 0
    assert w_near // d_near == w_mid // d_mid == w_far // d_far == CHUNK
    mid_blocks = n_blocks // d_mid
    sub = CHUNK // 2

    first_head = lax.broadcasted_iota(jnp.int32, (CHUNK, LANES), 1) < HEAD_DIM
    ones = jnp.ones((2 * CHUNK, LANES), BF16)

    both = lambda a: jnp.where(first_head, a[:CHUNK], a[CHUNK:])

    def keys_values(ref, s, cur, prev):
        a = ref[s, cur, :]
        return a if prev is None else jnp.concatenate([ref[s, prev, :], a], axis=0)

    def scores(s, cur, prev, src, s_tile):
        q_src, k_src, _ = src
        qb = q_src[s, cur, :].astype(BF16)
        zero = jnp.zeros_like(qb)
        lhs = jnp.concatenate([jnp.where(first_head, qb, zero), jnp.where(first_head, zero, qb)], axis=0)
        kk = keys_values(k_src, s, cur, prev).astype(BF16)
        s_tile[:, :kk.shape[0]] = _dot_nt(lhs, kk)

    def softmax(prev, has_prev, s_tile, p_tile):
        n_keys = CHUNK if prev is None else 2 * CHUNK
        m_parts = []
        for c in range(2 * CHUNK // sub):
            rows = slice(c * sub, (c + 1) * sub)
            bias = causal_ref[rows, :] if prev is None else band_ref[has_prev, rows, :]
            sc = s_tile[rows, :n_keys] + bias
            m = jnp.max(sc, axis=-1, keepdims=True)
            p_tile[rows, :n_keys] = jnp.exp2(sc - m).astype(BF16)
            m_parts.append(jnp.broadcast_to(m, (sub, LANES)))
        return both(jnp.concatenate(m_parts, axis=0))

    def values(s, cur, prev, src, p_tile):
        vv = keys_values(src[2], s, cur, prev).astype(BF16)
        n_keys = vv.shape[0]
        out = _dot(p_tile[:, :n_keys], jnp.concatenate([vv, ones[:n_keys]], axis=1))
        return both(out[:, LANES:]), both(out[:, :LANES])

    def attend(s, cur, prev, has_prev):
        src = (q_ref, k_ref, v_ref)
        scores(s, cur, prev, src, s_sc.at[s])
        m = softmax(prev, has_prev, s_sc.at[s], p_sc.at[s])
        return (m,) + values(s, cur, prev, src, p_sc.at[s])

    def merged(s, cur, stats):
        m_new, l_new, acc_new = stats
        m_old = m_sc[s, cur, :]
        m_tot = jnp.maximum(m_old, m_new)
        a_old = jnp.exp2(m_old - m_tot)
        a_new = jnp.exp2(m_new - m_tot)
        return (m_tot, a_old * l_sc[s, cur, :] + a_new * l_new, a_old * o_ref[s, cur, :] + a_new * acc_new)

    staged = [grp_sc.at[i] for i in range(6)]
    sub_classes = d_far // d_mid

    def far_group(r, carry):
        grp = pl.ds(r, o_ref.shape[1] // d_mid, stride=d_mid)
        for s in range(N_SLABS):
            for src, dst in zip((q_ref, k_ref, v_ref), staged[:3]):
                dst[s] = src[s, grp, :]
        blocks = [(c * N_SLABS + s, s, pl.ds(c, CHUNK, stride=sub_classes))
                  for c in range(sub_classes) for s in range(N_SLABS)]
        st_m, st_l, st_acc = staged[3:]
        for tile, s, cur in blocks:
            scores(s, cur, None, staged[:3], sf_sc.at[tile])
        for tile, s, cur in blocks:
            st_m[s, cur, :] = softmax(None, None, sf_sc.at[tile], pf_sc.at[tile])
        for tile, s, cur in blocks:
            st_l[s, cur, :], st_acc[s, cur, :] = values(s, cur, None, staged[:3], pf_sc.at[tile])
        for s in range(N_SLABS):
            for dst, src in zip((m_sc, l_sc, o_ref), staged[3:]):
                dst[s, grp, :] = src[s]
        return carry

    lax.fori_loop(0, d_mid, far_group, 0, unroll=2)

    def span(jb, carry):
        for r in range(d_mid):
            cur = pl.ds(r + (d_mid * CHUNK) * jb, CHUNK, stride=d_mid)
            prev = pl.ds(r + (d_mid * CHUNK) * jnp.maximum(jb - 1, 0), CHUNK, stride=d_mid)
            for s in range(N_SLABS):
                m_sc[s, cur, :], l_sc[s, cur, :], o_ref[s, cur, :] = merged(s, cur, attend(s, cur, prev, jnp.minimum(jb, 1)))
        for i in range(d_mid):
            j = jb * d_mid + i
            cur = pl.ds(pl.multiple_of(j * CHUNK, CHUNK), CHUNK)
            prev = pl.ds(pl.multiple_of(jnp.maximum(j - 1, 0) * CHUNK, CHUNK), CHUNK)
            for s in range(N_SLABS):
                _, l_tot, acc_tot = merged(s, cur, attend(s, cur, prev, jnp.minimum(j, 1)))
                o_ref[s, cur, :] = acc_tot / l_tot
        return carry

    lax.fori_loop(0, mid_blocks, span, 0)


def _sample_attn_stages(new_ref, kt_ref, vt_ref, mask_ref, o_ref, s_sc, p_sc, *, seq, new_valid):
    n_rows = mask_ref.shape[0]
    n_real = len(DILATED_PATTERNS) * seq
    assert new_ref.shape[2] == 2 * seq and n_rows == 4 * seq
    q_ref, kn_ref, vn_ref = (new_ref.at[:, i] for i in range(3))

    def branch_rows(a):
        return jnp.concatenate([a + pltpu.roll(a, seq, axis=0), a], axis=0)

    row = lax.broadcasted_iota(jnp.int32, (n_rows, 1), 0)
    real = row < n_real
    valid = mask_ref[...] > 0.5
    new_rows = []
    for tp in range(seq):
        sel = functools.reduce(jnp.logical_or, [row == r for r in range(n_rows) if new_valid[tp][r]])
        new_rows.append(sel)

    heads = range(q_ref.shape[0])
    for h in heads:
        sc = _dot(q_ref[h].astype(BF16), kt_ref[h].astype(BF16))
        s_sc[h] = jnp.where(valid, branch_rows(sc), -jnp.inf)
    yield

    weights = []
    for h in heads:
        q = branch_rows(q_ref[h])
        kn = kn_ref[h]
        s_new = [jnp.where(new_rows[tp], jnp.sum(q * kn[tp:tp + 1, :], axis=-1, keepdims=True), -jnp.inf)
                 for tp in range(seq)]
        sc = s_sc[h]
        m = jnp.max(sc, axis=-1, keepdims=True)
        for sn in s_new:
            m = jnp.maximum(m, sn)
        m = jnp.where(real, m, 0.0)
        p = jnp.exp2(sc - m)
        p_sc[h] = p.astype(BF16)
        p_new = [jnp.exp2(sn - m) for sn in s_new]
        l = jnp.sum(p, axis=-1, keepdims=True) + sum(p_new)
        m_eff = jnp.where(real, m, NEG_BIG)
        l_eff = jnp.where(real, l, 0.0)
        m_all = m_eff
        for i in range(1, n_rows // seq):
            m_all = jnp.maximum(m_all, pltpu.roll(m_eff, i * seq, axis=0))
        c = jnp.exp2(m_eff - m_all)
        cl = c * l_eff
        den = cl
        for i in range(1, n_rows // seq):
            den = den + pltpu.roll(cl, i * seq, axis=0)
        w = jnp.where(real, c / den, 0.0)
        weights.append((w, [pn * w for pn in p_new]))
    yield

    for h in heads:
        w, pw_new = weights[h]
        vn = vn_ref[h]
        out = _dot_nt(p_sc[h], vt_ref[h].astype(BF16)) * w
        for tp in range(seq):
            out = out + pw_new[tp] * vn[tp:tp + 1, :]
        tot = out
        for i in range(1, n_rows // seq):
            tot = tot + pltpu.roll(out, i * seq, axis=0)
        o_ref[h] = tot


def _finish_stages(x_ref, mixa_ref, b_ref, g_out_b_ref, wo_ref, g_ffn_ref, wg_ref, wu_ref, wd_ref, o_ref):
    slabs = _group_norm_rows([b_ref[s] for s in range(N_SLABS)], g_out_b_ref)
    mix_b = jnp.concatenate([a.astype(BF16) for a in slabs], axis=-1)
    x1 = x_ref[...] + _dot(mixa_ref[...], wo_ref[:WIDTH, :]) + _dot(mix_b, wo_ref[WIDTH:, :])
    ms = jnp.mean(x1 * x1, axis=-1, keepdims=True)
    h = (x1 * lax.rsqrt(ms + EPS) * g_ffn_ref[...]).astype(BF16)
    act = (jax.nn.silu(_dot(h, wg_ref[...])) * _dot(h, wu_ref[...])).astype(BF16)
    yield
    o_ref[...] = x1 + _dot(act, wd_ref[...])


def _params(sem):
    return pltpu.CompilerParams(dimension_semantics=sem, vmem_limit_bytes=VMEM_LIMIT)


_DONE = object()


def _staged_call(calls, order, name):
    steps = calls[0]["steps"]
    assert all(c["steps"] == steps for c in calls)
    counts = [(len(c["args"]), len(c["out_specs"]), len(c["scratch"])) for c in calls]

    def body(*refs):
        groups = []
        pos = 0
        for kind in range(3):
            per_call = []
            for cnt in counts:
                per_call.append(refs[pos:pos + cnt[kind]])
                pos += cnt[kind]
            groups.append(per_call)
        gens = [c["stages"](*groups[0][i], *groups[1][i], *groups[2][i]) for i, c in enumerate(calls)]
        for i in order:
            next(gens[i], None)
        assert all(next(g, _DONE) is _DONE for g in gens), "order leaves stages untraced"

    cat = lambda key: [item for c in calls for item in c[key]]
    outs = pl.pallas_call(
        body, grid=(steps,), in_specs=cat("in_specs"), out_specs=cat("out_specs"), out_shape=cat("out_shapes"),
        scratch_shapes=cat("scratch"), compiler_params=_params(("parallel",)), name=name,
    )(*cat("args"))
    split, pos = [], 0
    for _, n_out, _ in counts:
        split.append(outs[pos:pos + n_out])
        pos += n_out
    return split


def _shared_in_proj_args(g_attn, w_in, ln_v_g, ln_v_b, g_q, g_k):
    d_model, in_width = w_in.shape
    gq = (jnp.tile(g_q, N_HEADS) * (HEAD_DIM ** -0.5 * LOG2_E))[None]
    gk = jnp.tile(g_k, N_HEADS)[None]
    bd = jnp.asarray(np.kron(np.eye(N_HEADS), np.full((HEAD_DIM, HEAD_DIM), 1.0 / HEAD_DIM)), BF16)
    args = (g_attn[None], w_in.astype(BF16), ln_v_g[None], ln_v_b[None], gq, gk, bd)
    specs = [_const_spec((1, d_model)), _const_spec((d_model, in_width)), _const_spec((1, WIDTH)),
             _const_spec((1, WIDTH)), _const_spec((1, WIDTH)), _const_spec((1, WIDTH)),
             _const_spec((WIDTH, WIDTH))]
    return args, specs


def _in_proj_prompt_call(x, shared, w_s, b_s, g_out_a, *, tm):
    batch, seq, d_model = x.shape
    n = batch * seq
    nt = seq // tm
    shared_args, shared_specs = shared
    tri = np.tril(np.ones((CHUNK, CHUNK), bool))
    ws = jnp.where(tri, w_s, 0.0).astype(BF16)
    bs = jnp.repeat(b_s.T, HEAD_DIM, axis=1)
    slab = jax.ShapeDtypeStruct((batch, N_SLABS, seq, LANES), F32)
    flat = jax.ShapeDtypeStruct((batch, WIDTH, seq), F32)
    slab_spec = pl.BlockSpec((None, N_SLABS, tm, LANES), lambda i: (i // nt, 0, i % nt, 0))
    flat_spec = pl.BlockSpec((None, WIDTH, tm), lambda i: (i // nt, 0, i % nt))
    row_spec = pl.BlockSpec((tm, WIDTH), lambda i: (i, 0))
    return dict(
        stages=_in_proj_prompt_stages,
        args=(x.reshape(n, d_model), *shared_args, ws, bs, g_out_a[None]),
        in_specs=[pl.BlockSpec((tm, d_model), lambda i: (i, 0))] + shared_specs + [
            _const_spec((N_HEADS, CHUNK, CHUNK)), _const_spec((CHUNK, WIDTH)), _const_spec((1, WIDTH))],
        out_specs=[slab_spec, slab_spec, slab_spec, flat_spec, flat_spec, row_spec],
        out_shapes=[slab, slab, slab, flat, flat, jax.ShapeDtypeStruct((n, WIDTH), BF16)],
        scratch=[], steps=n // tm)


def _in_proj_sample(x, shared, w_s, b_s, g_out_a):
    n_seq, seq, d_model = x.shape
    n = n_seq * seq
    shared_args, shared_specs = shared
    sub = 8
    assert sub % seq == 0 and seq <= CHUNK
    step = np.arange(sub) % seq
    corner = w_s[:, :seq, :seq]
    taps = []
    for d in range(seq):
        src = np.maximum(step - d, 0)
        tap = jnp.where((step >= d)[:, None], corner[:, step, src].T, 0.0)
        taps.append(jnp.repeat(tap, HEAD_DIM, axis=1))
    coef = jnp.stack(taps)
    bias = jnp.repeat(b_s[:, step].T, HEAD_DIM, axis=1)
    flat = jax.ShapeDtypeStruct((n, WIDTH), F32)
    row_spec = pl.BlockSpec((n, WIDTH), lambda i: (0, 0))
    return pl.pallas_call(
        functools.partial(_in_proj_sample_kernel, seq=seq),
        grid=(1,),
        in_specs=[pl.BlockSpec((n, d_model), lambda i: (0, 0))] + shared_specs + [
            _const_spec((seq, sub, WIDTH)), _const_spec((sub, WIDTH)), _const_spec((1, WIDTH))],
        out_specs=[row_spec] * 5,
        out_shape=[flat, flat, flat, flat, jax.ShapeDtypeStruct((n, WIDTH), BF16)],
        compiler_params=_params(("arbitrary",)),
        name="in_proj_sample",
    )(x.reshape(n, d_model), *shared_args, coef, bias, g_out_a[None])


def _score_biases():
    row = (np.arange(2 * CHUNK) % CHUNK)[:, None]
    col = np.arange(2 * CHUNK)[None, :]
    in_band = (col >= row) & (col <= row + CHUNK)
    band = np.stack([in_band & (col >= CHUNK), in_band])
    causal = col[:, :CHUNK] <= row
    to_bias = lambda ok: jnp.asarray(np.where(ok, 0.0, -np.inf), F32)
    return to_bias(band), to_bias(causal)


def _prompt_attention(q, k, v):
    batch, _, seq, _ = q.shape
    n_blocks = seq // CHUNK
    assert all(n_blocks % dil == 0 for _, dil in DILATED_PATTERNS)
    spec = pl.BlockSpec((None, N_SLABS, seq, LANES), lambda b: (b, 0, 0, 0))
    band, causal = _score_biases()
    far_tiles = N_SLABS * DILATED_PATTERNS[2][1] // DILATED_PATTERNS[1][1]
    return pl.pallas_call(
        functools.partial(_prompt_attn_kernel, n_blocks=n_blocks),
        grid=(batch,),
        in_specs=[spec, spec, spec, _const_spec(band.shape), _const_spec(causal.shape)],
        out_specs=spec,
        out_shape=jax.ShapeDtypeStruct(q.shape, F32),
        scratch_shapes=[pltpu.VMEM((N_SLABS, seq, LANES), F32), pltpu.VMEM((N_SLABS, seq, LANES), F32),
                        pltpu.VMEM((N_SLABS, 2 * CHUNK, 2 * CHUNK), F32),
                        pltpu.VMEM((N_SLABS, 2 * CHUNK, 2 * CHUNK), BF16),
                        pltpu.VMEM((6, N_SLABS, seq // DILATED_PATTERNS[1][1], LANES), F32),
                        pltpu.VMEM((far_tiles, 2 * CHUNK, CHUNK), F32), pltpu.VMEM((far_tiles, 2 * CHUNK, CHUNK), BF16)],
        compiler_params=_params(("parallel",)),
        name="prompt_attention",
    )(q, k, v, band, causal)


def _sample_masks(seq, cache_len, past_len, n_rows):
    cache = np.zeros((n_rows, cache_len), np.float32)
    new = [[False] * n_rows for _ in range(seq)]
    for b, (window, dil) in enumerate(DILATED_PATTERNS):
        for t in range(seq):
            for jj in range(window // dil + 1):
                idx = cache_len + t - dil * jj
                if idx < 0 or idx + (past_len - cache_len) < 0:
                    continue
                if idx < cache_len:
                    cache[b * seq + t, idx] = 1.0
                else:
                    new[idx - cache_len][b * seq + t] = True
    return cache, new


def _sample_attention_operands(q, k, v, cache_k, cache_v):
    n_seq, seq, _ = q.shape
    cache_len = cache_k.shape[1]
    new = jnp.stack([q, k, v], axis=2).reshape(n_seq, seq, 3, N_HEADS, HEAD_DIM).transpose(0, 3, 2, 1, 4)
    new = jnp.pad(new, ((0, 0), (0, 0), (0, 0), (0, seq), (0, 0)))
    per_pair = lambda c: jnp.transpose(c, (0, 2, 3, 1)).reshape(n_seq * N_HEADS, HEAD_DIM, cache_len)
    return new.reshape(n_seq * N_HEADS, 3, 2 * seq, HEAD_DIM), per_pair(cache_k), per_pair(cache_v)


def _sample_attention_call(operands, seq, past_len, first, steps, per_step):
    new, kt, vt = operands
    n_rows = 4 * seq
    assert len(DILATED_PATTERNS) * seq <= n_rows and first % per_step == 0
    cache_len = kt.shape[2]
    mask, new_valid = _sample_masks(seq, cache_len, past_len, n_rows)
    block0 = first // per_step
    new_spec = pl.BlockSpec((per_step, 3, 2 * seq, HEAD_DIM), lambda i: (block0 + i, 0, 0, 0))
    cache_spec = pl.BlockSpec((per_step, HEAD_DIM, cache_len), lambda i: (block0 + i, 0, 0))
    return dict(
        stages=functools.partial(_sample_attn_stages, seq=seq, new_valid=new_valid),
        args=(new, kt, vt, jnp.asarray(mask)),
        in_specs=[new_spec, cache_spec, cache_spec, _const_spec((n_rows, cache_len))],
        out_specs=[pl.BlockSpec((per_step, n_rows, HEAD_DIM), lambda i: (i, 0, 0))],
        out_shapes=[jax.ShapeDtypeStruct((steps * per_step, n_rows, HEAD_DIM), F32)],
        scratch=[pltpu.VMEM((per_step, n_rows, cache_len), F32), pltpu.VMEM((per_step, n_rows, cache_len), BF16)],
        steps=steps)


def _sample_attention_slabs(outs, seq):
    out = jnp.concatenate(outs, axis=0)
    out = out.reshape(-1, N_HEADS, out.shape[1], HEAD_DIM)
    n_seq = out.shape[0]
    b_out = out[:, :, :seq, :].transpose(0, 2, 1, 3).reshape(n_seq * seq, N_SLABS, LANES)
    return b_out.transpose(1, 0, 2)[None]


def _finish_call(x2d, mix_a, b_slab, g_out_b, w_o, g_ffn, w_gate, w_up, w_down, *, tm):
    n, d_model = x2d.shape
    nt = b_slab.shape[2] // tm
    d_ff = w_gate.shape[1]
    return dict(
        stages=_finish_stages,
        args=(x2d, mix_a, b_slab, g_out_b[None], w_o.astype(BF16), g_ffn[None],
              w_gate.astype(BF16), w_up.astype(BF16), w_down.astype(BF16)),
        in_specs=[pl.BlockSpec((tm, d_model), lambda i: (i, 0)),
                  pl.BlockSpec((tm, WIDTH), lambda i: (i, 0)),
                  pl.BlockSpec((None, N_SLABS, tm, LANES), lambda i: (i // nt, 0, i % nt, 0)),
                  _const_spec((1, WIDTH)), _const_spec((2 * WIDTH, d_model)), _const_spec((1, d_model)),
                  _const_spec((d_model, d_ff)), _const_spec((d_model, d_ff)), _const_spec((d_ff, d_model))],
        out_specs=[pl.BlockSpec((tm, d_model), lambda i: (i, 0))],
        out_shapes=[jax.ShapeDtypeStruct((n, d_model), F32)],
        scratch=[], steps=n // tm)


def kernel(x_prompt, x_sample, cache_k, cache_v, g_attn, w_in, ln_v_g, ln_v_b, w_s, b_s, g_q, g_k,
           g_out_a, g_out_b, w_o, g_ffn, w_gate, w_up, w_down):
    depth = w_in.shape[0]
    assert depth == 1, "single-layer step"
    batch, seq, d_model = x_prompt.shape
    n_seq, dec_seq, _ = x_sample.shape
    l = 0
    shared = _shared_in_proj_args(g_attn[l], w_in[l], ln_v_g[l], ln_v_b[l], g_q[l], g_k[l])
    fin = (g_out_b[l], w_o[l], g_ffn[l], w_gate[l], w_up[l], w_down[l])
    n = batch * seq

    qs, ks, vs, vns, mix_as = _in_proj_sample(x_sample, shared, w_s[l], b_s[l], g_out_a[l])
    as3 = lambda a: a.reshape(n_seq, dec_seq, WIDTH)
    att_ops = _sample_attention_operands(as3(qs), as3(ks), as3(vs), cache_k[l], cache_v[l])
    (tm_in, pairs_in), (tm_fin, pairs_fin) = IN_PROJ_HOSTING, FINISH_HOSTING
    fin_pairs = (n // tm_fin) * pairs_fin
    assert fin_pairs + (n // tm_in) * pairs_in == n_seq * N_HEADS

    def hosted(host, first, per_step, name):
        att = _sample_attention_call(att_ops, dec_seq, PAST_LEN, first, host["steps"], per_step)
        host_outs, (att_out,) = _staged_call([host, att], [1, 0, 1, 0, 1], name)
        return host_outs, att_out

    in_proj = _in_proj_prompt_call(x_prompt, shared, w_s[l], b_s[l], g_out_a[l], tm=tm_in)
    (q, k, v, k_out, v_out, mix_a), att_in = hosted(in_proj, fin_pairs, pairs_in, "in_proj_prompt_and_sample_attention")
    b_slab = _prompt_attention(q, k, v)
    finish = _finish_call(x_prompt.reshape(n, d_model), mix_a, b_slab, *fin, tm=tm_fin)
    (y_prompt,), att_fin = hosted(finish, 0, pairs_fin, "finish_and_sample_attention")
    win = min(max(w for w, _ in DILATED_PATTERNS), seq)
    window = lambda a: a.reshape(batch, N_HEADS, HEAD_DIM, seq).transpose(0, 3, 1, 2)[:, -win:][None]

    bs_slab = _sample_attention_slabs([att_fin, att_in], dec_seq)
    finish_s = _finish_call(x_sample.reshape(n_seq * dec_seq, d_model), mix_as, bs_slab, *fin, tm=n_seq * dec_seq)
    ((y_sample,),) = _staged_call([finish_s], [0, 0], "finish")

    head_shape = (1, n_seq, dec_seq, N_HEADS, HEAD_DIM)
    return (y_prompt.reshape(batch, seq, d_model), y_sample.reshape(n_seq, dec_seq, d_model),
            window(k_out), window(v_out), ks.reshape(head_shape), vs.reshape(head_shape),
            vns.reshape(1, n_seq, dec_seq, WIDTH))
```
